```python
import math
import jax, jax.numpy as jnp
from jax import lax
import numpy as np

D_MODEL = 1024
BATCH = 16
SEQ = 2048
DEPTH = 1

CHUNK = 64
Q_BLOCK = 128
D_MIX = D_MODEL
LRU_WIDTH = D_MIX // 2
LRU_BLOCKS = 8
LRU_BLOCK_W = LRU_WIDTH // LRU_BLOCKS
CONV_W = 4
LRU_C = 8.0
ATT_WIDTH = D_MIX - LRU_WIDTH
N_HEADS = 4
HEAD_DIM = ATT_WIDTH // (2 * N_HEADS)
V_DIM = 2 * HEAD_DIM
IN_COLS = 2 * LRU_WIDTH + 3 * ATT_WIDTH
N_GROUPS = 4
EXPERTS_PER_GROUP = 4
N_EXPERTS = N_GROUPS * EXPERTS_PER_GROUP
TOP_K_IN_GROUP = 2
D_EXPERT = D_MODEL // 2
EPS = 1e-6
NEG_BIG = -1e30

kernel_name = "hybrid_rglru_diffattn_hiermoe"


def rms_norm(x, g):
    xf = x.astype(jnp.float32)
    y = xf * lax.rsqrt(jnp.mean(xf * xf, axis=-1, keepdims=True) + EPS)
    return (y * g.astype(jnp.float32)).astype(x.dtype)


def alibi_slopes(n_heads):
    return jnp.exp2(-8.0 * jnp.arange(1, n_heads + 1, dtype=jnp.float32) / n_heads)


def causal_depthwise_conv(x, w, b):
    S = x.shape[1]
    xp = jnp.pad(x, ((0, 0), (CONV_W - 1, 0), (0, 0)))
    y = b
    for j in range(CONV_W):
        y = y + xp[:, j:j + S] * w[j]
    return y


def rg_lru(xc, w_a, b_a, w_x, b_x, lam):
    B, S, _ = xc.shape
    f32 = jnp.float32
    xf = xc.astype(f32)
    xb = xf.reshape(B, S, LRU_BLOCKS, LRU_BLOCK_W)
    r = jax.nn.sigmoid(jnp.einsum('bsnc,ncd->bsnd', xb, w_a.astype(f32)).reshape(B, S, LRU_WIDTH) + b_a.astype(f32))
    i = jax.nn.sigmoid(jnp.einsum('bsnc,ncd->bsnd', xb, w_x.astype(f32)).reshape(B, S, LRU_WIDTH) + b_x.astype(f32))
    log_a = -LRU_C * r * jax.nn.softplus(-lam.astype(f32))
    a = jnp.exp(log_a)
    u = jnp.sqrt(-jnp.expm1(2.0 * log_a)) * (i * xf)

    def combine(left, right):
        a1, b1 = left
        a2, b2 = right
        return a1 * a2, a2 * b1 + b2

    _, h = lax.associative_scan(combine, (a, u), axis=1)
    return h.astype(xc.dtype)


def diff_attention(q, k, v, lam, slopes):
    S = q.shape[1]
    f32 = jnp.float32
    scale = HEAD_DIM ** -0.5
    outs = []
    for blk in range(S // Q_BLOCK):
        q0 = blk * Q_BLOCK
        q1 = q0 + Q_BLOCK
        qb = q[:, q0:q1].astype(f32)
        kb = k[:, :q1].astype(f32)
        vb = v[:, :q1].astype(f32)
        s = jnp.einsum('bqhcd,bkhcd->bhcqk', qb, kb) * scale
        tq = jnp.arange(q0, q1)
        tk = jnp.arange(q1)
        dist = jnp.abs(tq[:, None] - tk[None, :]).astype(f32)
        allowed = (tk[None, :] // CHUNK) <= (tq[:, None] // CHUNK)
        bias = jnp.where(allowed[None], -slopes[:, None, None] * dist[None], NEG_BIG)
        p = jax.nn.softmax(s + bias[None, :, None], axis=-1)
        w = p[:, :, 0] - lam * p[:, :, 1]
        outs.append(jnp.einsum('bhqk,bkhe->bqhe', w, vb))
    return jnp.concatenate(outs, axis=1)


def hier_moe(xt, w_rg, b_rg, w_re, b_re, w_g, w_u, w_d):
    f32 = jnp.float32
    xf = xt.astype(f32)
    p_group = jax.nn.softmax(xf @ w_rg.astype(f32) + b_rg.astype(f32), axis=-1)
    g_gate, g_idx = lax.top_k(p_group, 1)
    e_logits = jnp.einsum('td,gde->tge', xf, w_re.astype(f32)) + b_re.astype(f32)
    sel = jnp.take_along_axis(e_logits, g_idx[:, :, None], axis=1)[:, 0]
    p_exp = jax.nn.softmax(sel, axis=-1)
    e_w, e_idx = lax.top_k(p_exp, TOP_K_IN_GROUP)
    e_w = e_w / jnp.sum(e_w, axis=-1, keepdims=True)
    gates = g_gate * e_w
    expert_id = g_idx * EXPERTS_PER_GROUP + e_idx
    dense_gates = jnp.einsum('tk,tke->te', gates, jax.nn.one_hot(expert_id, N_EXPERTS, dtype=f32))
    y = jnp.zeros_like(xf)
    for e in range(N_EXPERTS):
        hid = jax.nn.silu(xt @ w_g[e]) * (xt @ w_u[e])
        y = y + dense_gates[:, e:e + 1] * (hid @ w_d[e]).astype(f32)
    return y.astype(xt.dtype)


def setup_inputs(seed: int = 0) -> dict:
    key = jax.random.key(seed)
    ks = jax.random.split(key, 32)
    f32 = jnp.float32
    L = DEPTH

    def nrm(k, shape, std):
        return std * jax.random.normal(k, shape, f32)

    u = jax.random.uniform(ks[9], (L, LRU_WIDTH), f32, 0.9, 0.999)
    sig = u ** (1.0 / LRU_C)
    lru_lambda = jnp.log(sig) - jnp.log1p(-sig)
    return {
        'x': nrm(ks[0], (BATCH, SEQ, D_MODEL), 1.0),
        'norm1_g': 1.0 + nrm(ks[1], (L, D_MODEL), 0.02),
        'w_in': nrm(ks[2], (L, D_MODEL, IN_COLS), D_MODEL ** -0.5),
        'conv_w': nrm(ks[3], (L, CONV_W, LRU_WIDTH), CONV_W ** -0.5),
        'conv_b': nrm(ks[4], (L, LRU_WIDTH), 0.01),
        'w_gate_a': nrm(ks[5], (L, LRU_BLOCKS, LRU_BLOCK_W, LRU_BLOCK_W), LRU_BLOCK_W ** -0.5),
        'b_gate_a': nrm(ks[6], (L, LRU_WIDTH), 0.01),
        'w_gate_x': nrm(ks[7], (L, LRU_BLOCKS, LRU_BLOCK_W, LRU_BLOCK_W), LRU_BLOCK_W ** -0.5),
        'b_gate_x': nrm(ks[8], (L, LRU_WIDTH), 0.01),
        'lru_lambda': lru_lambda,
        'lru_out_g': 1.0 + nrm(ks[10], (L, LRU_WIDTH), 0.02),
        'q_norm_g': 1.0 + nrm(ks[11], (L, HEAD_DIM), 0.02),
        'k_norm_g': 1.0 + nrm(ks[12], (L, HEAD_DIM), 0.02),
        'lambda_q1': nrm(ks[13], (L, HEAD_DIM), 0.1),
        'lambda_k1': nrm(ks[14], (L, HEAD_DIM), 0.1),
        'lambda_q2': nrm(ks[15], (L, HEAD_DIM), 0.1),
        'lambda_k2': nrm(ks[16], (L, HEAD_DIM), 0.1),
        'sub_norm_g': 1.0 + nrm(ks[17], (L, V_DIM), 0.02),
        'w_out': nrm(ks[18], (L, D_MIX, D_MODEL), D_MIX ** -0.5),
        'norm2_g': 1.0 + nrm(ks[19], (L, D_MODEL), 0.02),
        'w_router_group': nrm(ks[20], (L, D_MODEL, N_GROUPS), D_MODEL ** -0.5),
        'b_router_group': nrm(ks[21], (L, N_GROUPS), 0.01),
        'w_router_expert': nrm(ks[22], (L, N_GROUPS, D_MODEL, EXPERTS_PER_GROUP), D_MODEL ** -0.5),
        'b_router_expert': nrm(ks[23], (L, N_GROUPS, EXPERTS_PER_GROUP), 0.01),
        'w_expert_gate': nrm(ks[24], (L, N_EXPERTS, D_MODEL, D_EXPERT), D_MODEL ** -0.5),
        'w_expert_up': nrm(ks[25], (L, N_EXPERTS, D_MODEL, D_EXPERT), D_MODEL ** -0.5),
        'w_expert_down': nrm(ks[26], (L, N_EXPERTS, D_EXPERT, D_MODEL), D_EXPERT ** -0.5),
    }


def reference(x, norm1_g, w_in, conv_w, conv_b, w_gate_a, b_gate_a, w_gate_x, b_gate_x,
              lru_lambda, lru_out_g, q_norm_g, k_norm_g, lambda_q1, lambda_k1, lambda_q2,
              lambda_k2, sub_norm_g, w_out, norm2_g, w_router_group, b_router_group,
              w_router_expert, b_router_expert, w_expert_gate, w_expert_up, w_expert_down):
    B, S, D = x.shape
    f32 = jnp.float32
    slopes = alibi_slopes(N_HEADS)
    splits = [LRU_WIDTH, 2 * LRU_WIDTH, 2 * LRU_WIDTH + ATT_WIDTH, 2 * LRU_WIDTH + 2 * ATT_WIDTH]
    h = x
    for l in range(DEPTH):
        lambda_init = 0.8 - 0.6 * math.exp(-0.3 * l)
        hn = rms_norm(h, norm1_g[l])
        proj = hn @ w_in[l]
        x_lru, g_lru, q, k, v = jnp.split(proj, splits, axis=-1)
        xc = causal_depthwise_conv(x_lru, conv_w[l], conv_b[l])
        y_lru = rg_lru(xc, w_gate_a[l], b_gate_a[l], w_gate_x[l], b_gate_x[l], lru_lambda[l]) * jax.nn.gelu(g_lru)
        y_lru = rms_norm(y_lru, lru_out_g[l])
        q = rms_norm(q.reshape(B, S, N_HEADS, 2, HEAD_DIM), q_norm_g[l])
        k = rms_norm(k.reshape(B, S, N_HEADS, 2, HEAD_DIM), k_norm_g[l])
        v = v.reshape(B, S, N_HEADS, V_DIM)
        lam = (jnp.exp(jnp.sum(lambda_q1[l].astype(f32) * lambda_k1[l].astype(f32)))
               - jnp.exp(jnp.sum(lambda_q2[l].astype(f32) * lambda_k2[l].astype(f32)))
               + lambda_init)
        o = diff_attention(q, k, v, lam, slopes)
        o = rms_norm(o, sub_norm_g[l]) * (1.0 - lambda_init)
        y_att = o.reshape(B, S, ATT_WIDTH).astype(h.dtype)
        h = h + jnp.concatenate([y_lru.astype(h.dtype), y_att], axis=-1) @ w_out[l]
        hn2 = rms_norm(h, norm2_g[l]).reshape(B * S, D)
        h = h + hier_moe(hn2, w_router_group[l], b_router_group[l], w_router_expert[l],
                         b_router_expert[l], w_expert_gate[l], w_expert_up[l],
                         w_expert_down[l]).reshape(B, S, D)
    return h
```

```python
import functools
import math

import numpy as np
import jax
import jax.numpy as jnp
from jax import lax
from jax.experimental import pallas as pl
from jax.experimental.pallas import tpu as pltpu

F32 = jnp.float32
BF16 = jnp.bfloat16
I32 = jnp.int32
U32 = jnp.uint32

D_MODEL = 1024
LRU_WIDTH = 512
LRU_BLOCKS = 8
LRU_BLOCK_W = LRU_WIDTH // LRU_BLOCKS
CONV_W = 4
LRU_C = 8.0
ATT_WIDTH = 512
N_HEADS = 4
HEAD_DIM = 64
V_DIM = 128
IN_COLS = 2 * LRU_WIDTH + 3 * ATT_WIDTH
N_GROUPS = 4
EXPERTS_PER_GROUP = 4
N_EXPERTS = N_GROUPS * EXPERTS_PER_GROUP
D_EXPERT = D_MODEL // 2
CHUNK = 64
EPS = 1e-6
NEG_BIG = -1e30
LAMBDA_INIT = 0.8 - 0.6 * math.exp(-0.3 * 0)

LANES = 128
SUBLANES = 8
HALF = D_MODEL // 2

TM_PROJ = 512
TS_LRU = 256
TQ = 256
TM_TOK = 256
TM_EXP = 256
ROW_CHUNK = SUBLANES
LOC = 2 * TM_TOK + N_EXPERTS * ROW_CHUNK
VMEM_LIMIT = 56 * 1024 * 1024


def _cparams(n_axes):
    return pltpu.CompilerParams(
        dimension_semantics=("arbitrary",) * n_axes, vmem_limit_bytes=VMEM_LIMIT)


def _inproj_body(x_ref, g1_ref, w_ref, qkg_ref, o_ref):
    x = x_ref[...]
    ms = jnp.mean(x * x, axis=-1, keepdims=True)
    hn = ((x * lax.rsqrt(ms + EPS)) * g1_ref[...]).astype(BF16)
    tm = x.shape[0]
    lo_half = lax.broadcasted_iota(I32, (tm, LANES), 1) < HEAD_DIM
    width = 512
    for c in range(IN_COLS // width):
        c0 = c * width
        acc = jnp.dot(hn, w_ref[:, c0:c0 + width], preferred_element_type=F32)
        if c in (2, 3):
            gain = qkg_ref[c - 2:c - 1, :]
            for b in range(width // LANES):
                blk = acc[:, b * LANES:(b + 1) * LANES]
                sq = blk * blk
                s_lo = jnp.sum(jnp.where(lo_half, sq, 0.0), axis=-1, keepdims=True)
                s_hi = jnp.sum(jnp.where(lo_half, 0.0, sq), axis=-1, keepdims=True)
                inv = jnp.where(lo_half,
                                lax.rsqrt(s_lo * (1.0 / HEAD_DIM) + EPS),
                                lax.rsqrt(s_hi * (1.0 / HEAD_DIM) + EPS))
                o_ref[:, c0 + b * LANES:c0 + (b + 1) * LANES] = ((blk * inv) * gain).astype(BF16)
        else:
            o_ref[:, c0:c0 + width] = acc.astype(BF16)


def _in_proj(x2, g1, w_in_bf, qkg):
    t = x2.shape[0]
    tm = min(TM_PROJ, t)
    return pl.pallas_call(
        _inproj_body,
        grid=(t // tm,),
        in_specs=[
            pl.BlockSpec((tm, D_MODEL), lambda i: (i, 0)),
            pl.BlockSpec((1, D_MODEL), lambda i: (0, 0)),
            pl.BlockSpec((D_MODEL, IN_COLS), lambda i: (0, 0)),
            pl.BlockSpec((2, LANES), lambda i: (0, 0)),
        ],
        out_specs=pl.BlockSpec((tm, IN_COLS), lambda i: (i, 0)),
        out_shape=jax.ShapeDtypeStruct((t, IN_COLS), BF16),
        compiler_params=_cparams(1),
        name="in_proj",
    )(x2, g1, w_in_bf, qkg)


def _lru_body(p_ref, cw_ref, cb_ref, wg_ref, bg_ref, lam_ref, og_ref, o_ref, xbuf, hbuf, hc):
    s = pl.program_id(1)
    ts = p_ref.shape[0]

    @pl.when(s == 0)
    def _():
        xbuf[0:SUBLANES, :] = jnp.zeros((SUBLANES, LRU_WIDTH), F32)
        hc[...] = jnp.zeros((1, LRU_WIDTH), F32)

    x = p_ref[:, 0:LRU_WIDTH].astype(F32)
    gl = p_ref[:, LRU_WIDTH:2 * LRU_WIDTH].astype(F32)
    xbuf[SUBLANES:SUBLANES + ts, :] = x
    xc = cb_ref[...]
    for j in range(CONV_W):
        r0 = SUBLANES - (CONV_W - 1) + j
        xc = xc + xbuf[r0:r0 + ts, :] * cw_ref[j:j + 1, :]
    xbuf[0:SUBLANES, :] = x[ts - SUBLANES:ts, :]

    z = jnp.dot(xc.astype(BF16), wg_ref[...], preferred_element_type=F32) + bg_ref[...]
    r = jax.nn.sigmoid(z[:, 0:LRU_WIDTH])
    gi = jax.nn.sigmoid(z[:, LRU_WIDTH:2 * LRU_WIDTH])
    nl = -lam_ref[...]
    softplus = jnp.maximum(nl, 0.0) + jnp.log1p(jnp.exp(-jnp.abs(nl)))
    log_a = (-LRU_C) * r * softplus
    a = jnp.exp(log_a)
    u = jnp.sqrt(-jnp.tanh(log_a) * (a * a + 1.0)) * (gi * xc)

    row = lax.broadcasted_iota(I32, (ts, LRU_WIDTH), 0) & (SUBLANES - 1)
    ca, cb = a, u
    for d in (1, 2, 4):
        a_sh = pltpu.roll(ca, d, axis=0)
        b_sh = pltpu.roll(cb, d, axis=0)
        take = row >= d
        cb = jnp.where(take, ca * b_sh + cb, cb)
        ca = jnp.where(take, ca * a_sh, ca)
    h = hc[...]
    for blk in range(ts // SUBLANES):
        r0 = blk * SUBLANES
        hb = ca[r0:r0 + SUBLANES, :] * h + cb[r0:r0 + SUBLANES, :]
        hbuf[r0:r0 + SUBLANES, :] = hb
        h = hb[SUBLANES - 1:SUBLANES, :]
    hc[...] = h

    y = hbuf[...] * jax.nn.gelu(gl)
    ms = jnp.mean(y * y, axis=-1, keepdims=True)
    o_ref[...] = ((y * lax.rsqrt(ms + EPS)) * og_ref[...]).astype(o_ref.dtype)


def _lru(proj, b, s, conv_w, conv_b, wgate, bgate, lam, out_g):
    ts = min(TS_LRU, s)
    ns = s // ts
    vec = lambda n: pl.BlockSpec((1, n), lambda bi, si: (0, 0))
    return pl.pallas_call(
        _lru_body,
        grid=(b, ns),
        in_specs=[
            pl.BlockSpec((ts, 2 * LRU_WIDTH), lambda bi, si: (bi * ns + si, 0)),
            pl.BlockSpec((CONV_W, LRU_WIDTH), lambda bi, si: (0, 0)),
            vec(LRU_WIDTH),
            pl.BlockSpec((LRU_WIDTH, 2 * LRU_WIDTH), lambda bi, si: (0, 0)),
            vec(2 * LRU_WIDTH),
            vec(LRU_WIDTH),
            vec(LRU_WIDTH),
        ],
        out_specs=pl.BlockSpec((ts, LRU_WIDTH), lambda bi, si: (bi * ns + si, 0)),
        out_shape=jax.ShapeDtypeStruct((b * s, LRU_WIDTH), BF16),
        scratch_shapes=[
            pltpu.VMEM((ts + SUBLANES, LRU_WIDTH), F32),
            pltpu.VMEM((ts, LRU_WIDTH), F32),
            pltpu.VMEM((1, LRU_WIDTH), F32),
        ],
        compiler_params=_cparams(2),
        name="rg_lru",
    )(proj, conv_w, conv_b, wgate, bgate, lam, out_g)


def _attn_body(slope_ref, q_ref, k_ref, v_ref, bias_ref, lamp_ref, sg_ref, o_ref):
    h = pl.program_id(1)
    i = pl.program_id(2)
    tq = q_ref.shape[0]
    q = q_ref[...]
    lo_half = lax.broadcasted_iota(I32, (tq, LANES), 1) < HEAD_DIM
    zero = jnp.zeros_like(q)
    qs = jnp.concatenate([jnp.where(lo_half, q, zero), jnp.where(lo_half, zero, q)], axis=0)
    slope = slope_ref[h]

    def tile(j, carry, bias, shift):
        m, l, acc = carry
        k0 = pl.multiple_of(j * tq, tq)
        kt = k_ref[pl.ds(k0, tq), :]
        vt = v_ref[pl.ds(k0, tq), :]
        sb = lax.dot_general(qs, kt, (((1,), (1,)), ((), ())), preferred_element_type=F32) + bias
        m_new = jnp.maximum(m, jnp.max(sb, axis=-1, keepdims=True) + shift)
        p = jnp.exp(sb - (m_new - shift))
        alpha = jnp.exp(m - m_new)
        l_new = alpha * l + jnp.sum(p, axis=-1, keepdims=True)
        acc_new = alpha * acc + jnp.dot(p.astype(BF16), vt, preferred_element_type=F32)
        return m_new, l_new, acc_new

    def off_diag(j, carry):
        shift = -slope * ((i - j) * tq).astype(F32)
        return tile(j, carry, bias_ref[0, 0], shift)

    init = (jnp.full((2 * tq, 1), -jnp.inf, F32),
            jnp.zeros((2 * tq, 1), F32),
            jnp.zeros((2 * tq, V_DIM), F32))
    carry = lax.fori_loop(0, i, off_diag, init)
    m, l, acc = tile(i, carry, bias_ref[0, 1], jnp.float32(0.0))

    lp = lamp_ref[...]
    lam = (jnp.exp(jnp.sum(lp[0:1, :] * lp[1:2, :], axis=-1, keepdims=True))
           - jnp.exp(jnp.sum(lp[2:3, :] * lp[3:4, :], axis=-1, keepdims=True))
           + LAMBDA_INIT)
    o = acc[0:tq, :] / l[0:tq, :] - lam * (acc[tq:2 * tq, :] / l[tq:2 * tq, :])
    ms = jnp.mean(o * o, axis=-1, keepdims=True)
    o = ((o * lax.rsqrt(ms + EPS)) * sg_ref[...]) * (1.0 - LAMBDA_INIT)
    o_ref[...] = o.astype(o_ref.dtype)


def _alibi_tables(tq):
    slopes = np.exp2(-8.0 * np.arange(1, N_HEADS + 1, dtype=np.float64) / N_HEADS)
    qi = np.arange(tq)[:, None]
    kj = np.arange(tq)[None, :]
    off = -(slopes[:, None, None] * (qi - kj)[None])
    allowed = (kj // CHUNK) <= (qi // CHUNK)
    diag = np.where(allowed[None], -(slopes[:, None, None] * np.abs(qi - kj)[None]), NEG_BIG)
    tab = np.stack([off, diag], axis=1)
    tab = np.concatenate([tab, tab], axis=2)
    return jnp.asarray(tab, F32), jnp.asarray(slopes, F32)


def _attention(proj, b, s, lam_params, sub_g):
    tq = min(TQ, s)
    nq = s // tq
    bias, slopes = _alibi_tables(tq)
    qcol = 2 * LRU_WIDTH // LANES
    kcol = qcol + ATT_WIDTH // LANES
    vcol = kcol + ATT_WIDTH // LANES
    grid_spec = pltpu.PrefetchScalarGridSpec(
        num_scalar_prefetch=1,
        grid=(b, N_HEADS, nq),
        in_specs=[
            pl.BlockSpec((tq, LANES), lambda bi, h, i, sl: (bi * nq + i, qcol + h)),
            pl.BlockSpec((s, LANES), lambda bi, h, i, sl: (bi, kcol + h)),
            pl.BlockSpec((s, LANES), lambda bi, h, i, sl: (bi, vcol + h)),
            pl.BlockSpec((1, 2, 2 * tq, tq), lambda bi, h, i, sl: (h, 0, 0, 0)),
            pl.BlockSpec((4, HEAD_DIM), lambda bi, h, i, sl: (0, 0)),
            pl.BlockSpec((1, V_DIM), lambda bi, h, i, sl: (0, 0)),
        ],
        out_specs=pl.BlockSpec((tq, V_DIM), lambda bi, h, i, sl: (bi * nq + i, h)),
    )
    return pl.pallas_call(
        _attn_body,
        grid_spec=grid_spec,
        out_shape=jax.ShapeDtypeStruct((b * s, ATT_WIDTH), BF16),
        compiler_params=_cparams(3),
        name="diff_attn",
    )(slopes, proj, proj, proj, bias, lam_params, sub_g)


def _pack_rows(lo, hi):
    lo_b = lax.bitcast_convert_type(lo, U32)
    hi_b = lax.bitcast_convert_type(hi, U32)
    return (lo_b >> 16) | (hi_b & jnp.uint32(0xFFFF0000))


def _unpack_rows(w):
    lo = lax.bitcast_convert_type(w << 16, F32).astype(BF16)
    hi = lax.bitcast_convert_type(w & jnp.uint32(0xFFFF0000), F32).astype(BF16)
    return lo, hi


def _first_max4(v0, v1, v2, v3):
    m = jnp.maximum(jnp.maximum(v0, v1), jnp.maximum(v2, v3))
    idx = jnp.where(v0 == m, 0, jnp.where(v1 == m, 1, jnp.where(v2 == m, 2, 3))).astype(I32)
    return m, idx


def _route_body(yl_ref, ya_ref, x_ref, wo_ref, g2_ref, wrh_ref, wrl_ref, br_ref,
                h1_ref, xl_ref, aux_ref, gt_ref):
    tm = x_ref.shape[0]
    mix = (jnp.dot(yl_ref[...], wo_ref[0:LRU_WIDTH, :], preferred_element_type=F32)
           + jnp.dot(ya_ref[...], wo_ref[LRU_WIDTH:, :], preferred_element_type=F32))
    h1 = x_ref[...] + mix
    h1_ref[...] = h1
    ms = jnp.mean(h1 * h1, axis=-1, keepdims=True)
    hn = (h1 * lax.rsqrt(ms + EPS)) * g2_ref[...]
    hn_hi = hn.astype(BF16)
    hn_lo = (hn - hn_hi.astype(F32)).astype(BF16)
    logits = (jnp.dot(hn_hi, wrh_ref[...], preferred_element_type=F32)
              + jnp.dot(hn_hi, wrl_ref[...], preferred_element_type=F32)
              + jnp.dot(hn_lo, wrh_ref[...], preferred_element_type=F32)) + br_ref[...]
    lt = logits.T
    row = lambda n: lt[n:n + 1, :]
    gmax, gidx = _first_max4(row(0), row(1), row(2), row(3))
    zg = (jnp.exp(row(0) - gmax) + jnp.exp(row(1) - gmax)
          + jnp.exp(row(2) - gmax) + jnp.exp(row(3) - gmax))
    g_gate = 1.0 / zg
    base = N_GROUPS
    sel = [jnp.where(gidx == 0, row(base + j),
                     jnp.where(gidx == 1, row(base + 4 + j),
                               jnp.where(gidx == 2, row(base + 8 + j), row(base + 12 + j))))
           for j in range(EXPERTS_PER_GROUP)]
    m1, i1 = _first_max4(*sel)
    ze = sum(jnp.exp(sj - m1) for sj in sel)
    rest = [jnp.where(i1 == j, -jnp.inf, sel[j]) for j in range(EXPERTS_PER_GROUP)]
    m2, i2 = _first_max4(*rest)
    p1 = 1.0 / ze
    p2 = jnp.exp(m2 - m1) / ze
    gate1 = g_gate * (p1 / (p1 + p2))
    gate2 = g_gate * (p2 / (p1 + p2))
    e1 = gidx * EXPERTS_PER_GROUP + i1
    e2 = gidx * EXPERTS_PER_GROUP + i2

    eio = lax.broadcasted_iota(I32, (N_EXPERTS, tm), 0)
    oh1 = eio == e1
    oh2 = eio == e2
    both = (oh1 | oh2).astype(F32)
    cnt = jnp.sum(both, axis=1, keepdims=True)
    grp = jnp.floor((cnt + (ROW_CHUNK - 1)) * (1.0 / ROW_CHUNK)) * ROW_CHUNK
    ti = lax.broadcasted_iota(I32, (tm, tm), 0)
    tj = lax.broadcasted_iota(I32, (tm, tm), 1)
    before = (ti < tj).astype(BF16)
    rank = jnp.dot(both.astype(BF16), before, preferred_element_type=F32)
    start1 = jnp.sum(jnp.where(eio < e1, grp, 0.0), axis=0, keepdims=True)
    start2 = jnp.sum(jnp.where(eio < e2, grp, 0.0), axis=0, keepdims=True)
    slot1 = start1 + jnp.sum(jnp.where(oh1, rank, 0.0), axis=0, keepdims=True)
    slot2 = start2 + jnp.sum(jnp.where(oh2, rank, 0.0), axis=0, keepdims=True)

    loc = xl_ref.shape[0]
    sio = lax.broadcasted_iota(I32, (loc, tm), 0)
    perm = ((sio == slot1.astype(I32)) | (sio == slot2.astype(I32))).astype(BF16)
    xs = jnp.dot(perm, hn_hi, preferred_element_type=F32)
    xl_ref[...] = _pack_rows(xs[:, 0:HALF], xs[:, HALF:])

    rio = lax.broadcasted_iota(I32, (LANES, tm), 0)
    aux_t = jnp.where(rio == 0, slot1, jnp.where(rio == 1, slot2,
                      jnp.where(rio == 2, gate1, jnp.where(rio == 3, gate2, 0.0))))
    aux_ref[...] = aux_t.T
    gt_ref[0] = jnp.broadcast_to(grp, (N_EXPERTS, LANES))


def _out_route(y_lru, y_att, x2, wo_bf, g2, wr_hi, wr_lo, br):
    t = x2.shape[0]
    tm = min(TM_TOK, t)
    nt = t // tm
    loc = 2 * tm + N_EXPERTS * ROW_CHUNK
    const = lambda shape: pl.BlockSpec(shape, lambda i: (0,) * len(shape))
    return pl.pallas_call(
        _route_body,
        grid=(nt,),
        in_specs=[
            pl.BlockSpec((tm, LRU_WIDTH), lambda i: (i, 0)),
            pl.BlockSpec((tm, ATT_WIDTH), lambda i: (i, 0)),
            pl.BlockSpec((tm, D_MODEL), lambda i: (i, 0)),
            const((D_MODEL, D_MODEL)),
            const((1, D_MODEL)),
            const((D_MODEL, LANES)),
            const((D_MODEL, LANES)),
            const((1, LANES)),
        ],
        out_specs=[
            pl.BlockSpec((tm, D_MODEL), lambda i: (i, 0)),
            pl.BlockSpec((loc, HALF), lambda i: (i, 0)),
            pl.BlockSpec((tm, LANES), lambda i: (i, 0)),
            pl.BlockSpec((1, N_EXPERTS, LANES), lambda i: (i, 0, 0)),
        ],
        out_shape=[
            jax.ShapeDtypeStruct((t, D_MODEL), F32),
            jax.ShapeDtypeStruct((nt * loc, HALF), U32),
            jax.ShapeDtypeStruct((t, LANES), F32),
            jax.ShapeDtypeStruct((nt, N_EXPERTS, LANES), F32),
        ],
        compiler_params=_cparams(1),
        name="out_route",
    )(y_lru, y_att, x2, wo_bf, g2, wr_hi, wr_lo, br)


def _regroup_body(loc_start_ref, glob_start_ref, grp_ref, nwait_ref, fill_dst_ref, fill_n_ref,
                  src_ref, zsrc_ref, dst_ref, sem, *, to_local, loc, n_fill):
    i = pl.program_id(0)
    nt = pl.num_programs(0)

    def copy(src, s_row, d_row):
        return pltpu.make_async_copy(
            src.at[pl.ds(pl.multiple_of(s_row, ROW_CHUNK), ROW_CHUNK)],
            dst_ref.at[pl.ds(pl.multiple_of(d_row, ROW_CHUNK), ROW_CHUNK)], sem)

    def start_run(src, s_row, d_row, n_chunks):
        def body(k, c):
            copy(src, s_row + k * ROW_CHUNK, d_row + k * ROW_CHUNK).start()
            return c
        lax.fori_loop(0, n_chunks, body, 0)

    def wait_chunks(n_chunks):
        def body(k, c):
            copy(zsrc_ref, 0, 0).wait()
            return c
        lax.fori_loop(0, n_chunks, body, 0)

    for e in range(N_EXPERTS):
        idx = i * N_EXPERTS + e
        l_row = i * loc + loc_start_ref[idx]
        g_row = glob_start_ref[idx]
        n_chunks = lax.shift_right_logical(grp_ref[idx], int(math.log2(ROW_CHUNK)))
        if to_local:
            start_run(src_ref, g_row, l_row, n_chunks)
        else:
            start_run(src_ref, l_row, g_row, n_chunks)

    if to_local:
        used = fill_dst_ref[i]
        start_run(zsrc_ref, i * loc + used, i * loc + used, fill_n_ref[i])
    else:
        @pl.when(i == 0)
        def _():
            for e in range(n_fill):
                start_run(zsrc_ref, loc - ROW_CHUNK, fill_dst_ref[e], fill_n_ref[e])

    @pl.when(i > 0)
    def _():
        wait_chunks(nwait_ref[i - 1])

    @pl.when(i == nt - 1)
    def _():
        wait_chunks(nwait_ref[i])


def _regroup(src, zsrc, tables, *, to_local, nt, loc, out_rows):
    loc_start, glob_start, grp, nwait, fill_dst, fill_n = tables
    body = functools.partial(_regroup_body, to_local=to_local, loc=loc, n_fill=N_EXPERTS)
    grid_spec = pltpu.PrefetchScalarGridSpec(
        num_scalar_prefetch=6,
        grid=(nt,),
        in_specs=[pl.BlockSpec(memory_space=pl.ANY), pl.BlockSpec(memory_space=pl.ANY)],
        out_specs=pl.BlockSpec(memory_space=pl.ANY),
        scratch_shapes=[pltpu.SemaphoreType.DMA],
    )
    return pl.pallas_call(
        body,
        grid_spec=grid_spec,
        out_shape=jax.ShapeDtypeStruct((out_rows, HALF), U32),
        compiler_params=_cparams(1),
        name="regroup_to_local" if to_local else "regroup_to_sorted",
    )(loc_start, glob_start, grp, nwait, fill_dst, fill_n, src, zsrc)


def _expert_body(tile_e_ref, nused_ref, x_ref, wg_ref, wu_ref, wd_ref, o_ref):
    j = pl.program_id(0)

    @pl.when(j < nused_ref[0])
    def _():
        x_lo, x_hi = _unpack_rows(x_ref[...])
        gate = (jnp.dot(x_lo, wg_ref[0, 0:HALF, :], preferred_element_type=F32)
                + jnp.dot(x_hi, wg_ref[0, HALF:, :], preferred_element_type=F32))
        up = (jnp.dot(x_lo, wu_ref[0, 0:HALF, :], preferred_element_type=F32)
              + jnp.dot(x_hi, wu_ref[0, HALF:, :], preferred_element_type=F32))
        hid = (jax.nn.silu(gate) * up).astype(BF16)
        y = jnp.dot(hid, wd_ref[0], preferred_element_type=F32)
        yb = y.astype(BF16).astype(F32)
        o_ref[...] = _pack_rows(yb[:, 0:HALF], yb[:, HALF:])


def _experts(xs, tile_e, nused, wg_bf, wu_bf, wd_bf, n_tiles):
    tile_map = lambda j, te, nu: (jnp.minimum(j, nu[0] - 1), 0)
    grid_spec = pltpu.PrefetchScalarGridSpec(
        num_scalar_prefetch=2,
        grid=(n_tiles,),
        in_specs=[
            pl.BlockSpec((TM_EXP, HALF), tile_map),
            pl.BlockSpec((1, D_MODEL, D_EXPERT), lambda j, te, nu: (te[j], 0, 0)),
            pl.BlockSpec((1, D_MODEL, D_EXPERT), lambda j, te, nu: (te[j], 0, 0)),
            pl.BlockSpec((1, D_EXPERT, D_MODEL), lambda j, te, nu: (te[j], 0, 0)),
        ],
        out_specs=pl.BlockSpec((TM_EXP, HALF), tile_map),
    )
    return pl.pallas_call(
        _expert_body,
        grid_spec=grid_spec,
        out_shape=jax.ShapeDtypeStruct((n_tiles * TM_EXP, HALF), U32),
        compiler_params=_cparams(1),
        name="experts",
    )(tile_e, nused, xs, wg_bf, wu_bf, wd_bf)


def _combine_body(h1_ref, aux_ref, yl_ref, o_ref):
    tm = h1_ref.shape[0]
    loc = yl_ref.shape[0]
    aux = aux_ref[...]
    slot1 = aux[:, 0:1].astype(I32)
    slot2 = aux[:, 1:2].astype(I32)
    sio = lax.broadcasted_iota(I32, (tm, loc), 1)
    gm = jnp.where(sio == slot1, aux[:, 2:3], 0.0) + jnp.where(sio == slot2, aux[:, 3:4], 0.0)
    gm_hi = gm.astype(BF16)
    gm_lo = (gm - gm_hi.astype(F32)).astype(BF16)
    y_lo, y_hi = _unpack_rows(yl_ref[...])
    mix = lambda y: (jnp.dot(gm_hi, y, preferred_element_type=F32)
                     + jnp.dot(gm_lo, y, preferred_element_type=F32))
    o_ref[:, 0:HALF] = h1_ref[:, 0:HALF] + mix(y_lo)
    o_ref[:, HALF:] = h1_ref[:, HALF:] + mix(y_hi)


def _combine(h1, aux, yl, nt, tm, loc):
    return pl.pallas_call(
        _combine_body,
        grid=(nt,),
        in_specs=[
            pl.BlockSpec((tm, D_MODEL), lambda i: (i, 0)),
            pl.BlockSpec((tm, LANES), lambda i: (i, 0)),
            pl.BlockSpec((loc, HALF), lambda i: (i, 0)),
        ],
        out_specs=pl.BlockSpec((tm, D_MODEL), lambda i: (i, 0)),
        out_shape=jax.ShapeDtypeStruct(h1.shape, F32),
        compiler_params=_cparams(1),
        name="combine",
    )(h1, aux, yl)


def _excl_cumsum(a, axis):
    return jnp.cumsum(a, axis=axis) - a


def _regroup_tables(grp, n_tiles, loc):
    nt = grp.shape[0]
    loc_start = _excl_cumsum(grp, 1)
    used = jnp.sum(grp, axis=1)
    col = jnp.sum(grp, axis=0)
    seg = ((col + TM_EXP - 1) // TM_EXP) * TM_EXP
    seg_end = jnp.cumsum(seg)
    off = seg_end - seg
    glob_start = off[None, :] + _excl_cumsum(grp, 0)
    tile_e = jnp.minimum(
        jnp.searchsorted(seg_end, jnp.arange(n_tiles, dtype=I32) * TM_EXP, side="right"),
        N_EXPERTS - 1).astype(I32)
    nused = (seg_end[-1:] // TM_EXP).astype(I32)
    pad_dst = (off + col).astype(I32)
    pad_n = ((seg - col) // ROW_CHUNK).astype(I32)
    chunks = (used // ROW_CHUNK).astype(I32)
    flat = lambda a: a.reshape(nt * N_EXPERTS).astype(I32)
    nwait_sorted = chunks.at[0].add(jnp.sum(pad_n))
    tail_n = ((loc - used) // ROW_CHUNK).astype(I32)
    to_sorted = (flat(loc_start), flat(glob_start), flat(grp), nwait_sorted, pad_dst, pad_n)
    to_local = (flat(loc_start), flat(glob_start), flat(grp), chunks + tail_n, used.astype(I32), tail_n)
    return to_sorted, to_local, tile_e, nused


def kernel(x, norm1_g, w_in, conv_w, conv_b, w_gate_a, b_gate_a, w_gate_x, b_gate_x, lru_lambda,
           lru_out_g, q_norm_g, k_norm_g, lambda_q1, lambda_k1, lambda_q2, lambda_k2, sub_norm_g,
           w_out, norm2_g, w_router_group, b_router_group, w_router_expert, b_router_expert,
           w_expert_gate, w_expert_up, w_expert_down):
    b, s, d = x.shape
    assert d == D_MODEL and norm1_g.shape[0] == 1
    t = b * s
    l = 0
    x2 = x.reshape(t, d)

    w_in_bf = w_in[l].astype(BF16)
    scale = HEAD_DIM ** -0.5
    qkg = jnp.stack([jnp.tile(q_norm_g[l], 2) * scale, jnp.tile(k_norm_g[l], 2)]).astype(F32)
    eye = jnp.eye(LRU_BLOCKS, dtype=F32)
    blockdiag = lambda w: jnp.einsum("ncd,nm->ncmd", w, eye).reshape(LRU_WIDTH, LRU_WIDTH)
    wgate = jnp.concatenate([blockdiag(w_gate_a[l]), blockdiag(w_gate_x[l])], axis=1).astype(BF16)
    bgate = jnp.concatenate([b_gate_a[l], b_gate_x[l]])[None, :]
    lam_params = jnp.stack([lambda_q1[l], lambda_k1[l], lambda_q2[l], lambda_k2[l]])
    wr = jnp.concatenate(
        [w_router_group[l], jnp.transpose(w_router_expert[l], (1, 0, 2)).reshape(d, N_EXPERTS)], axis=1)
    wr = jnp.pad(wr, ((0, 0), (0, LANES - wr.shape[1])))
    wr_hi = wr.astype(BF16)
    wr_lo = (wr - wr_hi.astype(F32)).astype(BF16)
    br = jnp.pad(jnp.concatenate([b_router_group[l], b_router_expert[l].reshape(-1)]),
                 (0, LANES - N_GROUPS - N_EXPERTS))[None, :]

    proj = _in_proj(x2, norm1_g[l][None, :], w_in_bf, qkg)
    y_lru = _lru(proj, b, s, conv_w[l], conv_b[l][None, :], wgate, bgate,
                 lru_lambda[l][None, :], lru_out_g[l][None, :])
    y_att = _attention(proj, b, s, lam_params, sub_norm_g[l][None, :])
    h1, xl, aux, gt = _out_route(y_lru, y_att, x2, w_out[l].astype(BF16), norm2_g[l][None, :],
                                 wr_hi, wr_lo, br)

    tm = min(TM_TOK, t)
    nt = t // tm
    loc = 2 * tm + N_EXPERTS * ROW_CHUNK
    max_rows = 2 * t + nt * N_EXPERTS * (ROW_CHUNK - 1) + N_EXPERTS * (TM_EXP - ROW_CHUNK)
    n_tiles = -(-max_rows // TM_EXP)
    grp = gt[:, :, 0].astype(I32)
    to_sorted, to_local, tile_e, nused = _regroup_tables(grp, n_tiles, loc)

    xs = _regroup(xl, xl, to_sorted, to_local=False, nt=nt, loc=loc, out_rows=n_tiles * TM_EXP)
    ys = _experts(xs, tile_e, nused, w_expert_gate[l].astype(BF16), w_expert_up[l].astype(BF16),
                  w_expert_down[l].astype(BF16), n_tiles)
    yl = _regroup(ys, xl, to_local, to_local=True, nt=nt, loc=loc, out_rows=nt * loc)
    out = _combine(h1, aux, yl, nt, tm, loc)
    return out.reshape(b, s, d)
```

```python
import math

import numpy as np
import jax
import jax.numpy as jnp
from jax import lax
from jax.experimental import pallas as pl
from jax.experimental.pallas import tpu as pltpu

F32 = jnp.float32
BF16 = jnp.bfloat16
I32 = jnp.int32

D_MODEL = 1024
LRU_WIDTH = 512
LRU_BLOCKS = 8
LRU_BLOCK_W = LRU_WIDTH // LRU_BLOCKS
CONV_W = 4
LRU_C = 8.0
ATT_WIDTH = 512
N_HEADS = 4
HEAD_DIM = 64
V_DIM = 128
IN_COLS = 2 * LRU_WIDTH + 3 * ATT_WIDTH
N_GROUPS = 4
EXPERTS_PER_GROUP = 4
N_EXPERTS = N_GROUPS * EXPERTS_PER_GROUP
D_EXPERT = D_MODEL // 2
CHUNK = 64
EPS = 1e-6
NEG_BIG = -1e30
LAMBDA_INIT = 0.8 - 0.6 * math.exp(-0.3 * 0)

LANES = 128
SUBLANES = 8

TM_PROJ = 512
TS_LRU = 256
TQ = 512
TM_TOK = 256
TM_EXP = 256
ROW_CHUNK = SUBLANES
VMEM_LIMIT = 56 * 1024 * 1024


def _cparams(n_axes):
    return pltpu.CompilerParams(
        dimension_semantics=("arbitrary",) * n_axes, vmem_limit_bytes=VMEM_LIMIT)


def _inproj_body(x_ref, g1_ref, w_ref, qkg_ref, o_ref):
    x = x_ref[...]
    ms = jnp.mean(x * x, axis=-1, keepdims=True)
    hn = ((x * lax.rsqrt(ms + EPS)) * g1_ref[...]).astype(BF16)
    tm = x.shape[0]
    lo_half = lax.broadcasted_iota(I32, (tm, LANES), 1) < HEAD_DIM
    width = 512
    for c in range(IN_COLS // width):
        c0 = c * width
        acc = jnp.dot(hn, w_ref[:, c0:c0 + width], preferred_element_type=F32)
        if c in (2, 3):
            gain = qkg_ref[c - 2:c - 1, :]
            for b in range(width // LANES):
                blk = acc[:, b * LANES:(b + 1) * LANES]
                sq = blk * blk
                s_lo = jnp.sum(jnp.where(lo_half, sq, 0.0), axis=-1, keepdims=True)
                s_hi = jnp.sum(jnp.where(lo_half, 0.0, sq), axis=-1, keepdims=True)
                inv = jnp.where(lo_half,
                                lax.rsqrt(s_lo * (1.0 / HEAD_DIM) + EPS),
                                lax.rsqrt(s_hi * (1.0 / HEAD_DIM) + EPS))
                o_ref[:, c0 + b * LANES:c0 + (b + 1) * LANES] = ((blk * inv) * gain).astype(BF16)
        else:
            o_ref[:, c0:c0 + width] = acc.astype(BF16)


def _in_proj(x2, g1, w_in_bf, qkg):
    t = x2.shape[0]
    tm = min(TM_PROJ, t)
    return pl.pallas_call(
        _inproj_body,
        grid=(t // tm,),
        in_specs=[
            pl.BlockSpec((tm, D_MODEL), lambda i: (i, 0)),
            pl.BlockSpec((1, D_MODEL), lambda i: (0, 0)),
            pl.BlockSpec((D_MODEL, IN_COLS), lambda i: (0, 0)),
            pl.BlockSpec((2, LANES), lambda i: (0, 0)),
        ],
        out_specs=pl.BlockSpec((tm, IN_COLS), lambda i: (i, 0)),
        out_shape=jax.ShapeDtypeStruct((t, IN_COLS), BF16),
        compiler_params=_cparams(1),
        name="in_proj",
    )(x2, g1, w_in_bf, qkg)


def _lru_body(p_ref, cw_ref, cb_ref, wg_ref, bg_ref, lam_ref, og_ref, o_ref, xbuf, hbuf, hc):
    s = pl.program_id(1)
    ts = p_ref.shape[0]

    @pl.when(s == 0)
    def _():
        xbuf[0:SUBLANES, :] = jnp.zeros((SUBLANES, LRU_WIDTH), F32)
        hc[...] = jnp.zeros((1, LRU_WIDTH), F32)

    x = p_ref[:, 0:LRU_WIDTH].astype(F32)
    gl = p_ref[:, LRU_WIDTH:2 * LRU_WIDTH].astype(F32)
    xbuf[SUBLANES:SUBLANES + ts, :] = x
    xc = cb_ref[...]
    for j in range(CONV_W):
        r0 = SUBLANES - (CONV_W - 1) + j
        xc = xc + xbuf[r0:r0 + ts, :] * cw_ref[j:j + 1, :]
    xbuf[0:SUBLANES, :] = x[ts - SUBLANES:ts, :]

    z = jnp.dot(xc.astype(BF16), wg_ref[...], preferred_element_type=F32) + bg_ref[...]
    r = jax.nn.sigmoid(z[:, 0:LRU_WIDTH])
    gi = jax.nn.sigmoid(z[:, LRU_WIDTH:2 * LRU_WIDTH])
    nl = -lam_ref[...]
    softplus = jnp.maximum(nl, 0.0) + jnp.log1p(jnp.exp(-jnp.abs(nl)))
    log_a = (-LRU_C) * r * softplus
    a = jnp.exp(log_a)
    u = jnp.sqrt(-jnp.tanh(log_a) * (a * a + 1.0)) * (gi * xc)

    row = lax.broadcasted_iota(I32, (ts, LRU_WIDTH), 0) & (SUBLANES - 1)
    ca, cb = a, u
    for d in (1, 2, 4):
        a_sh = pltpu.roll(ca, d, axis=0)
        b_sh = pltpu.roll(cb, d, axis=0)
        take = row >= d
        cb = jnp.where(take, ca * b_sh + cb, cb)
        ca = jnp.where(take, ca * a_sh, ca)
    h = hc[...]
    for blk in range(ts // SUBLANES):
        r0 = blk * SUBLANES
        hb = ca[r0:r0 + SUBLANES, :] * h + cb[r0:r0 + SUBLANES, :]
        hbuf[r0:r0 + SUBLANES, :] = hb
        h = hb[SUBLANES - 1:SUBLANES, :]
    hc[...] = h

    y = hbuf[...] * jax.nn.gelu(gl)
    ms = jnp.mean(y * y, axis=-1, keepdims=True)
    o_ref[...] = ((y * lax.rsqrt(ms + EPS)) * og_ref[...]).astype(o_ref.dtype)


def _lru(proj, b, s, conv_w, conv_b, wgate, bgate, lam, out_g):
    ts = min(TS_LRU, s)
    ns = s // ts
    vec = lambda n: pl.BlockSpec((1, n), lambda bi, si: (0, 0))
    return pl.pallas_call(
        _lru_body,
        grid=(b, ns),
        in_specs=[
            pl.BlockSpec((ts, 2 * LRU_WIDTH), lambda bi, si: (bi * ns + si, 0)),
            pl.BlockSpec((CONV_W, LRU_WIDTH), lambda bi, si: (0, 0)),
            vec(LRU_WIDTH),
            pl.BlockSpec((LRU_WIDTH, 2 * LRU_WIDTH), lambda bi, si: (0, 0)),
            vec(2 * LRU_WIDTH),
            vec(LRU_WIDTH),
            vec(LRU_WIDTH),
        ],
        out_specs=pl.BlockSpec((ts, LRU_WIDTH), lambda bi, si: (bi * ns + si, 0)),
        out_shape=jax.ShapeDtypeStruct((b * s, LRU_WIDTH), BF16),
        scratch_shapes=[
            pltpu.VMEM((ts + SUBLANES, LRU_WIDTH), F32),
            pltpu.VMEM((ts, LRU_WIDTH), F32),
            pltpu.VMEM((1, LRU_WIDTH), F32),
        ],
        compiler_params=_cparams(2),
        name="rg_lru",
    )(proj, conv_w, conv_b, wgate, bgate, lam, out_g)


def _attn_body(slope_ref, q_ref, k_ref, v_ref, bias_ref, lamp_ref, sg_ref, o_ref):
    h = pl.program_id(1)
    i = pl.program_id(2)
    tq = q_ref.shape[0]
    q = q_ref[...]
    lo_half = lax.broadcasted_iota(I32, (tq, LANES), 1) < HEAD_DIM
    zero = jnp.zeros_like(q)
    qs = jnp.concatenate([jnp.where(lo_half, q, zero), jnp.where(lo_half, zero, q)], axis=0)
    slope = slope_ref[h]

    def tile(j, carry, bias, shift):
        m, l, acc = carry
        k0 = pl.multiple_of(j * tq, tq)
        kt = k_ref[pl.ds(k0, tq), :]
        vt = v_ref[pl.ds(k0, tq), :]
        sb = lax.dot_general(qs, kt, (((1,), (1,)), ((), ())), preferred_element_type=F32) + bias
        m_new = jnp.maximum(m, jnp.max(sb, axis=-1, keepdims=True) + shift)
        p = jnp.exp(sb - (m_new - shift))
        alpha = jnp.exp(m - m_new)
        l_new = alpha * l + jnp.sum(p, axis=-1, keepdims=True)
        acc_new = alpha * acc + jnp.dot(p.astype(BF16), vt, preferred_element_type=F32)
        return m_new, l_new, acc_new

    def off_diag(j, carry):
        shift = -slope * ((i - j) * tq).astype(F32)
        return tile(j, carry, bias_ref[0, 0], shift)

    init = (jnp.full((2 * tq, 1), -jnp.inf, F32),
            jnp.zeros((2 * tq, 1), F32),
            jnp.zeros((2 * tq, V_DIM), F32))
    carry = lax.fori_loop(0, i, off_diag, init)
    m, l, acc = tile(i, carry, bias_ref[0, 1], jnp.float32(0.0))

    lp = lamp_ref[...]
    lam = (jnp.exp(jnp.sum(lp[0:1, :] * lp[1:2, :], axis=-1, keepdims=True))
           - jnp.exp(jnp.sum(lp[2:3, :] * lp[3:4, :], axis=-1, keepdims=True))
           + LAMBDA_INIT)
    o = acc[0:tq, :] / l[0:tq, :] - lam * (acc[tq:2 * tq, :] / l[tq:2 * tq, :])
    ms = jnp.mean(o * o, axis=-1, keepdims=True)
    o = ((o * lax.rsqrt(ms + EPS)) * sg_ref[...]) * (1.0 - LAMBDA_INIT)
    o_ref[...] = o.astype(o_ref.dtype)


def _alibi_tables(tq):
    slopes = np.exp2(-8.0 * np.arange(1, N_HEADS + 1, dtype=np.float64) / N_HEADS)
    qi = np.arange(tq)[:, None]
    kj = np.arange(tq)[None, :]
    off = -(slopes[:, None, None] * (qi - kj)[None])
    allowed = (kj // CHUNK) <= (qi // CHUNK)
    diag = np.where(allowed[None], -(slopes[:, None, None] * np.abs(qi - kj)[None]), NEG_BIG)
    tab = np.stack([off, diag], axis=1)
    tab = np.concatenate([tab, tab], axis=2)
    return jnp.asarray(tab, F32), jnp.asarray(slopes, F32)


def _attention(proj, b, s, lam_params, sub_g):
    tq = min(TQ, s)
    nq = s // tq
    bias, slopes = _alibi_tables(tq)
    qcol = 2 * LRU_WIDTH // LANES
    kcol = qcol + ATT_WIDTH // LANES
    vcol = kcol + ATT_WIDTH // LANES
    grid_spec = pltpu.PrefetchScalarGridSpec(
        num_scalar_prefetch=1,
        grid=(b, N_HEADS, nq),
        in_specs=[
            pl.BlockSpec((tq, LANES), lambda bi, h, i, sl: (bi * nq + i, qcol + h)),
            pl.BlockSpec((s, LANES), lambda bi, h, i, sl: (bi, kcol + h)),
            pl.BlockSpec((s, LANES), lambda bi, h, i, sl: (bi, vcol + h)),
            pl.BlockSpec((1, 2, 2 * tq, tq), lambda bi, h, i, sl: (h, 0, 0, 0)),
            pl.BlockSpec((4, HEAD_DIM), lambda bi, h, i, sl: (0, 0)),
            pl.BlockSpec((1, V_DIM), lambda bi, h, i, sl: (0, 0)),
        ],
        out_specs=pl.BlockSpec((tq, V_DIM), lambda bi, h, i, sl: (bi * nq + i, h)),
    )
    return pl.pallas_call(
        _attn_body,
        grid_spec=grid_spec,
        out_shape=jax.ShapeDtypeStruct((b * s, ATT_WIDTH), BF16),
        compiler_params=_cparams(3),
        name="diff_attn",
    )(slopes, proj, proj, proj, bias, lam_params, sub_g)


def _first_max4(v0, v1, v2, v3):
    m = jnp.maximum(jnp.maximum(v0, v1), jnp.maximum(v2, v3))
    idx = jnp.where(v0 == m, 0, jnp.where(v1 == m, 1, jnp.where(v2 == m, 2, 3))).astype(I32)
    return m, idx


def _route_body(yl_ref, ya_ref, x_ref, wo_ref, g2_ref, wrh_ref, wrl_ref, br_ref,
                h1_ref, xl_ref, aux_ref, gt_ref):
    tm = x_ref.shape[0]
    mix = (jnp.dot(yl_ref[...], wo_ref[0:LRU_WIDTH, :], preferred_element_type=F32)
           + jnp.dot(ya_ref[...], wo_ref[LRU_WIDTH:, :], preferred_element_type=F32))
    h1 = x_ref[...] + mix
    h1_ref[...] = h1
    ms = jnp.mean(h1 * h1, axis=-1, keepdims=True)
    hn = (h1 * lax.rsqrt(ms + EPS)) * g2_ref[...]
    hn_hi = hn.astype(BF16)
    hn_lo = (hn - hn_hi.astype(F32)).astype(BF16)
    logits = (jnp.dot(hn_hi, wrh_ref[...], preferred_element_type=F32)
              + jnp.dot(hn_hi, wrl_ref[...], preferred_element_type=F32)
              + jnp.dot(hn_lo, wrh_ref[...], preferred_element_type=F32)) + br_ref[...]
    lt = logits.T
    row = lambda n: lt[n:n + 1, :]
    gmax, gidx = _first_max4(row(0), row(1), row(2), row(3))
    zg = (jnp.exp(row(0) - gmax) + jnp.exp(row(1) - gmax)
          + jnp.exp(row(2) - gmax) + jnp.exp(row(3) - gmax))
    g_gate = 1.0 / zg
    base = N_GROUPS
    sel = [jnp.where(gidx == 0, row(base + j),
                     jnp.where(gidx == 1, row(base + 4 + j),
                               jnp.where(gidx == 2, row(base + 8 + j), row(base + 12 + j))))
           for j in range(EXPERTS_PER_GROUP)]
    m1, i1 = _first_max4(*sel)
    ze = sum(jnp.exp(sj - m1) for sj in sel)
    rest = [jnp.where(i1 == j, -jnp.inf, sel[j]) for j in range(EXPERTS_PER_GROUP)]
    m2, i2 = _first_max4(*rest)
    p1 = 1.0 / ze
    p2 = jnp.exp(m2 - m1) / ze
    gate1 = g_gate * (p1 / (p1 + p2))
    gate2 = g_gate * (p2 / (p1 + p2))
    e1 = gidx * EXPERTS_PER_GROUP + i1
    e2 = gidx * EXPERTS_PER_GROUP + i2

    eio = lax.broadcasted_iota(I32, (N_EXPERTS, tm), 0)
    oh1 = eio == e1
    oh2 = eio == e2
    both = (oh1 | oh2).astype(F32)
    cnt = jnp.sum(both, axis=1, keepdims=True)
    grp = jnp.floor((cnt + (ROW_CHUNK - 1)) * (1.0 / ROW_CHUNK)) * ROW_CHUNK
    ti = lax.broadcasted_iota(I32, (tm, tm), 0)
    tj = lax.broadcasted_iota(I32, (tm, tm), 1)
    before = (ti < tj).astype(BF16)
    rank = jnp.dot(both.astype(BF16), before, preferred_element_type=F32)
    start1 = jnp.sum(jnp.where(eio < e1, grp, 0.0), axis=0, keepdims=True)
    start2 = jnp.sum(jnp.where(eio < e2, grp, 0.0), axis=0, keepdims=True)
    slot1 = start1 + jnp.sum(jnp.where(oh1, rank, 0.0), axis=0, keepdims=True)
    slot2 = start2 + jnp.sum(jnp.where(oh2, rank, 0.0), axis=0, keepdims=True)

    loc = xl_ref.shape[0]
    sio = lax.broadcasted_iota(I32, (loc, tm), 0)
    perm = ((sio == slot1.astype(I32)) | (sio == slot2.astype(I32))).astype(BF16)
    xs = jnp.dot(perm, hn_hi, preferred_element_type=F32)
    xl_ref[...] = xs

    rio = lax.broadcasted_iota(I32, (LANES, tm), 0)
    aux_t = jnp.where(rio == 0, slot1, jnp.where(rio == 1, slot2,
                      jnp.where(rio == 2, gate1, jnp.where(rio == 3, gate2, 0.0))))
    aux_ref[...] = aux_t.T
    gt_ref[0] = jnp.broadcast_to(grp, (N_EXPERTS, LANES))


def _out_route(y_lru, y_att, x2, wo_bf, g2, wr_hi, wr_lo, br):
    t = x2.shape[0]
    tm = min(TM_TOK, t)
    nt = t // tm
    loc = 2 * tm + N_EXPERTS * ROW_CHUNK
    const = lambda shape: pl.BlockSpec(shape, lambda i: (0,) * len(shape))
    return pl.pallas_call(
        _route_body,
        grid=(nt,),
        in_specs=[
            pl.BlockSpec((tm, LRU_WIDTH), lambda i: (i, 0)),
            pl.BlockSpec((tm, ATT_WIDTH), lambda i: (i, 0)),
            pl.BlockSpec((tm, D_MODEL), lambda i: (i, 0)),
            const((D_MODEL, D_MODEL)),
            const((1, D_MODEL)),
            const((D_MODEL, LANES)),
            const((D_MODEL, LANES)),
            const((1, LANES)),
        ],
        out_specs=[
            pl.BlockSpec((tm, D_MODEL), lambda i: (i, 0)),
            pl.BlockSpec((loc, D_MODEL), lambda i: (i, 0)),
            pl.BlockSpec((tm, LANES), lambda i: (i, 0)),
            pl.BlockSpec((1, N_EXPERTS, LANES), lambda i: (i, 0, 0)),
        ],
        out_shape=[
            jax.ShapeDtypeStruct((t, D_MODEL), F32),
            jax.ShapeDtypeStruct((nt * loc, D_MODEL), F32),
            jax.ShapeDtypeStruct((t, LANES), F32),
            jax.ShapeDtypeStruct((nt, N_EXPERTS, LANES), F32),
        ],
        compiler_params=_cparams(1),
        name="out_route",
    )(y_lru, y_att, x2, wo_bf, g2, wr_hi, wr_lo, br)


CHUNKS_PER_TILE = TM_EXP // ROW_CHUNK


def _expert_body(tile_e_ref, nused_ref, nvalid_ref, chunk_ref, tail_row_ref, tail_n_ref,
                 xl_ref, wg_ref, wu_ref, wd_ref, yl_ref,
                 xbuf, ybuf, zbuf, gsem, ssem, zsem):
    j = pl.program_id(0)
    nused = nused_ref[0]
    nt = tail_n_ref.shape[0]
    slot = lax.rem(j, 2)

    def rows(c):
        return pl.ds(pl.multiple_of(c * ROW_CHUNK, ROW_CHUNK), ROW_CHUNK)

    def gather(t, sl, k):
        return pltpu.make_async_copy(
            xl_ref.at[rows(chunk_ref[t * CHUNKS_PER_TILE + k])], xbuf.at[sl, rows(k)], gsem.at[sl])

    def scatter(t, sl, k):
        return pltpu.make_async_copy(
            ybuf.at[sl, rows(k)], yl_ref.at[rows(chunk_ref[t * CHUNKS_PER_TILE + k])], ssem.at[sl])

    def for_chunks(n, fn):
        def body(k, c):
            fn(k)
            return c
        lax.fori_loop(0, n, body, 0)

    def zero_fill(i, k):
        return pltpu.make_async_copy(
            zbuf, yl_ref.at[pl.ds(pl.multiple_of(tail_row_ref[i] + k * ROW_CHUNK, ROW_CHUNK), ROW_CHUNK)],
            zsem)

    @pl.when(j == 0)
    def _():
        xbuf[...] = jnp.zeros(xbuf.shape, F32)
        zbuf[...] = jnp.zeros(zbuf.shape, F32)
        for_chunks(nt, lambda i: for_chunks(tail_n_ref[i], lambda k: zero_fill(i, k).start()))
        for_chunks(nvalid_ref[0], lambda k: gather(0, 0, k).start())

    @pl.when(j + 1 < nused)
    def _():
        for_chunks(nvalid_ref[j + 1], lambda k: gather(j + 1, 1 - slot, k).start())

    @pl.when(j < nused)
    def _():
        for_chunks(nvalid_ref[j], lambda k: gather(j, slot, 0).wait())

        @pl.when(j >= 2)
        def _():
            for_chunks(nvalid_ref[j - 2], lambda k: scatter(j - 2, slot, 0).wait())

        xb = xbuf[slot].astype(BF16)
        gate = jnp.dot(xb, wg_ref[0], preferred_element_type=F32)
        up = jnp.dot(xb, wu_ref[0], preferred_element_type=F32)
        hid = (jax.nn.silu(gate) * up).astype(BF16)
        ybuf[slot] = jnp.dot(hid, wd_ref[0], preferred_element_type=F32)
        for_chunks(nvalid_ref[j], lambda k: scatter(j, slot, k).start())

    @pl.when(j == nused - 1)
    def _():
        @pl.when(j >= 1)
        def _():
            for_chunks(nvalid_ref[j - 1], lambda k: scatter(j - 1, 1 - slot, 0).wait())
        for_chunks(nvalid_ref[j], lambda k: scatter(j, slot, 0).wait())
        for_chunks(nt, lambda i: for_chunks(tail_n_ref[i], lambda k: zero_fill(i, 0).wait()))


def _experts(xl, tables, wg_bf, wu_bf, wd_bf, n_tiles):
    tile_e, nused, nvalid, chunk_map, tail_row, tail_n = tables
    wmap = lambda j, te, *_: (te[j], 0, 0)
    grid_spec = pltpu.PrefetchScalarGridSpec(
        num_scalar_prefetch=6,
        grid=(n_tiles,),
        in_specs=[
            pl.BlockSpec(memory_space=pl.ANY),
            pl.BlockSpec((1, D_MODEL, D_EXPERT), wmap),
            pl.BlockSpec((1, D_MODEL, D_EXPERT), wmap),
            pl.BlockSpec((1, D_EXPERT, D_MODEL), wmap),
        ],
        out_specs=pl.BlockSpec(memory_space=pl.ANY),
        scratch_shapes=[
            pltpu.VMEM((2, TM_EXP, D_MODEL), F32),
            pltpu.VMEM((2, TM_EXP, D_MODEL), F32),
            pltpu.VMEM((ROW_CHUNK, D_MODEL), F32),
            pltpu.SemaphoreType.DMA((2,)),
            pltpu.SemaphoreType.DMA((2,)),
            pltpu.SemaphoreType.DMA,
        ],
    )
    return pl.pallas_call(
        _expert_body,
        grid_spec=grid_spec,
        out_shape=jax.ShapeDtypeStruct(xl.shape, F32),
        compiler_params=_cparams(1),
        name="experts",
    )(tile_e, nused, nvalid, chunk_map, tail_row, tail_n, xl, wg_bf, wu_bf, wd_bf)


def _combine_body(h1_ref, aux_ref, yl_ref, o_ref):
    tm = h1_ref.shape[0]
    loc = yl_ref.shape[0]
    aux = aux_ref[...]
    slot1 = aux[:, 0:1].astype(I32)
    slot2 = aux[:, 1:2].astype(I32)
    sio = lax.broadcasted_iota(I32, (tm, loc), 1)
    gm = jnp.where(sio == slot1, aux[:, 2:3], 0.0) + jnp.where(sio == slot2, aux[:, 3:4], 0.0)
    gm_hi = gm.astype(BF16)
    gm_lo = (gm - gm_hi.astype(F32)).astype(BF16)
    yb = yl_ref[...].astype(BF16)
    o_ref[...] = (h1_ref[...] + jnp.dot(gm_hi, yb, preferred_element_type=F32)
                  + jnp.dot(gm_lo, yb, preferred_element_type=F32))


def _combine(h1, aux, yl, nt, tm, loc):
    return pl.pallas_call(
        _combine_body,
        grid=(nt,),
        in_specs=[
            pl.BlockSpec((tm, D_MODEL), lambda i: (i, 0)),
            pl.BlockSpec((tm, LANES), lambda i: (i, 0)),
            pl.BlockSpec((loc, D_MODEL), lambda i: (i, 0)),
        ],
        out_specs=pl.BlockSpec((tm, D_MODEL), lambda i: (i, 0)),
        out_shape=jax.ShapeDtypeStruct(h1.shape, F32),
        compiler_params=_cparams(1),
        name="combine",
    )(h1, aux, yl)


def _excl_cumsum(a, axis):
    return jnp.cumsum(a, axis=axis) - a


def _expert_tables(grp, n_tiles, loc):
    nt = grp.shape[0]
    gch = grp // ROW_CHUNK
    loc_start = _excl_cumsum(gch, 1)
    used = jnp.sum(gch, axis=1)
    col = jnp.sum(gch, axis=0)
    seg = ((col + CHUNKS_PER_TILE - 1) // CHUNKS_PER_TILE) * CHUNKS_PER_TILE
    seg_end = jnp.cumsum(seg)
    off = seg_end - seg
    tile_first = jnp.arange(n_tiles, dtype=I32) * CHUNKS_PER_TILE
    tile_e = jnp.minimum(jnp.sum(seg_end[None, :] <= tile_first[:, None], axis=1), N_EXPERTS - 1)
    nused = seg_end[-1:] // CHUNKS_PER_TILE
    nvalid = jnp.clip(col[tile_e] - (tile_first - off[tile_e]), 0, CHUNKS_PER_TILE)

    c = jnp.arange(n_tiles * CHUNKS_PER_TILE, dtype=I32)
    e_c = jnp.repeat(tile_e, CHUNKS_PER_TILE)
    cc = c - off[e_c]
    cum = jnp.cumsum(gch, axis=0)
    cum_c = cum.T[e_c]
    i_c = jnp.minimum(jnp.sum(cum_c <= cc[:, None], axis=1), nt - 1)
    k_c = cc - (cum[i_c, e_c] - gch[i_c, e_c])
    chunk_map = i_c * (loc // ROW_CHUNK) + loc_start[i_c, e_c] + k_c
    valid = cc < col[e_c]
    chunk_map = jnp.where(valid, chunk_map, 0)

    tail_row = jnp.arange(nt, dtype=I32) * loc + used * ROW_CHUNK
    tail_n = loc // ROW_CHUNK - used
    as_i32 = lambda a: a.astype(I32)
    return tuple(map(as_i32, (tile_e, nused, nvalid, chunk_map, tail_row, tail_n)))


def kernel(x, norm1_g, w_in, conv_w, conv_b, w_gate_a, b_gate_a, w_gate_x, b_gate_x, lru_lambda,
           lru_out_g, q_norm_g, k_norm_g, lambda_q1, lambda_k1, lambda_q2, lambda_k2, sub_norm_g,
           w_out, norm2_g, w_router_group, b_router_group, w_router_expert, b_router_expert,
           w_expert_gate, w_expert_up, w_expert_down):
    b, s, d = x.shape
    assert d == D_MODEL and norm1_g.shape[0] == 1
    t = b * s
    l = 0
    x2 = x.reshape(t, d)

    w_in_bf = w_in[l].astype(BF16)
    scale = HEAD_DIM ** -0.5
    qkg = jnp.stack([jnp.tile(q_norm_g[l], 2) * scale, jnp.tile(k_norm_g[l], 2)]).astype(F32)
    eye = jnp.eye(LRU_BLOCKS, dtype=F32)
    blockdiag = lambda w: jnp.einsum("ncd,nm->ncmd", w, eye).reshape(LRU_WIDTH, LRU_WIDTH)
    wgate = jnp.concatenate([blockdiag(w_gate_a[l]), blockdiag(w_gate_x[l])], axis=1).astype(BF16)
    bgate = jnp.concatenate([b_gate_a[l], b_gate_x[l]])[None, :]
    lam_params = jnp.stack([lambda_q1[l], lambda_k1[l], lambda_q2[l], lambda_k2[l]])
    wr = jnp.concatenate(
        [w_router_group[l], jnp.transpose(w_router_expert[l], (1, 0, 2)).reshape(d, N_EXPERTS)], axis=1)
    wr = jnp.pad(wr, ((0, 0), (0, LANES - wr.shape[1])))
    wr_hi = wr.astype(BF16)
    wr_lo = (wr - wr_hi.astype(F32)).astype(BF16)
    br = jnp.pad(jnp.concatenate([b_router_group[l], b_router_expert[l].reshape(-1)]),
                 (0, LANES - N_GROUPS - N_EXPERTS))[None, :]

    proj = _in_proj(x2, norm1_g[l][None, :], w_in_bf, qkg)
    y_lru = _lru(proj, b, s, conv_w[l], conv_b[l][None, :], wgate, bgate,
                 lru_lambda[l][None, :], lru_out_g[l][None, :])
    y_att = _attention(proj, b, s, lam_params, sub_norm_g[l][None, :])
    h1, xl, aux, gt = _out_route(y_lru, y_att, x2, w_out[l].astype(BF16), norm2_g[l][None, :],
                                 wr_hi, wr_lo, br)

    tm = min(TM_TOK, t)
    nt = t // tm
    loc = 2 * tm + N_EXPERTS * ROW_CHUNK
    max_rows = 2 * t + nt * N_EXPERTS * (ROW_CHUNK - 1) + N_EXPERTS * (TM_EXP - ROW_CHUNK)
    n_tiles = -(-max_rows // TM_EXP)
    grp = gt[:, :, 0].astype(I32)
    tables = _expert_tables(grp, n_tiles, loc)
    yl = _experts(xl, tables, w_expert_gate[l].astype(BF16), w_expert_up[l].astype(BF16),
                  w_expert_down[l].astype(BF16), n_tiles)
    out = _combine(h1, aux, yl, nt, tm, loc)
    return out.reshape(b, s, d)
```

```python
import functools
import math

import numpy as np
import jax
import jax.numpy as jnp
from jax import lax
from jax.experimental import pallas as pl
from jax.experimental.pallas import tpu as pltpu

F32 = jnp.float32
BF16 = jnp.bfloat16
I32 = jnp.int32

D_MODEL = 1024
LRU_WIDTH = 512
LRU_BLOCKS = 8
LRU_BLOCK_W = LRU_WIDTH // LRU_BLOCKS
CONV_W = 4
LRU_C = 8.0
ATT_WIDTH = 512
N_HEADS = 4
HEAD_DIM = 64
V_DIM = 128
IN_COLS = 2 * LRU_WIDTH + 3 * ATT_WIDTH
N_GROUPS = 4
EXPERTS_PER_GROUP = 4
N_EXPERTS = N_GROUPS * EXPERTS_PER_GROUP
D_EXPERT = D_MODEL // 2
CHUNK = 64
EPS = 1e-6
NEG_BIG = -1e30
LAMBDA_INIT = 0.8 - 0.6 * math.exp(-0.3 * 0)

LANES = 128
SUBLANES = 8

TM_PROJ = 512
TS_LRU = 256
TQ = 512
TM_TOK = 256
TM_EXP = 256
ROW_CHUNK = SUBLANES
VMEM_LIMIT = 56 * 1024 * 1024


def _cparams(n_axes):
    return pltpu.CompilerParams(
        dimension_semantics=("arbitrary",) * n_axes, vmem_limit_bytes=VMEM_LIMIT)


def _inproj_body(x_ref, g1_ref, w_ref, qkg_ref, o_ref):
    x = x_ref[...]
    ms = jnp.mean(x * x, axis=-1, keepdims=True)
    hn = ((x * lax.rsqrt(ms + EPS)) * g1_ref[...]).astype(BF16)
    tm = x.shape[0]
    lo_half = lax.broadcasted_iota(I32, (tm, LANES), 1) < HEAD_DIM
    width = 512
    for c in range(IN_COLS // width):
        c0 = c * width
        acc = jnp.dot(hn, w_ref[:, c0:c0 + width], preferred_element_type=F32)
        if c in (2, 3):
            gain = qkg_ref[c - 2:c - 1, :]
            for b in range(width // LANES):
                blk = acc[:, b * LANES:(b + 1) * LANES]
                sq = blk * blk
                s_lo = jnp.sum(jnp.where(lo_half, sq, 0.0), axis=-1, keepdims=True)
                s_hi = jnp.sum(jnp.where(lo_half, 0.0, sq), axis=-1, keepdims=True)
                inv = jnp.where(lo_half,
                                lax.rsqrt(s_lo * (1.0 / HEAD_DIM) + EPS),
                                lax.rsqrt(s_hi * (1.0 / HEAD_DIM) + EPS))
                o_ref[:, c0 + b * LANES:c0 + (b + 1) * LANES] = ((blk * inv) * gain).astype(BF16)
        else:
            o_ref[:, c0:c0 + width] = acc.astype(BF16)


def _in_proj(x2, g1, w_in_bf, qkg):
    t = x2.shape[0]
    tm = min(TM_PROJ, t)
    return pl.pallas_call(
        _inproj_body,
        grid=(t // tm,),
        in_specs=[
            pl.BlockSpec((tm, D_MODEL), lambda i: (i, 0)),
            pl.BlockSpec((1, D_MODEL), lambda i: (0, 0)),
            pl.BlockSpec((D_MODEL, IN_COLS), lambda i: (0, 0)),
            pl.BlockSpec((2, LANES), lambda i: (0, 0)),
        ],
        out_specs=pl.BlockSpec((tm, IN_COLS), lambda i: (i, 0)),
        out_shape=jax.ShapeDtypeStruct((t, IN_COLS), BF16),
        compiler_params=_cparams(1),
        name="in_proj",
    )(x2, g1, w_in_bf, qkg)


def _lru_body(p_ref, cw_ref, cb_ref, wg_ref, bg_ref, lam_ref, og_ref, o_ref, xbuf, hbuf, hc):
    s = pl.program_id(1)
    ts = p_ref.shape[0]

    @pl.when(s == 0)
    def _():
        xbuf[0:SUBLANES, :] = jnp.zeros((SUBLANES, LRU_WIDTH), F32)
        hc[...] = jnp.zeros((1, LRU_WIDTH), F32)

    x = p_ref[:, 0:LRU_WIDTH].astype(F32)
    gl = p_ref[:, LRU_WIDTH:2 * LRU_WIDTH].astype(F32)
    xbuf[SUBLANES:SUBLANES + ts, :] = x
    xc = cb_ref[...]
    for j in range(CONV_W):
        r0 = SUBLANES - (CONV_W - 1) + j
        xc = xc + xbuf[r0:r0 + ts, :] * cw_ref[j:j + 1, :]
    xbuf[0:SUBLANES, :] = x[ts - SUBLANES:ts, :]

    z = jnp.dot(xc.astype(BF16), wg_ref[...], preferred_element_type=F32) + bg_ref[...]
    r = 0.5 * jnp.tanh(0.5 * z[:, 0:LRU_WIDTH]) + 0.5
    gi = 0.5 * jnp.tanh(0.5 * z[:, LRU_WIDTH:2 * LRU_WIDTH]) + 0.5
    nl = -lam_ref[...]
    softplus = jnp.maximum(nl, 0.0) + jnp.log1p(jnp.exp(-jnp.abs(nl)))
    log_a = (-LRU_C) * r * softplus
    a = jnp.exp(log_a)
    v = -jnp.tanh(log_a) * (a * a + 1.0)
    u = jnp.where(v > 0.0, v * lax.rsqrt(v), 0.0) * (gi * xc)

    row = lax.broadcasted_iota(I32, (ts, LRU_WIDTH), 0) & (SUBLANES - 1)
    ca, cb = a, u
    for d in (1, 2, 4):
        a_sh = pltpu.roll(ca, d, axis=0)
        b_sh = pltpu.roll(cb, d, axis=0)
        take = row >= d
        cb = jnp.where(take, ca * b_sh + cb, cb)
        ca = jnp.where(take, ca * a_sh, ca)
    h = hc[...]
    for blk in range(ts // SUBLANES):
        r0 = blk * SUBLANES
        hb = ca[r0:r0 + SUBLANES, :] * h + cb[r0:r0 + SUBLANES, :]
        hbuf[r0:r0 + SUBLANES, :] = hb
        h = hb[SUBLANES - 1:SUBLANES, :]
    hc[...] = h

    y = hbuf[...] * jax.nn.gelu(gl)
    ms = jnp.mean(y * y, axis=-1, keepdims=True)
    o_ref[...] = ((y * lax.rsqrt(ms + EPS)) * og_ref[...]).astype(o_ref.dtype)


def _lru(proj, b, s, conv_w, conv_b, wgate, bgate, lam, out_g):
    ts = min(TS_LRU, s)
    ns = s // ts
    vec = lambda n: pl.BlockSpec((1, n), lambda bi, si: (0, 0))
    return pl.pallas_call(
        _lru_body,
        grid=(b, ns),
        in_specs=[
            pl.BlockSpec((ts, 2 * LRU_WIDTH), lambda bi, si: (bi * ns + si, 0)),
            pl.BlockSpec((CONV_W, LRU_WIDTH), lambda bi, si: (0, 0)),
            vec(LRU_WIDTH),
            pl.BlockSpec((LRU_WIDTH, 2 * LRU_WIDTH), lambda bi, si: (0, 0)),
            vec(2 * LRU_WIDTH),
            vec(LRU_WIDTH),
            vec(LRU_WIDTH),
        ],
        out_specs=pl.BlockSpec((ts, LRU_WIDTH), lambda bi, si: (bi * ns + si, 0)),
        out_shape=jax.ShapeDtypeStruct((b * s, LRU_WIDTH), BF16),
        scratch_shapes=[
            pltpu.VMEM((ts + SUBLANES, LRU_WIDTH), F32),
            pltpu.VMEM((ts, LRU_WIDTH), F32),
            pltpu.VMEM((1, LRU_WIDTH), F32),
        ],
        compiler_params=_cparams(2),
        name="rg_lru",
    )(proj, conv_w, conv_b, wgate, bgate, lam, out_g)


SOFTMAX_ROWS = 32


def _attn_body(slope_ref, q_ref, k_ref, v_ref, bias_ref, lamp_ref, sg_ref, o_ref,
               qs_buf, s0, s1, p0, p1, a0, a1, m_buf, acc0, acc1, *, tq, nq):
    h = pl.program_id(1)
    slope = slope_ref[h]
    s_bufs, p_bufs, a_bufs, accs = (s0, s1), (p0, p1), (a0, a1), (acc0, acc1)
    lo_half = lax.broadcasted_iota(I32, (tq, LANES), 1) < HEAD_DIM
    for i in range(nq):
        q = q_ref[i * tq:(i + 1) * tq, :]
        zero = jnp.zeros_like(q)
        qs_buf[i, 0:tq, :] = jnp.where(lo_half, q, zero)
        qs_buf[i, tq:2 * tq, :] = jnp.where(lo_half, zero, q)
    ones = jnp.ones((tq, V_DIM), BF16)
    lp = lamp_ref[...]
    lam = (jnp.exp(jnp.sum(lp[0:1, :] * lp[1:2, :], axis=-1, keepdims=True))
           - jnp.exp(jnp.sum(lp[2:3, :] * lp[3:4, :], axis=-1, keepdims=True))
           + LAMBDA_INIT)
    pairs = [(i, j) for i in range(nq) for j in range(i + 1)]

    def scores(t):
        i, j = pairs[t]
        s_bufs[t % 2][...] = lax.dot_general(
            qs_buf[i], k_ref[j * tq:(j + 1) * tq, :], (((1,), (1,)), ((), ())),
            preferred_element_type=F32)

    def softmax(t):
        i, j = pairs[t]
        which = 1 if j == i else 0
        shift = slope * float(-(i - j) * tq)
        s_buf, p_buf, a_buf = s_bufs[t % 2], p_bufs[t % 2], a_bufs[t % 2]
        for r in range(2 * tq // SOFTMAX_ROWS):
            rows = slice(r * SOFTMAX_ROWS, (r + 1) * SOFTMAX_ROWS)
            sb = s_buf[rows, :] + bias_ref[0, which, rows, :]
            m_new = jnp.max(sb, axis=-1, keepdims=True) + shift
            if j > 0:
                m_old = m_buf[rows, :]
                m_new = jnp.maximum(m_old, m_new)
                a_buf[rows, :] = jnp.exp(m_old - m_new)
            p_buf[rows, :] = jnp.exp(sb - (m_new - shift)).astype(BF16)
            m_buf[rows, :] = m_new

    def accumulate(t):
        i, j = pairs[t]
        acc = accs[i % 2]
        v_aug = jnp.concatenate([v_ref[j * tq:(j + 1) * tq, :], ones], axis=1)
        pv = jnp.dot(p_bufs[t % 2][...], v_aug, preferred_element_type=F32)
        if j == 0:
            acc[...] = pv
        else:
            acc[...] = a_bufs[t % 2][...] * acc[...] + pv
        if j == i:
            o = (acc[0:tq, 0:V_DIM] / acc[0:tq, V_DIM:V_DIM + 1]
                 - lam * (acc[tq:2 * tq, 0:V_DIM] / acc[tq:2 * tq, V_DIM:V_DIM + 1]))
            ms = jnp.mean(o * o, axis=-1, keepdims=True)
            o = ((o * lax.rsqrt(ms + EPS)) * sg_ref[...]) * (1.0 - LAMBDA_INIT)
            o_ref[i * tq:(i + 1) * tq, :] = o.astype(o_ref.dtype)

    scores(0)
    for t in range(len(pairs)):
        if t + 1 < len(pairs):
            scores(t + 1)
        softmax(t)
        if t >= 1:
            accumulate(t - 1)
    accumulate(len(pairs) - 1)


def _alibi_tables(tq):
    slopes = np.exp2(-8.0 * np.arange(1, N_HEADS + 1, dtype=np.float64) / N_HEADS)
    qi = np.arange(tq)[:, None]
    kj = np.arange(tq)[None, :]
    off = -(slopes[:, None, None] * (qi - kj)[None])
    allowed = (kj // CHUNK) <= (qi // CHUNK)
    diag = np.where(allowed[None], -(slopes[:, None, None] * np.abs(qi - kj)[None]), NEG_BIG)
    tab = np.stack([off, diag], axis=1)
    tab = np.concatenate([tab, tab], axis=2)
    return jnp.asarray(tab, F32), jnp.asarray(slopes, F32)


def _attention(proj, b, s, lam_params, sub_g):
    tq = min(TQ, s)
    nq = s // tq
    bias, slopes = _alibi_tables(tq)
    qcol = 2 * LRU_WIDTH // LANES
    kcol = qcol + ATT_WIDTH // LANES
    vcol = kcol + ATT_WIDTH // LANES
    score_buf = pltpu.VMEM((2 * tq, tq), F32)
    prob_buf = pltpu.VMEM((2 * tq, tq), BF16)
    col_buf = pltpu.VMEM((2 * tq, 1), F32)
    acc_buf = pltpu.VMEM((2 * tq, 2 * V_DIM), F32)
    grid_spec = pltpu.PrefetchScalarGridSpec(
        num_scalar_prefetch=1,
        grid=(b, N_HEADS),
        in_specs=[
            pl.BlockSpec((s, LANES), lambda bi, h, sl: (bi, qcol + h)),
            pl.BlockSpec((s, LANES), lambda bi, h, sl: (bi, kcol + h)),
            pl.BlockSpec((s, LANES), lambda bi, h, sl: (bi, vcol + h)),
            pl.BlockSpec((1, 2, 2 * tq, tq), lambda bi, h, sl: (h, 0, 0, 0)),
            pl.BlockSpec((4, HEAD_DIM), lambda bi, h, sl: (0, 0)),
            pl.BlockSpec((1, V_DIM), lambda bi, h, sl: (0, 0)),
        ],
        out_specs=pl.BlockSpec((s, V_DIM), lambda bi, h, sl: (bi, h)),
        scratch_shapes=[
            pltpu.VMEM((nq, 2 * tq, LANES), BF16),
            score_buf, score_buf, prob_buf, prob_buf, col_buf, col_buf, col_buf, acc_buf, acc_buf,
        ],
    )
    return pl.pallas_call(
        functools.partial(_attn_body, tq=tq, nq=nq),
        grid_spec=grid_spec,
        out_shape=jax.ShapeDtypeStruct((b * s, ATT_WIDTH), BF16),
        compiler_params=_cparams(2),
        name="diff_attn",
    )(slopes, proj, proj, proj, bias, lam_params, sub_g)


def _first_max4(v0, v1, v2, v3):
    m = jnp.maximum(jnp.maximum(v0, v1), jnp.maximum(v2, v3))
    idx = jnp.where(v0 == m, 0, jnp.where(v1 == m, 1, jnp.where(v2 == m, 2, 3))).astype(I32)
    return m, idx


def _route_body(yl_ref, ya_ref, x_ref, wo_ref, g2_ref, wrh_ref, wrl_ref, br_ref,
                h1_ref, xl_ref, aux_ref, gt_ref):
    tm = x_ref.shape[0]
    mix = (jnp.dot(yl_ref[...], wo_ref[0:LRU_WIDTH, :], preferred_element_type=F32)
           + jnp.dot(ya_ref[...], wo_ref[LRU_WIDTH:, :], preferred_element_type=F32))
    h1 = x_ref[...] + mix
    h1_ref[...] = h1
    ms = jnp.mean(h1 * h1, axis=-1, keepdims=True)
    hn = (h1 * lax.rsqrt(ms + EPS)) * g2_ref[...]
    hn_hi = hn.astype(BF16)
    hn_lo = (hn - hn_hi.astype(F32)).astype(BF16)
    logits = (jnp.dot(hn_hi, wrh_ref[...], preferred_element_type=F32)
              + jnp.dot(hn_hi, wrl_ref[...], preferred_element_type=F32)
              + jnp.dot(hn_lo, wrh_ref[...], preferred_element_type=F32)) + br_ref[...]
    lt = logits.T
    row = lambda n: lt[n:n + 1, :]
    gmax, gidx = _first_max4(row(0), row(1), row(2), row(3))
    zg = (jnp.exp(row(0) - gmax) + jnp.exp(row(1) - gmax)
          + jnp.exp(row(2) - gmax) + jnp.exp(row(3) - gmax))
    g_gate = 1.0 / zg
    base = N_GROUPS
    sel = [jnp.where(gidx == 0, row(base + j),
                     jnp.where(gidx == 1, row(base + 4 + j),
                               jnp.where(gidx == 2, row(base + 8 + j), row(base + 12 + j))))
           for j in range(EXPERTS_PER_GROUP)]
    m1, i1 = _first_max4(*sel)
    ze = sum(jnp.exp(sj - m1) for sj in sel)
    rest = [jnp.where(i1 == j, -jnp.inf, sel[j]) for j in range(EXPERTS_PER_GROUP)]
    m2, i2 = _first_max4(*rest)
    p1 = 1.0 / ze
    p2 = jnp.exp(m2 - m1) / ze
    gate1 = g_gate * (p1 / (p1 + p2))
    gate2 = g_gate * (p2 / (p1 + p2))
    e1 = gidx * EXPERTS_PER_GROUP + i1
    e2 = gidx * EXPERTS_PER_GROUP + i2

    eio = lax.broadcasted_iota(I32, (N_EXPERTS, tm), 0)
    oh1 = eio == e1
    oh2 = eio == e2
    both = (oh1 | oh2).astype(F32)
    cnt = jnp.sum(both, axis=1, keepdims=True)
    grp = jnp.floor((cnt + (ROW_CHUNK - 1)) * (1.0 / ROW_CHUNK)) * ROW_CHUNK
    ti = lax.broadcasted_iota(I32, (tm, tm), 0)
    tj = lax.broadcasted_iota(I32, (tm, tm), 1)
    before = (ti < tj).astype(BF16)
    rank = jnp.dot(both.astype(BF16), before, preferred_element_type=F32)
    start1 = jnp.sum(jnp.where(eio < e1, grp, 0.0), axis=0, keepdims=True)
    start2 = jnp.sum(jnp.where(eio < e2, grp, 0.0), axis=0, keepdims=True)
    slot1 = start1 + jnp.sum(jnp.where(oh1, rank, 0.0), axis=0, keepdims=True)
    slot2 = start2 + jnp.sum(jnp.where(oh2, rank, 0.0), axis=0, keepdims=True)

    loc = xl_ref.shape[0]
    sio = lax.broadcasted_iota(I32, (loc, tm), 0)
    perm = ((sio == slot1.astype(I32)) | (sio == slot2.astype(I32))).astype(BF16)
    xs = jnp.dot(perm, hn_hi, preferred_element_type=F32)
    xl_ref[...] = xs

    rio = lax.broadcasted_iota(I32, (LANES, tm), 0)
    aux_t = jnp.where(rio == 0, slot1, jnp.where(rio == 1, slot2,
                      jnp.where(rio == 2, gate1, jnp.where(rio == 3, gate2, 0.0))))
    aux_ref[...] = aux_t.T
    gt_ref[0] = jnp.broadcast_to(grp, (N_EXPERTS, LANES))


def _out_route(y_lru, y_att, x2, wo_bf, g2, wr_hi, wr_lo, br):
    t = x2.shape[0]
    tm = min(TM_TOK, t)
    nt = t // tm
    loc = 2 * tm + N_EXPERTS * ROW_CHUNK
    const = lambda shape: pl.BlockSpec(shape, lambda i: (0,) * len(shape))
    return pl.pallas_call(
        _route_body,
        grid=(nt,),
        in_specs=[
            pl.BlockSpec((tm, LRU_WIDTH), lambda i: (i, 0)),
            pl.BlockSpec((tm, ATT_WIDTH), lambda i: (i, 0)),
            pl.BlockSpec((tm, D_MODEL), lambda i: (i, 0)),
            const((D_MODEL, D_MODEL)),
            const((1, D_MODEL)),
            const((D_MODEL, LANES)),
            const((D_MODEL, LANES)),
            const((1, LANES)),
        ],
        out_specs=[
            pl.BlockSpec((tm, D_MODEL), lambda i: (i, 0)),
            pl.BlockSpec((loc, D_MODEL), lambda i: (i, 0)),
            pl.BlockSpec((tm, LANES), lambda i: (i, 0)),
            pl.BlockSpec((1, N_EXPERTS, LANES), lambda i: (i, 0, 0)),
        ],
        out_shape=[
            jax.ShapeDtypeStruct((t, D_MODEL), F32),
            jax.ShapeDtypeStruct((nt * loc, D_MODEL), F32),
            jax.ShapeDtypeStruct((t, LANES), F32),
            jax.ShapeDtypeStruct((nt, N_EXPERTS, LANES), F32),
        ],
        compiler_params=_cparams(1),
        name="out_route",
    )(y_lru, y_att, x2, wo_bf, g2, wr_hi, wr_lo, br)


CHUNKS_PER_TILE = TM_EXP // ROW_CHUNK


def _expert_body(tile_e_ref, nused_ref, nvalid_ref, chunk_ref, tail_row_ref, tail_n_ref,
                 xl_ref, wg_ref, wu_ref, wd_ref, yl_ref,
                 xbuf, ybuf, zbuf, gsem, ssem, zsem):
    j = pl.program_id(0)
    nused = nused_ref[0]
    nt = tail_n_ref.shape[0]
    slot = lax.rem(j, 2)

    def rows(c):
        if isinstance(c, int):
            return pl.ds(c * ROW_CHUNK, ROW_CHUNK)
        return pl.ds(pl.multiple_of(c * ROW_CHUNK, ROW_CHUNK), ROW_CHUNK)

    def gather(t, sl, k):
        return pltpu.make_async_copy(
            xl_ref.at[rows(chunk_ref[t * CHUNKS_PER_TILE + k])], xbuf.at[sl, rows(k)], gsem.at[sl])

    def scatter(t, sl, k):
        return pltpu.make_async_copy(
            ybuf.at[sl, rows(k)], yl_ref.at[rows(chunk_ref[t * CHUNKS_PER_TILE + k])], ssem.at[sl])

    def for_chunks(n, fn):
        def body(k, c):
            fn(k)
            return c
        lax.fori_loop(0, n, body, 0)

    def start_tile(n, fn):
        @pl.when(n == CHUNKS_PER_TILE)
        def _():
            for k in range(CHUNKS_PER_TILE):
                fn(k).start()

        @pl.when(n < CHUNKS_PER_TILE)
        def _():
            for_chunks(n, lambda k: fn(k).start())

    def wait_tile(n, chunk_copy, tile_copy):
        @pl.when(n == CHUNKS_PER_TILE)
        def _():
            tile_copy.wait()

        @pl.when(n < CHUNKS_PER_TILE)
        def _():
            for_chunks(n, lambda k: chunk_copy.wait())

    def wait_gather(t, sl):
        wait_tile(nvalid_ref[t], gather(t, sl, 0),
                  pltpu.make_async_copy(xl_ref.at[pl.ds(0, TM_EXP)], xbuf.at[sl], gsem.at[sl]))

    def wait_scatter(t, sl):
        wait_tile(nvalid_ref[t], scatter(t, sl, 0),
                  pltpu.make_async_copy(ybuf.at[sl], yl_ref.at[pl.ds(0, TM_EXP)], ssem.at[sl]))

    def zero_fill(i, k):
        return pltpu.make_async_copy(
            zbuf, yl_ref.at[pl.ds(pl.multiple_of(tail_row_ref[i] + k * ROW_CHUNK, ROW_CHUNK), ROW_CHUNK)],
            zsem)

    @pl.when(j == 0)
    def _():
        xbuf[...] = jnp.zeros(xbuf.shape, F32)
        zbuf[...] = jnp.zeros(zbuf.shape, F32)
        for_chunks(nt, lambda i: for_chunks(tail_n_ref[i], lambda k: zero_fill(i, k).start()))
        start_tile(nvalid_ref[0], lambda k: gather(0, 0, k))

    @pl.when(j + 1 < nused)
    def _():
        start_tile(nvalid_ref[j + 1], lambda k: gather(j + 1, 1 - slot, k))

    @pl.when(j < nused)
    def _():
        wait_gather(j, slot)

        @pl.when(j >= 2)
        def _():
            wait_scatter(j - 2, slot)

        xb = xbuf[slot].astype(BF16)
        gate = jnp.dot(xb, wg_ref[0], preferred_element_type=F32)
        up = jnp.dot(xb, wu_ref[0], preferred_element_type=F32)
        hid = (jax.nn.silu(gate) * up).astype(BF16)
        ybuf[slot] = jnp.dot(hid, wd_ref[0], preferred_element_type=F32)
        start_tile(nvalid_ref[j], lambda k: scatter(j, slot, k))

    @pl.when(j == nused - 1)
    def _():
        @pl.when(j >= 1)
        def _():
            wait_scatter(j - 1, 1 - slot)
        wait_scatter(j, slot)
        for_chunks(nt, lambda i: for_chunks(tail_n_ref[i], lambda k: zero_fill(i, 0).wait()))


def _experts(xl, tables, wg_bf, wu_bf, wd_bf, n_tiles):
    tile_e, nused, nvalid, chunk_map, tail_row, tail_n = tables
    wmap = lambda j, te, *_: (te[j], 0, 0)
    grid_spec = pltpu.PrefetchScalarGridSpec(
        num_scalar_prefetch=6,
        grid=(n_tiles,),
        in_specs=[
            pl.BlockSpec(memory_space=pl.ANY),
            pl.BlockSpec((1, D_MODEL, D_EXPERT), wmap),
            pl.BlockSpec((1, D_MODEL, D_EXPERT), wmap),
            pl.BlockSpec((1, D_EXPERT, D_MODEL), wmap),
        ],
        out_specs=pl.BlockSpec(memory_space=pl.ANY),
        scratch_shapes=[
            pltpu.VMEM((2, TM_EXP, D_MODEL), F32),
            pltpu.VMEM((2, TM_EXP, D_MODEL), F32),
            pltpu.VMEM((ROW_CHUNK, D_MODEL), F32),
            pltpu.SemaphoreType.DMA((2,)),
            pltpu.SemaphoreType.DMA((2,)),
            pltpu.SemaphoreType.DMA,
        ],
    )
    return pl.pallas_call(
        _expert_body,
        grid_spec=grid_spec,
        out_shape=jax.ShapeDtypeStruct(xl.shape, F32),
        compiler_params=_cparams(1),
        name="experts",
    )(tile_e, nused, nvalid, chunk_map, tail_row, tail_n, xl, wg_bf, wu_bf, wd_bf)


def _combine_body(h1_ref, aux_ref, yl_ref, o_ref):
    tm = h1_ref.shape[0]
    loc = yl_ref.shape[0]
    aux = aux_ref[...]
    slot1 = aux[:, 0:1].astype(I32)
    slot2 = aux[:, 1:2].astype(I32)
    sio = lax.broadcasted_iota(I32, (tm, loc), 1)
    gm = jnp.where(sio == slot1, aux[:, 2:3], 0.0) + jnp.where(sio == slot2, aux[:, 3:4], 0.0)
    gm_hi = gm.astype(BF16)
    gm_lo = (gm - gm_hi.astype(F32)).astype(BF16)
    yb = yl_ref[...].astype(BF16)
    o_ref[...] = (h1_ref[...] + jnp.dot(gm_hi, yb, preferred_element_type=F32)
                  + jnp.dot(gm_lo, yb, preferred_element_type=F32))


def _combine(h1, aux, yl, nt, tm, loc):
    return pl.pallas_call(
        _combine_body,
        grid=(nt,),
        in_specs=[
            pl.BlockSpec((tm, D_MODEL), lambda i: (i, 0)),
            pl.BlockSpec((tm, LANES), lambda i: (i, 0)),
            pl.BlockSpec((loc, D_MODEL), lambda i: (i, 0)),
        ],
        out_specs=pl.BlockSpec((tm, D_MODEL), lambda i: (i, 0)),
        out_shape=jax.ShapeDtypeStruct(h1.shape, F32),
        compiler_params=_cparams(1),
        name="combine",
    )(h1, aux, yl)


def _excl_cumsum(a, axis):
    return jnp.cumsum(a, axis=axis) - a


def _expert_tables(grp, n_tiles, loc):
    nt = grp.shape[0]
    gch = grp // ROW_CHUNK
    loc_start = _excl_cumsum(gch, 1)
    used = jnp.sum(gch, axis=1)
    col = jnp.sum(gch, axis=0)
    seg = ((col + CHUNKS_PER_TILE - 1) // CHUNKS_PER_TILE) * CHUNKS_PER_TILE
    seg_end = jnp.cumsum(seg)
    off = seg_end - seg
    tile_first = jnp.arange(n_tiles, dtype=I32) * CHUNKS_PER_TILE
    tile_e = jnp.minimum(jnp.sum(seg_end[None, :] <= tile_first[:, None], axis=1), N_EXPERTS - 1)
    nused = seg_end[-1:] // CHUNKS_PER_TILE
    onehot = tile_e[:, None] == jnp.arange(N_EXPERTS, dtype=I32)[None, :]
    pick = lambda tab: jnp.sum(jnp.where(onehot[:, :, None], tab.T[None], 0), axis=1)
    pick1 = lambda vec: jnp.sum(jnp.where(onehot, vec[None, :], 0), axis=1)
    first = tile_first - pick1(off)
    nvalid = jnp.clip(pick1(col) - first, 0, CHUNKS_PER_TILE)
    cum = jnp.cumsum(gch, axis=0)
    delta = jnp.arange(nt, dtype=I32)[:, None] * (loc // ROW_CHUNK) + loc_start - (cum - gch)
    step = delta - jnp.concatenate([jnp.zeros((1, N_EXPERTS), I32), delta[:-1]], axis=0)
    cum_prev = jnp.concatenate([jnp.full((1, N_EXPERTS), -1, I32), cum[:-1]], axis=0)
    cc = first[:, None] + jnp.arange(CHUNKS_PER_TILE, dtype=I32)[None, :]
    passed = pick(cum_prev)[:, None, :] <= cc[:, :, None]
    chunk_map = cc + jnp.sum(jnp.where(passed, pick(step)[:, None, :], 0), axis=2)
    valid = jnp.arange(CHUNKS_PER_TILE, dtype=I32)[None, :] < nvalid[:, None]
    chunk_map = jnp.where(valid, chunk_map, 0).reshape(n_tiles * CHUNKS_PER_TILE)

    tail_row = jnp.arange(nt, dtype=I32) * loc + used * ROW_CHUNK
    tail_n = loc // ROW_CHUNK - used
    as_i32 = lambda a: a.astype(I32)
    return tuple(map(as_i32, (tile_e, nused, nvalid, chunk_map, tail_row, tail_n)))


def kernel(x, norm1_g, w_in, conv_w, conv_b, w_gate_a, b_gate_a, w_gate_x, b_gate_x, lru_lambda,
           lru_out_g, q_norm_g, k_norm_g, lambda_q1, lambda_k1, lambda_q2, lambda_k2, sub_norm_g,
           w_out, norm2_g, w_router_group, b_router_group, w_router_expert, b_router_expert,
           w_expert_gate, w_expert_up, w_expert_down):
    b, s, d = x.shape
    assert d == D_MODEL and norm1_g.shape[0] == 1
    t = b * s
    l = 0
    x2 = x.reshape(t, d)

    w_in_bf = w_in[l].astype(BF16)
    scale = HEAD_DIM ** -0.5
    qkg = jnp.stack([jnp.tile(q_norm_g[l], 2) * scale, jnp.tile(k_norm_g[l], 2)]).astype(F32)
    eye = jnp.eye(LRU_BLOCKS, dtype=F32)
    blockdiag = lambda w: jnp.einsum("ncd,nm->ncmd", w, eye).reshape(LRU_WIDTH, LRU_WIDTH)
    wgate = jnp.concatenate([blockdiag(w_gate_a[l]), blockdiag(w_gate_x[l])], axis=1).astype(BF16)
    bgate = jnp.concatenate([b_gate_a[l], b_gate_x[l]])[None, :]
    lam_params = jnp.stack([lambda_q1[l], lambda_k1[l], lambda_q2[l], lambda_k2[l]])
    wr = jnp.concatenate(
        [w_router_group[l], jnp.transpose(w_router_expert[l], (1, 0, 2)).reshape(d, N_EXPERTS)], axis=1)
    wr = jnp.pad(wr, ((0, 0), (0, LANES - wr.shape[1])))
    wr_hi = wr.astype(BF16)
    wr_lo = (wr - wr_hi.astype(F32)).astype(BF16)
    br = jnp.pad(jnp.concatenate([b_router_group[l], b_router_expert[l].reshape(-1)]),
                 (0, LANES - N_GROUPS - N_EXPERTS))[None, :]

    proj = _in_proj(x2, norm1_g[l][None, :], w_in_bf, qkg)
    y_lru = _lru(proj, b, s, conv_w[l], conv_b[l][None, :], wgate, bgate,
                 lru_lambda[l][None, :], lru_out_g[l][None, :])
    y_att = _attention(proj, b, s, lam_params, sub_norm_g[l][None, :])
    h1, xl, aux, gt = _out_route(y_lru, y_att, x2, w_out[l].astype(BF16), norm2_g[l][None, :],
                                 wr_hi, wr_lo, br)

    tm = min(TM_TOK, t)
    nt = t // tm
    loc = 2 * tm + N_EXPERTS * ROW_CHUNK
    max_rows = 2 * t + nt * N_EXPERTS * (ROW_CHUNK - 1) + N_EXPERTS * (TM_EXP - ROW_CHUNK)
    n_tiles = -(-max_rows // TM_EXP)
    grp = gt[:, :, 0].astype(I32)
    tables = _expert_tables(grp, n_tiles, loc)
    yl = _experts(xl, tables, w_expert_gate[l].astype(BF16), w_expert_up[l].astype(BF16),
                  w_expert_down[l].astype(BF16), n_tiles)
    out = _combine(h1, aux, yl, nt, tm, loc)
    return out.reshape(b, s, d)
```

```python
import functools
import math

import numpy as np
import jax
import jax.numpy as jnp
from jax import lax
from jax.experimental import pallas as pl
from jax.experimental.pallas import tpu as pltpu

F32 = jnp.float32
BF16 = jnp.bfloat16
I32 = jnp.int32

D_MODEL = 1024
LRU_WIDTH = 512
LRU_BLOCKS = 8
LRU_BLOCK_W = LRU_WIDTH // LRU_BLOCKS
CONV_W = 4
LRU_C = 8.0
ATT_WIDTH = 512
N_HEADS = 4
HEAD_DIM = 64
V_DIM = 128
IN_COLS = 2 * LRU_WIDTH + 3 * ATT_WIDTH
N_GROUPS = 4
EXPERTS_PER_GROUP = 4
N_EXPERTS = N_GROUPS * EXPERTS_PER_GROUP
D_EXPERT = D_MODEL // 2
CHUNK = 64
EPS = 1e-6
NEG_BIG = -1e30
LAMBDA_INIT = 0.8 - 0.6 * math.exp(-0.3 * 0)

LANES = 128
SUBLANES = 8

TM_PROJ = 512
TS_LRU = 256
TQ = 512
TM_TOK = 256
TM_EXP = 256
ROW_CHUNK = 2 * SUBLANES
VMEM_LIMIT = 56 * 1024 * 1024


def _cparams(n_axes):
    return pltpu.CompilerParams(
        dimension_semantics=("arbitrary",) * n_axes, vmem_limit_bytes=VMEM_LIMIT)


def _inproj_body(x_ref, g1_ref, w_ref, qkg_ref, o_ref):
    x = x_ref[...]
    ms = jnp.mean(x * x, axis=-1, keepdims=True)
    hn = ((x * lax.rsqrt(ms + EPS)) * g1_ref[...]).astype(BF16)
    tm = x.shape[0]
    lo_half = lax.broadcasted_iota(I32, (tm, LANES), 1) < HEAD_DIM
    width = 512
    for c in range(IN_COLS // width):
        c0 = c * width
        acc = jnp.dot(hn, w_ref[:, c0:c0 + width], preferred_element_type=F32)
        if c in (2, 3):
            gain = qkg_ref[c - 2:c - 1, :]
            for b in range(width // LANES):
                blk = acc[:, b * LANES:(b + 1) * LANES]
                sq = blk * blk
                s_lo = jnp.sum(jnp.where(lo_half, sq, 0.0), axis=-1, keepdims=True)
                s_hi = jnp.sum(jnp.where(lo_half, 0.0, sq), axis=-1, keepdims=True)
                inv = jnp.where(lo_half,
                                lax.rsqrt(s_lo * (1.0 / HEAD_DIM) + EPS),
                                lax.rsqrt(s_hi * (1.0 / HEAD_DIM) + EPS))
                o_ref[:, c0 + b * LANES:c0 + (b + 1) * LANES] = ((blk * inv) * gain).astype(BF16)
        else:
            o_ref[:, c0:c0 + width] = acc.astype(BF16)


def _in_proj(x2, g1, w_in_bf, qkg):
    t = x2.shape[0]
    tm = min(TM_PROJ, t)
    return pl.pallas_call(
        _inproj_body,
        grid=(t // tm,),
        in_specs=[
            pl.BlockSpec((tm, D_MODEL), lambda i: (i, 0)),
            pl.BlockSpec((1, D_MODEL), lambda i: (0, 0)),
            pl.BlockSpec((D_MODEL, IN_COLS), lambda i: (0, 0)),
            pl.BlockSpec((2, LANES), lambda i: (0, 0)),
        ],
        out_specs=pl.BlockSpec((tm, IN_COLS), lambda i: (i, 0)),
        out_shape=jax.ShapeDtypeStruct((t, IN_COLS), BF16),
        compiler_params=_cparams(1),
        name="in_proj",
    )(x2, g1, w_in_bf, qkg)


def _lru_body(p_ref, cw_ref, cb_ref, wg_ref, bg_ref, lam_ref, og_ref, o_ref, xbuf, hbuf, hc):
    s = pl.program_id(1)
    ts = p_ref.shape[0]

    @pl.when(s == 0)
    def _():
        xbuf[0:SUBLANES, :] = jnp.zeros((SUBLANES, LRU_WIDTH), F32)
        hc[...] = jnp.zeros((1, LRU_WIDTH), F32)

    x = p_ref[:, 0:LRU_WIDTH].astype(F32)
    gl = p_ref[:, LRU_WIDTH:2 * LRU_WIDTH].astype(F32)
    xbuf[SUBLANES:SUBLANES + ts, :] = x
    xc = cb_ref[...]
    for j in range(CONV_W):
        r0 = SUBLANES - (CONV_W - 1) + j
        xc = xc + xbuf[r0:r0 + ts, :] * cw_ref[j:j + 1, :]
    xbuf[0:SUBLANES, :] = x[ts - SUBLANES:ts, :]

    z = jnp.dot(xc.astype(BF16), wg_ref[...], preferred_element_type=F32) + bg_ref[...]
    r = 0.5 * jnp.tanh(0.5 * z[:, 0:LRU_WIDTH]) + 0.5
    gi = 0.5 * jnp.tanh(0.5 * z[:, LRU_WIDTH:2 * LRU_WIDTH]) + 0.5
    nl = -lam_ref[...]
    softplus = jnp.maximum(nl, 0.0) + jnp.log1p(jnp.exp(-jnp.abs(nl)))
    log_a = (-LRU_C) * r * softplus
    a = jnp.exp(log_a)
    v = -jnp.tanh(log_a) * (a * a + 1.0)
    u = jnp.where(v > 0.0, v * lax.rsqrt(v), 0.0) * (gi * xc)

    row = lax.broadcasted_iota(I32, (ts, LRU_WIDTH), 0) & (SUBLANES - 1)
    ca, cb = a, u
    for d in (1, 2, 4):
        a_sh = pltpu.roll(ca, d, axis=0)
        b_sh = pltpu.roll(cb, d, axis=0)
        take = row >= d
        cb = jnp.where(take, ca * b_sh + cb, cb)
        ca = jnp.where(take, ca * a_sh, ca)
    h = hc[...]
    for blk in range(ts // SUBLANES):
        r0 = blk * SUBLANES
        hb = ca[r0:r0 + SUBLANES, :] * h + cb[r0:r0 + SUBLANES, :]
        hbuf[r0:r0 + SUBLANES, :] = hb
        h = hb[SUBLANES - 1:SUBLANES, :]
    hc[...] = h

    y = hbuf[...] * jax.nn.gelu(gl)
    ms = jnp.mean(y * y, axis=-1, keepdims=True)
    o_ref[...] = ((y * lax.rsqrt(ms + EPS)) * og_ref[...]).astype(o_ref.dtype)


def _lru(proj, b, s, conv_w, conv_b, wgate, bgate, lam, out_g):
    ts = min(TS_LRU, s)
    ns = s // ts
    vec = lambda n: pl.BlockSpec((1, n), lambda bi, si: (0, 0))
    return pl.pallas_call(
        _lru_body,
        grid=(b, ns),
        in_specs=[
            pl.BlockSpec((ts, 2 * LRU_WIDTH), lambda bi, si: (bi * ns + si, 0)),
            pl.BlockSpec((CONV_W, LRU_WIDTH), lambda bi, si: (0, 0)),
            vec(LRU_WIDTH),
            pl.BlockSpec((LRU_WIDTH, 2 * LRU_WIDTH), lambda bi, si: (0, 0)),
            vec(2 * LRU_WIDTH),
            vec(LRU_WIDTH),
            vec(LRU_WIDTH),
        ],
        out_specs=pl.BlockSpec((ts, LRU_WIDTH), lambda bi, si: (bi * ns + si, 0)),
        out_shape=jax.ShapeDtypeStruct((b * s, LRU_WIDTH), BF16),
        scratch_shapes=[
            pltpu.VMEM((ts + SUBLANES, LRU_WIDTH), F32),
            pltpu.VMEM((ts, LRU_WIDTH), F32),
            pltpu.VMEM((1, LRU_WIDTH), F32),
        ],
        compiler_params=_cparams(2),
        name="rg_lru",
    )(proj, conv_w, conv_b, wgate, bgate, lam, out_g)


SOFTMAX_ROWS = 32


def _attn_body(slope_ref, q_ref, k_ref, v_ref, bias_ref, lamp_ref, sg_ref, o_ref,
               qs_buf, s0, s1, p0, p1, a0, a1, m_buf, acc0, acc1, *, tq, nq):
    h = pl.program_id(1)
    slope = slope_ref[h]
    s_bufs, p_bufs, a_bufs, accs = (s0, s1), (p0, p1), (a0, a1), (acc0, acc1)
    lo_half = lax.broadcasted_iota(I32, (tq, LANES), 1) < HEAD_DIM
    for i in range(nq):
        q = q_ref[i * tq:(i + 1) * tq, :]
        zero = jnp.zeros_like(q)
        qs_buf[i, 0:tq, :] = jnp.where(lo_half, q, zero)
        qs_buf[i, tq:2 * tq, :] = jnp.where(lo_half, zero, q)
    ones = jnp.ones((tq, V_DIM), BF16)
    lp = lamp_ref[...]
    lam = (jnp.exp(jnp.sum(lp[0:1, :] * lp[1:2, :], axis=-1, keepdims=True))
           - jnp.exp(jnp.sum(lp[2:3, :] * lp[3:4, :], axis=-1, keepdims=True))
           + LAMBDA_INIT)
    pairs = [(i, j) for i in range(nq) for j in range(i + 1)]

    def scores(t):
        i, j = pairs[t]
        s_bufs[t % 2][...] = lax.dot_general(
            qs_buf[i], k_ref[j * tq:(j + 1) * tq, :], (((1,), (1,)), ((), ())),
            preferred_element_type=F32)

    def softmax(t):
        i, j = pairs[t]
        which = 1 if j == i else 0
        shift = slope * float(-(i - j) * tq)
        s_buf, p_buf, a_buf = s_bufs[t % 2], p_bufs[t % 2], a_bufs[t % 2]
        for r in range(2 * tq // SOFTMAX_ROWS):
            rows = slice(r * SOFTMAX_ROWS, (r + 1) * SOFTMAX_ROWS)
            sb = s_buf[rows, :] + bias_ref[0, which, rows, :]
            m_new = jnp.max(sb, axis=-1, keepdims=True) + shift
            if j > 0:
                m_old = m_buf[rows, :]
                m_new = jnp.maximum(m_old, m_new)
                a_buf[rows, :] = jnp.exp(m_old - m_new)
            p_buf[rows, :] = jnp.exp(sb - (m_new - shift)).astype(BF16)
            m_buf[rows, :] = m_new

    def accumulate(t):
        i, j = pairs[t]
        acc = accs[i % 2]
        v_aug = jnp.concatenate([v_ref[j * tq:(j + 1) * tq, :], ones], axis=1)
        pv = jnp.dot(p_bufs[t % 2][...], v_aug, preferred_element_type=F32)
        if j == 0:
            acc[...] = pv
        else:
            acc[...] = a_bufs[t % 2][...] * acc[...] + pv
        if j == i:
            o = (acc[0:tq, 0:V_DIM] / acc[0:tq, V_DIM:V_DIM + 1]
                 - lam * (acc[tq:2 * tq, 0:V_DIM] / acc[tq:2 * tq, V_DIM:V_DIM + 1]))
            ms = jnp.mean(o * o, axis=-1, keepdims=True)
            o = ((o * lax.rsqrt(ms + EPS)) * sg_ref[...]) * (1.0 - LAMBDA_INIT)
            o_ref[i * tq:(i + 1) * tq, :] = o.astype(o_ref.dtype)

    scores(0)
    for t in range(len(pairs)):
        if t + 1 < len(pairs):
            scores(t + 1)
        softmax(t)
        if t >= 1:
            accumulate(t - 1)
    accumulate(len(pairs) - 1)


def _alibi_tables(tq):
    slopes = np.exp2(-8.0 * np.arange(1, N_HEADS + 1, dtype=np.float64) / N_HEADS)
    qi = np.arange(tq)[:, None]
    kj = np.arange(tq)[None, :]
    off = -(slopes[:, None, None] * (qi - kj)[None])
    allowed = (kj // CHUNK) <= (qi // CHUNK)
    diag = np.where(allowed[None], -(slopes[:, None, None] * np.abs(qi - kj)[None]), NEG_BIG)
    tab = np.stack([off, diag], axis=1)
    tab = np.concatenate([tab, tab], axis=2)
    return jnp.asarray(tab, F32), jnp.asarray(slopes, F32)


def _attention(proj, b, s, lam_params, sub_g):
    tq = min(TQ, s)
    nq = s // tq
    bias, slopes = _alibi_tables(tq)
    qcol = 2 * LRU_WIDTH // LANES
    kcol = qcol + ATT_WIDTH // LANES
    vcol = kcol + ATT_WIDTH // LANES
    score_buf = pltpu.VMEM((2 * tq, tq), F32)
    prob_buf = pltpu.VMEM((2 * tq, tq), BF16)
    col_buf = pltpu.VMEM((2 * tq, 1), F32)
    acc_buf = pltpu.VMEM((2 * tq, 2 * V_DIM), F32)
    grid_spec = pltpu.PrefetchScalarGridSpec(
        num_scalar_prefetch=1,
        grid=(b, N_HEADS),
        in_specs=[
            pl.BlockSpec((s, LANES), lambda bi, h, sl: (bi, qcol + h)),
            pl.BlockSpec((s, LANES), lambda bi, h, sl: (bi, kcol + h)),
            pl.BlockSpec((s, LANES), lambda bi, h, sl: (bi, vcol + h)),
            pl.BlockSpec((1, 2, 2 * tq, tq), lambda bi, h, sl: (h, 0, 0, 0)),
            pl.BlockSpec((4, HEAD_DIM), lambda bi, h, sl: (0, 0)),
            pl.BlockSpec((1, V_DIM), lambda bi, h, sl: (0, 0)),
        ],
        out_specs=pl.BlockSpec((s, V_DIM), lambda bi, h, sl: (bi, h)),
        scratch_shapes=[
            pltpu.VMEM((nq, 2 * tq, LANES), BF16),
            score_buf, score_buf, prob_buf, prob_buf, col_buf, col_buf, col_buf, acc_buf, acc_buf,
        ],
    )
    return pl.pallas_call(
        functools.partial(_attn_body, tq=tq, nq=nq),
        grid_spec=grid_spec,
        out_shape=jax.ShapeDtypeStruct((b * s, ATT_WIDTH), BF16),
        compiler_params=_cparams(2),
        name="diff_attn",
    )(slopes, proj, proj, proj, bias, lam_params, sub_g)


def _first_max4(v0, v1, v2, v3):
    m = jnp.maximum(jnp.maximum(v0, v1), jnp.maximum(v2, v3))
    idx = jnp.where(v0 == m, 0, jnp.where(v1 == m, 1, jnp.where(v2 == m, 2, 3))).astype(I32)
    return m, idx


def _route_body(yl_ref, ya_ref, x_ref, wo_ref, g2_ref, wr2_ref, br_ref,
                h1_ref, xl_ref, aux_ref, gt_ref):
    tm = x_ref.shape[0]
    mix = (jnp.dot(yl_ref[...], wo_ref[0:LRU_WIDTH, :], preferred_element_type=F32)
           + jnp.dot(ya_ref[...], wo_ref[LRU_WIDTH:, :], preferred_element_type=F32))
    h1 = x_ref[...] + mix
    h1_ref[...] = h1
    ms = jnp.mean(h1 * h1, axis=-1, keepdims=True)
    hn = (h1 * lax.rsqrt(ms + EPS)) * g2_ref[...]
    hn_hi = hn.astype(BF16)
    hn_lo = (hn - hn_hi.astype(F32)).astype(BF16)
    logits = (jnp.dot(hn_hi, wr2_ref[:, 0:LANES], preferred_element_type=F32)
              + jnp.dot(hn_hi, wr2_ref[:, LANES:2 * LANES], preferred_element_type=F32)
              + jnp.dot(hn_lo, wr2_ref[:, 0:LANES], preferred_element_type=F32)) + br_ref[...]
    lt = logits.T
    row = lambda n: lt[n:n + 1, :]
    gmax, gidx = _first_max4(row(0), row(1), row(2), row(3))
    zg = (jnp.exp(row(0) - gmax) + jnp.exp(row(1) - gmax)
          + jnp.exp(row(2) - gmax) + jnp.exp(row(3) - gmax))
    g_gate = 1.0 / zg
    base = N_GROUPS
    sel = [jnp.where(gidx == 0, row(base + j),
                     jnp.where(gidx == 1, row(base + 4 + j),
                               jnp.where(gidx == 2, row(base + 8 + j), row(base + 12 + j))))
           for j in range(EXPERTS_PER_GROUP)]
    m1, i1 = _first_max4(*sel)
    ze = sum(jnp.exp(sj - m1) for sj in sel)
    rest = [jnp.where(i1 == j, -jnp.inf, sel[j]) for j in range(EXPERTS_PER_GROUP)]
    m2, i2 = _first_max4(*rest)
    p1 = 1.0 / ze
    p2 = jnp.exp(m2 - m1) / ze
    gate1 = g_gate * (p1 / (p1 + p2))
    gate2 = g_gate * (p2 / (p1 + p2))
    e1 = gidx * EXPERTS_PER_GROUP + i1
    e2 = gidx * EXPERTS_PER_GROUP + i2

    eio = lax.broadcasted_iota(I32, (N_EXPERTS, tm), 0)
    oh1 = eio == e1
    oh2 = eio == e2
    both = (oh1 | oh2).astype(F32)
    cnt = jnp.sum(both, axis=1, keepdims=True)
    grp = jnp.floor((cnt + (ROW_CHUNK - 1)) * (1.0 / ROW_CHUNK)) * ROW_CHUNK
    ti = lax.broadcasted_iota(I32, (tm, tm), 0)
    tj = lax.broadcasted_iota(I32, (tm, tm), 1)
    before = (ti < tj).astype(BF16)
    rank = jnp.dot(both.astype(BF16), before, preferred_element_type=F32)
    start1 = jnp.sum(jnp.where(eio < e1, grp, 0.0), axis=0, keepdims=True)
    start2 = jnp.sum(jnp.where(eio < e2, grp, 0.0), axis=0, keepdims=True)
    slot1 = start1 + jnp.sum(jnp.where(oh1, rank, 0.0), axis=0, keepdims=True)
    slot2 = start2 + jnp.sum(jnp.where(oh2, rank, 0.0), axis=0, keepdims=True)

    loc = xl_ref.shape[0]
    sio = lax.broadcasted_iota(I32, (loc, tm), 0)
    perm = ((sio == slot1.astype(I32)) | (sio == slot2.astype(I32))).astype(BF16)
    xs = jnp.dot(perm, hn_hi, preferred_element_type=F32)
    xl_ref[...] = xs.astype(BF16)

    rio = lax.broadcasted_iota(I32, (LANES, tm), 0)
    aux_t = jnp.where(rio == 0, slot1, jnp.where(rio == 1, slot2,
                      jnp.where(rio == 2, gate1, jnp.where(rio == 3, gate2, 0.0))))
    aux_ref[...] = aux_t.T
    gt_ref[0] = jnp.broadcast_to(grp, (N_EXPERTS, LANES))


def _out_route(y_lru, y_att, x2, wo_bf, g2, wr2, br):
    t = x2.shape[0]
    tm = min(TM_TOK, t)
    nt = t // tm
    loc = 2 * tm + N_EXPERTS * ROW_CHUNK
    const = lambda shape: pl.BlockSpec(shape, lambda i: (0,) * len(shape))
    return pl.pallas_call(
        _route_body,
        grid=(nt,),
        in_specs=[
            pl.BlockSpec((tm, LRU_WIDTH), lambda i: (i, 0)),
            pl.BlockSpec((tm, ATT_WIDTH), lambda i: (i, 0)),
            pl.BlockSpec((tm, D_MODEL), lambda i: (i, 0)),
            const((D_MODEL, D_MODEL)),
            const((1, D_MODEL)),
            const((D_MODEL, 2 * LANES)),
            const((1, LANES)),
        ],
        out_specs=[
            pl.BlockSpec((tm, D_MODEL), lambda i: (i, 0)),
            pl.BlockSpec((loc, D_MODEL), lambda i: (i, 0)),
            pl.BlockSpec((tm, LANES), lambda i: (i, 0)),
            pl.BlockSpec((1, N_EXPERTS, LANES), lambda i: (i, 0, 0)),
        ],
        out_shape=[
            jax.ShapeDtypeStruct((t, D_MODEL), F32),
            jax.ShapeDtypeStruct((nt * loc, D_MODEL), BF16),
            jax.ShapeDtypeStruct((t, LANES), F32),
            jax.ShapeDtypeStruct((nt, N_EXPERTS, LANES), F32),
        ],
        compiler_params=_cparams(1),
        name="out_route",
    )(y_lru, y_att, x2, wo_bf, g2, wr2, br)


CHUNKS_PER_TILE = TM_EXP // ROW_CHUNK


def _expert_body(tile_e_ref, nused_ref, nvalid_ref, chunk_ref, tail_row_ref, tail_n_ref,
                 xl_ref, wg_ref, wu_ref, wd_ref, yl_ref,
                 xbuf, ybuf, zbuf, gsem, ssem, zsem):
    j = pl.program_id(0)
    nused = nused_ref[0]
    nt = tail_n_ref.shape[0]
    slot = lax.rem(j, 2)

    def rows(c):
        if isinstance(c, int):
            return pl.ds(c * ROW_CHUNK, ROW_CHUNK)
        return pl.ds(pl.multiple_of(c * ROW_CHUNK, ROW_CHUNK), ROW_CHUNK)

    def gather(t, sl, k):
        return pltpu.make_async_copy(
            xl_ref.at[rows(chunk_ref[t * CHUNKS_PER_TILE + k])], xbuf.at[sl, rows(k)], gsem.at[sl])

    def scatter(t, sl, k):
        return pltpu.make_async_copy(
            ybuf.at[sl, rows(k)], yl_ref.at[rows(chunk_ref[t * CHUNKS_PER_TILE + k])], ssem.at[sl])

    def for_chunks(n, fn):
        def body(k, c):
            fn(k)
            return c
        lax.fori_loop(0, n, body, 0)

    def start_tile(n, fn):
        @pl.when(n == CHUNKS_PER_TILE)
        def _():
            for k in range(CHUNKS_PER_TILE):
                fn(k).start()

        @pl.when(n < CHUNKS_PER_TILE)
        def _():
            for_chunks(n, lambda k: fn(k).start())

    def wait_tile(n, chunk_copy, tile_copy):
        @pl.when(n == CHUNKS_PER_TILE)
        def _():
            tile_copy.wait()

        @pl.when(n < CHUNKS_PER_TILE)
        def _():
            for_chunks(n, lambda k: chunk_copy.wait())

    def wait_gather(t, sl):
        wait_tile(nvalid_ref[t], gather(t, sl, 0),
                  pltpu.make_async_copy(xl_ref.at[pl.ds(0, TM_EXP)], xbuf.at[sl], gsem.at[sl]))

    def wait_scatter(t, sl):
        wait_tile(nvalid_ref[t], scatter(t, sl, 0),
                  pltpu.make_async_copy(ybuf.at[sl], yl_ref.at[pl.ds(0, TM_EXP)], ssem.at[sl]))

    def zero_fill(i, k):
        return pltpu.make_async_copy(
            zbuf, yl_ref.at[pl.ds(pl.multiple_of(tail_row_ref[i] + k * ROW_CHUNK, ROW_CHUNK), ROW_CHUNK)],
            zsem)

    @pl.when(j == 0)
    def _():
        xbuf[...] = jnp.zeros(xbuf.shape, BF16)
        zbuf[...] = jnp.zeros(zbuf.shape, BF16)
        for_chunks(nt, lambda i: for_chunks(tail_n_ref[i], lambda k: zero_fill(i, k).start()))
        start_tile(nvalid_ref[0], lambda k: gather(0, 0, k))

    @pl.when(j + 1 < nused)
    def _():
        start_tile(nvalid_ref[j + 1], lambda k: gather(j + 1, 1 - slot, k))

    @pl.when(j < nused)
    def _():
        wait_gather(j, slot)

        @pl.when(j >= 2)
        def _():
            wait_scatter(j - 2, slot)

        xb = xbuf[slot]
        gate = jnp.dot(xb, wg_ref[0], preferred_element_type=F32)
        up = jnp.dot(xb, wu_ref[0], preferred_element_type=F32)
        hid = (jax.nn.silu(gate) * up).astype(BF16)
        ybuf[slot] = jnp.dot(hid, wd_ref[0], preferred_element_type=F32).astype(BF16)
        start_tile(nvalid_ref[j], lambda k: scatter(j, slot, k))

    @pl.when(j == nused - 1)
    def _():
        @pl.when(j >= 1)
        def _():
            wait_scatter(j - 1, 1 - slot)
        wait_scatter(j, slot)
        for_chunks(nt, lambda i: for_chunks(tail_n_ref[i], lambda k: zero_fill(i, 0).wait()))


def _experts(xl, tables, wg_bf, wu_bf, wd_bf, n_tiles):
    tile_e, nused, nvalid, chunk_map, tail_row, tail_n = tables
    wmap = lambda j, te, *_: (te[j], 0, 0)
    grid_spec = pltpu.PrefetchScalarGridSpec(
        num_scalar_prefetch=6,
        grid=(n_tiles,),
        in_specs=[
            pl.BlockSpec(memory_space=pl.ANY),
            pl.BlockSpec((1, D_MODEL, D_EXPERT), wmap),
            pl.BlockSpec((1, D_MODEL, D_EXPERT), wmap),
            pl.BlockSpec((1, D_EXPERT, D_MODEL), wmap),
        ],
        out_specs=pl.BlockSpec(memory_space=pl.ANY),
        scratch_shapes=[
            pltpu.VMEM((2, TM_EXP, D_MODEL), BF16),
            pltpu.VMEM((2, TM_EXP, D_MODEL), BF16),
            pltpu.VMEM((ROW_CHUNK, D_MODEL), BF16),
            pltpu.SemaphoreType.DMA((2,)),
            pltpu.SemaphoreType.DMA((2,)),
            pltpu.SemaphoreType.DMA,
        ],
    )
    return pl.pallas_call(
        _expert_body,
        grid_spec=grid_spec,
        out_shape=jax.ShapeDtypeStruct(xl.shape, BF16),
        compiler_params=_cparams(1),
        name="experts",
    )(tile_e, nused, nvalid, chunk_map, tail_row, tail_n, xl, wg_bf, wu_bf, wd_bf)


def _combine_body(h1_ref, aux_ref, yl_ref, o_ref):
    tm = h1_ref.shape[0]
    loc = yl_ref.shape[0]
    aux = aux_ref[...]
    slot1 = aux[:, 0:1].astype(I32)
    slot2 = aux[:, 1:2].astype(I32)
    sio = lax.broadcasted_iota(I32, (tm, loc), 1)
    gm = jnp.where(sio == slot1, aux[:, 2:3], 0.0) + jnp.where(sio == slot2, aux[:, 3:4], 0.0)
    gm_hi = gm.astype(BF16)
    gm_lo = (gm - gm_hi.astype(F32)).astype(BF16)
    yb = yl_ref[...]
    o_ref[...] = (h1_ref[...] + jnp.dot(gm_hi, yb, preferred_element_type=F32)
                  + jnp.dot(gm_lo, yb, preferred_element_type=F32))


def _combine(h1, aux, yl, nt, tm, loc):
    return pl.pallas_call(
        _combine_body,
        grid=(nt,),
        in_specs=[
            pl.BlockSpec((tm, D_MODEL), lambda i: (i, 0)),
            pl.BlockSpec((tm, LANES), lambda i: (i, 0)),
            pl.BlockSpec((loc, D_MODEL), lambda i: (i, 0)),
        ],
        out_specs=pl.BlockSpec((tm, D_MODEL), lambda i: (i, 0)),
        out_shape=jax.ShapeDtypeStruct(h1.shape, F32),
        compiler_params=_cparams(1),
        name="combine",
    )(h1, aux, yl)


def _excl_cumsum(a, axis):
    return jnp.cumsum(a, axis=axis) - a


def _expert_tables(grp, n_tiles, loc):
    nt = grp.shape[0]
    gch = grp // ROW_CHUNK
    loc_start = _excl_cumsum(gch, 1)
    used = jnp.sum(gch, axis=1)
    col = jnp.sum(gch, axis=0)
    seg = ((col + CHUNKS_PER_TILE - 1) // CHUNKS_PER_TILE) * CHUNKS_PER_TILE
    seg_end = jnp.cumsum(seg)
    off = seg_end - seg
    tile_first = jnp.arange(n_tiles, dtype=I32) * CHUNKS_PER_TILE
    tile_e = jnp.minimum(jnp.sum(seg_end[None, :] <= tile_first[:, None], axis=1), N_EXPERTS - 1)
    nused = seg_end[-1:] // CHUNKS_PER_TILE
    onehot = tile_e[:, None] == jnp.arange(N_EXPERTS, dtype=I32)[None, :]
    pick = lambda tab: jnp.sum(jnp.where(onehot[:, :, None], tab.T[None], 0), axis=1)
    pick1 = lambda vec: jnp.sum(jnp.where(onehot, vec[None, :], 0), axis=1)
    first = tile_first - pick1(off)
    nvalid = jnp.clip(pick1(col) - first, 0, CHUNKS_PER_TILE)
    cum = jnp.cumsum(gch, axis=0)
    delta = jnp.arange(nt, dtype=I32)[:, None] * (loc // ROW_CHUNK) + loc_start - (cum - gch)
    step = delta - jnp.concatenate([jnp.zeros((1, N_EXPERTS), I32), delta[:-1]], axis=0)
    cum_prev = jnp.concatenate([jnp.full((1, N_EXPERTS), -1, I32), cum[:-1]], axis=0)
    cc = first[:, None] + jnp.arange(CHUNKS_PER_TILE, dtype=I32)[None, :]
    passed = pick(cum_prev)[:, None, :] <= cc[:, :, None]
    chunk_map = cc + jnp.sum(jnp.where(passed, pick(step)[:, None, :], 0), axis=2)
    valid = jnp.arange(CHUNKS_PER_TILE, dtype=I32)[None, :] < nvalid[:, None]
    chunk_map = jnp.where(valid, chunk_map, 0).reshape(n_tiles * CHUNKS_PER_TILE)

    tail_row = jnp.arange(nt, dtype=I32) * loc + used * ROW_CHUNK
    tail_n = loc // ROW_CHUNK - used
    as_i32 = lambda a: a.astype(I32)
    return tuple(map(as_i32, (tile_e, nused, nvalid, chunk_map, tail_row, tail_n)))


def kernel(x, norm1_g, w_in, conv_w, conv_b, w_gate_a, b_gate_a, w_gate_x, b_gate_x, lru_lambda,
           lru_out_g, q_norm_g, k_norm_g, lambda_q1, lambda_k1, lambda_q2, lambda_k2, sub_norm_g,
           w_out, norm2_g, w_router_group, b_router_group, w_router_expert, b_router_expert,
           w_expert_gate, w_expert_up, w_expert_down):
    b, s, d = x.shape
    assert d == D_MODEL and norm1_g.shape[0] == 1
    t = b * s
    l = 0
    x2 = x.reshape(t, d)

    w_in_bf = w_in[l].astype(BF16)
    scale = HEAD_DIM ** -0.5
    qkg = jnp.stack([jnp.tile(q_norm_g[l], 2) * scale, jnp.tile(k_norm_g[l], 2)]).astype(F32)
    eye = jnp.eye(LRU_BLOCKS, dtype=F32)
    blockdiag = lambda w: jnp.einsum("ncd,nm->ncmd", w, eye).reshape(LRU_WIDTH, LRU_WIDTH)
    wgate = jnp.concatenate([blockdiag(w_gate_a[l]), blockdiag(w_gate_x[l])], axis=1).astype(BF16)
    bgate = jnp.concatenate([b_gate_a[l], b_gate_x[l]])[None, :]
    lam_params = jnp.stack([lambda_q1[l], lambda_k1[l], lambda_q2[l], lambda_k2[l]])
    wr = jnp.concatenate(
        [w_router_group[l], jnp.transpose(w_router_expert[l], (1, 0, 2)).reshape(d, N_EXPERTS)], axis=1)
    wr = jnp.pad(wr, ((0, 0), (0, LANES - wr.shape[1])))
    wr_hi = wr.astype(BF16)
    wr2 = jnp.concatenate([wr_hi, (wr - wr_hi.astype(F32)).astype(BF16)], axis=1)
    br = jnp.pad(jnp.concatenate([b_router_group[l], b_router_expert[l].reshape(-1)]),
                 (0, LANES - N_GROUPS - N_EXPERTS))[None, :]

    proj = _in_proj(x2, norm1_g[l][None, :], w_in_bf, qkg)
    y_lru = _lru(proj, b, s, conv_w[l], conv_b[l][None, :], wgate, bgate,
                 lru_lambda[l][None, :], lru_out_g[l][None, :])
    y_att = _attention(proj, b, s, lam_params, sub_norm_g[l][None, :])
    h1, xl, aux, gt = _out_route(y_lru, y_att, x2, w_out[l].astype(BF16), norm2_g[l][None, :],
                                 wr2, br)

    tm = min(TM_TOK, t)
    nt = t // tm
    loc = 2 * tm + N_EXPERTS * ROW_CHUNK
    max_rows = 2 * t + nt * N_EXPERTS * (ROW_CHUNK - 1) + N_EXPERTS * (TM_EXP - ROW_CHUNK)
    n_tiles = -(-max_rows // TM_EXP)
    grp = gt[:, :, 0].astype(I32)
    tables = _expert_tables(grp, n_tiles, loc)
    yl = _experts(xl, tables, w_expert_gate[l].astype(BF16), w_expert_up[l].astype(BF16),
                  w_expert_down[l].astype(BF16), n_tiles)
    out = _combine(h1, aux, yl, nt, tm, loc)
    return out.reshape(b, s, d)
```

```python
import functools
import math

import numpy as np
import jax
import jax.numpy as jnp
from jax import lax
from jax.experimental import pallas as pl
from jax.experimental.pallas import tpu as pltpu

F32 = jnp.float32
BF16 = jnp.bfloat16
I32 = jnp.int32

D_MODEL = 1024
LRU_WIDTH = 512
LRU_BLOCKS = 8
LRU_BLOCK_W = LRU_WIDTH // LRU_BLOCKS
CONV_W = 4
LRU_C = 8.0
ATT_WIDTH = 512
N_HEADS = 4
HEAD_DIM = 64
V_DIM = 128
IN_COLS = 2 * LRU_WIDTH + 3 * ATT_WIDTH
N_GROUPS = 4
EXPERTS_PER_GROUP = 4
N_EXPERTS = N_GROUPS * EXPERTS_PER_GROUP
D_EXPERT = D_MODEL // 2
CHUNK = 64
EPS = 1e-6
NEG_BIG = -1e30
LAMBDA_INIT = 0.8 - 0.6 * math.exp(-0.3 * 0)
LOG2E = math.log2(math.e)

LANES = 128
SUBLANES = 8

TM_PROJ = 512
TS_LRU = 256
TQ = 512
TM_TOK = 256
TM_EXP = 512
ROW_CHUNK = 2 * SUBLANES
VMEM_LIMIT = 56 * 1024 * 1024


def _cparams(n_axes):
    return pltpu.CompilerParams(
        dimension_semantics=("arbitrary",) * n_axes, vmem_limit_bytes=VMEM_LIMIT)


def _inproj_body(x_ref, g1_ref, w_ref, qkg_ref, o_ref):
    x = x_ref[...]
    ms = jnp.mean(x * x, axis=-1, keepdims=True)
    hn = ((x * lax.rsqrt(ms + EPS)) * g1_ref[...]).astype(BF16)
    tm = x.shape[0]
    lo_half = lax.broadcasted_iota(I32, (tm, LANES), 1) < HEAD_DIM
    width = 512
    for c in range(IN_COLS // width):
        c0 = c * width
        acc = jnp.dot(hn, w_ref[:, c0:c0 + width], preferred_element_type=F32)
        if c in (2, 3):
            gain = qkg_ref[c - 2:c - 1, :]
            for b in range(width // LANES):
                blk = acc[:, b * LANES:(b + 1) * LANES]
                sq = blk * blk
                s_lo = jnp.sum(jnp.where(lo_half, sq, 0.0), axis=-1, keepdims=True)
                s_hi = jnp.sum(jnp.where(lo_half, 0.0, sq), axis=-1, keepdims=True)
                inv = jnp.where(lo_half,
                                lax.rsqrt(s_lo * (1.0 / HEAD_DIM) + EPS),
                                lax.rsqrt(s_hi * (1.0 / HEAD_DIM) + EPS))
                o_ref[:, c0 + b * LANES:c0 + (b + 1) * LANES] = ((blk * inv) * gain).astype(BF16)
        else:
            o_ref[:, c0:c0 + width] = acc.astype(BF16)


def _in_proj(x2, g1, w_in_bf, qkg):
    t = x2.shape[0]
    tm = min(TM_PROJ, t)
    return pl.pallas_call(
        _inproj_body,
        grid=(t // tm,),
        in_specs=[
            pl.BlockSpec((tm, D_MODEL), lambda i: (i, 0)),
            pl.BlockSpec((1, D_MODEL), lambda i: (0, 0)),
            pl.BlockSpec((D_MODEL, IN_COLS), lambda i: (0, 0)),
            pl.BlockSpec((2, LANES), lambda i: (0, 0)),
        ],
        out_specs=pl.BlockSpec((tm, IN_COLS), lambda i: (i, 0)),
        out_shape=jax.ShapeDtypeStruct((t, IN_COLS), BF16),
        compiler_params=_cparams(1),
        name="in_proj",
    )(x2, g1, w_in_bf, qkg)


def _lru_body(p_ref, cw_ref, cb_ref, wg_ref, bg_ref, lam_ref, og_ref, o_ref, xbuf, hbuf, hc):
    s = pl.program_id(1)
    ts = p_ref.shape[0]

    @pl.when(s == 0)
    def _():
        xbuf[0:SUBLANES, :] = jnp.zeros((SUBLANES, LRU_WIDTH), F32)
        hc[...] = jnp.zeros((1, LRU_WIDTH), F32)

    x = p_ref[:, 0:LRU_WIDTH].astype(F32)
    gl = p_ref[:, LRU_WIDTH:2 * LRU_WIDTH].astype(F32)
    xbuf[SUBLANES:SUBLANES + ts, :] = x
    xc = cb_ref[...]
    for j in range(CONV_W):
        r0 = SUBLANES - (CONV_W - 1) + j
        xc = xc + xbuf[r0:r0 + ts, :] * cw_ref[j:j + 1, :]
    xbuf[0:SUBLANES, :] = x[ts - SUBLANES:ts, :]

    z = jnp.dot(xc.astype(BF16), wg_ref[...], preferred_element_type=F32) + bg_ref[...]
    r = 0.5 * jnp.tanh(0.5 * z[:, 0:LRU_WIDTH]) + 0.5
    gi = 0.5 * jnp.tanh(0.5 * z[:, LRU_WIDTH:2 * LRU_WIDTH]) + 0.5
    nl = -lam_ref[...]
    softplus = jnp.maximum(nl, 0.0) + jnp.log1p(jnp.exp(-jnp.abs(nl)))
    log_a = (-LRU_C) * r * softplus
    a = jnp.exp(log_a)
    v = -jnp.tanh(log_a) * (a * a + 1.0)
    u = jnp.where(v > 0.0, v * lax.rsqrt(v), 0.0) * (gi * xc)

    row = lax.broadcasted_iota(I32, (ts, LRU_WIDTH), 0) & (SUBLANES - 1)
    ca, cb = a, u
    for d in (1, 2, 4):
        a_sh = pltpu.roll(ca, d, axis=0)
        b_sh = pltpu.roll(cb, d, axis=0)
        take = row >= d
        cb = jnp.where(take, ca * b_sh + cb, cb)
        ca = jnp.where(take, ca * a_sh, ca)
    h = hc[...]
    for blk in range(ts // SUBLANES):
        r0 = blk * SUBLANES
        hb = ca[r0:r0 + SUBLANES, :] * h + cb[r0:r0 + SUBLANES, :]
        hbuf[r0:r0 + SUBLANES, :] = hb
        h = hb[SUBLANES - 1:SUBLANES, :]
    hc[...] = h

    y = hbuf[...] * jax.nn.gelu(gl)
    ms = jnp.mean(y * y, axis=-1, keepdims=True)
    o_ref[...] = ((y * lax.rsqrt(ms + EPS)) * og_ref[...]).astype(o_ref.dtype)


def _lru(proj, b, s, conv_w, conv_b, wgate, bgate, lam, out_g):
    ts = min(TS_LRU, s)
    ns = s // ts
    vec = lambda n: pl.BlockSpec((1, n), lambda bi, si: (0, 0))
    return pl.pallas_call(
        _lru_body,
        grid=(b, ns),
        in_specs=[
            pl.BlockSpec((ts, 2 * LRU_WIDTH), lambda bi, si: (bi * ns + si, 0)),
            pl.BlockSpec((CONV_W, LRU_WIDTH), lambda bi, si: (0, 0)),
            vec(LRU_WIDTH),
            pl.BlockSpec((LRU_WIDTH, 2 * LRU_WIDTH), lambda bi, si: (0, 0)),
            vec(2 * LRU_WIDTH),
            vec(LRU_WIDTH),
            vec(LRU_WIDTH),
        ],
        out_specs=pl.BlockSpec((ts, LRU_WIDTH), lambda bi, si: (bi * ns + si, 0)),
        out_shape=jax.ShapeDtypeStruct((b * s, LRU_WIDTH), BF16),
        scratch_shapes=[
            pltpu.VMEM((ts + SUBLANES, LRU_WIDTH), F32),
            pltpu.VMEM((ts, LRU_WIDTH), F32),
            pltpu.VMEM((1, LRU_WIDTH), F32),
        ],
        compiler_params=_cparams(2),
        name="rg_lru",
    )(proj, conv_w, conv_b, wgate, bgate, lam, out_g)


SOFTMAX_ROWS = 32


def _attn_body(slope_ref, q_ref, k_ref, v_ref, bias_ref, lamp_ref, sg_ref, o_ref,
               qs_buf, s0, s1, p0, p1, a0, a1, m_buf, acc0, acc1, *, tq, nq):
    h = pl.program_id(1)
    slope = slope_ref[h]
    s_bufs, p_bufs, a_bufs, accs = (s0, s1), (p0, p1), (a0, a1), (acc0, acc1)
    lo_half = lax.broadcasted_iota(I32, (tq, LANES), 1) < HEAD_DIM
    for i in range(nq):
        q = q_ref[i * tq:(i + 1) * tq, :]
        zero = jnp.zeros_like(q)
        qs_buf[i, 0:tq, :] = jnp.where(lo_half, q, zero)
        qs_buf[i, tq:2 * tq, :] = jnp.where(lo_half, zero, q)
    ones = jnp.ones((tq, V_DIM), BF16)
    lp = lamp_ref[...]
    lam = (jnp.exp(jnp.sum(lp[0:1, :] * lp[1:2, :], axis=-1, keepdims=True))
           - jnp.exp(jnp.sum(lp[2:3, :] * lp[3:4, :], axis=-1, keepdims=True))
           + LAMBDA_INIT)
    pairs = [(i, j) for i in range(nq) for j in range(i + 1)]

    def scores(t):
        i, j = pairs[t]
        s_bufs[t % 2][...] = lax.dot_general(
            qs_buf[i], k_ref[j * tq:(j + 1) * tq, :], (((1,), (1,)), ((), ())),
            preferred_element_type=F32)

    def softmax(t):
        i, j = pairs[t]
        which = 1 if j == i else 0
        shift = slope * float(-(i - j) * tq)
        s_buf, p_buf, a_buf = s_bufs[t % 2], p_bufs[t % 2], a_bufs[t % 2]
        for r in range(2 * tq // SOFTMAX_ROWS):
            rows = slice(r * SOFTMAX_ROWS, (r + 1) * SOFTMAX_ROWS)
            sb = s_buf[rows, :] + bias_ref[0, which, rows, :]
            m_new = jnp.max(sb, axis=-1, keepdims=True) + shift
            if j > 0:
                m_old = m_buf[rows, :]
                m_new = jnp.maximum(m_old, m_new)
                a_buf[rows, :] = jnp.exp2(m_old - m_new)
            p_buf[rows, :] = jnp.exp2(sb - (m_new - shift)).astype(BF16)
            m_buf[rows, :] = m_new

    def accumulate(t):
        i, j = pairs[t]
        acc = accs[i % 2]
        v_aug = jnp.concatenate([v_ref[j * tq:(j + 1) * tq, :], ones], axis=1)
        pv = jnp.dot(p_bufs[t % 2][...], v_aug, preferred_element_type=F32)
        if j == 0:
            acc[...] = pv
        else:
            acc[...] = a_bufs[t % 2][...] * acc[...] + pv
        if j == i:
            o = (acc[0:tq, 0:V_DIM] / acc[0:tq, V_DIM:V_DIM + 1]
                 - lam * (acc[tq:2 * tq, 0:V_DIM] / acc[tq:2 * tq, V_DIM:V_DIM + 1]))
            ms = jnp.mean(o * o, axis=-1, keepdims=True)
            o = ((o * lax.rsqrt(ms + EPS)) * sg_ref[...]) * (1.0 - LAMBDA_INIT)
            o_ref[i * tq:(i + 1) * tq, :] = o.astype(o_ref.dtype)

    scores(0)
    for t in range(len(pairs)):
        if t + 1 < len(pairs):
            scores(t + 1)
        softmax(t)
        if t >= 1:
            accumulate(t - 1)
    accumulate(len(pairs) - 1)


def _alibi_tables(tq):
    slopes = np.exp2(-8.0 * np.arange(1, N_HEADS + 1, dtype=np.float64) / N_HEADS)
    qi = np.arange(tq)[:, None]
    kj = np.arange(tq)[None, :]
    off = -(slopes[:, None, None] * (qi - kj)[None])
    allowed = (kj // CHUNK) <= (qi // CHUNK)
    diag = np.where(allowed[None], -(slopes[:, None, None] * np.abs(qi - kj)[None]), NEG_BIG)
    tab = np.stack([off, diag], axis=1)
    tab = np.concatenate([tab, tab], axis=2)
    return jnp.asarray(tab * LOG2E, F32), jnp.asarray(slopes * LOG2E, F32)


def _attention(proj, b, s, lam_params, sub_g):
    tq = min(TQ, s)
    nq = s // tq
    bias, slopes = _alibi_tables(tq)
    qcol = 2 * LRU_WIDTH // LANES
    kcol = qcol + ATT_WIDTH // LANES
    vcol = kcol + ATT_WIDTH // LANES
    score_buf = pltpu.VMEM((2 * tq, tq), F32)
    prob_buf = pltpu.VMEM((2 * tq, tq), BF16)
    col_buf = pltpu.VMEM((2 * tq, 1), F32)
    acc_buf = pltpu.VMEM((2 * tq, 2 * V_DIM), F32)
    grid_spec = pltpu.PrefetchScalarGridSpec(
        num_scalar_prefetch=1,
        grid=(b, N_HEADS),
        in_specs=[
            pl.BlockSpec((s, LANES), lambda bi, h, sl: (bi, qcol + h)),
            pl.BlockSpec((s, LANES), lambda bi, h, sl: (bi, kcol + h)),
            pl.BlockSpec((s, LANES), lambda bi, h, sl: (bi, vcol + h)),
            pl.BlockSpec((1, 2, 2 * tq, tq), lambda bi, h, sl: (h, 0, 0, 0)),
            pl.BlockSpec((4, HEAD_DIM), lambda bi, h, sl: (0, 0)),
            pl.BlockSpec((1, V_DIM), lambda bi, h, sl: (0, 0)),
        ],
        out_specs=pl.BlockSpec((s, V_DIM), lambda bi, h, sl: (bi, h)),
        scratch_shapes=[
            pltpu.VMEM((nq, 2 * tq, LANES), BF16),
            score_buf, score_buf, prob_buf, prob_buf, col_buf, col_buf, col_buf, acc_buf, acc_buf,
        ],
    )
    return pl.pallas_call(
        functools.partial(_attn_body, tq=tq, nq=nq),
        grid_spec=grid_spec,
        out_shape=jax.ShapeDtypeStruct((b * s, ATT_WIDTH), BF16),
        compiler_params=_cparams(2),
        name="diff_attn",
    )(slopes, proj, proj, proj, bias, lam_params, sub_g)


def _first_max4(v0, v1, v2, v3):
    m = jnp.maximum(jnp.maximum(v0, v1), jnp.maximum(v2, v3))
    idx = jnp.where(v0 == m, 0, jnp.where(v1 == m, 1, jnp.where(v2 == m, 2, 3))).astype(I32)
    return m, idx


def _route_body(yl_ref, ya_ref, x_ref, wo_ref, g2_ref, wr2_ref, br_ref,
                h1_ref, xl_ref, aux_ref, gt_ref):
    tm = x_ref.shape[0]
    mix = (jnp.dot(yl_ref[...], wo_ref[0:LRU_WIDTH, :], preferred_element_type=F32)
           + jnp.dot(ya_ref[...], wo_ref[LRU_WIDTH:, :], preferred_element_type=F32))
    h1 = x_ref[...] + mix
    h1_ref[...] = h1
    ms = jnp.mean(h1 * h1, axis=-1, keepdims=True)
    hn = (h1 * lax.rsqrt(ms + EPS)) * g2_ref[...]
    hn_hi = hn.astype(BF16)
    hn_lo = (hn - hn_hi.astype(F32)).astype(BF16)
    logits = (jnp.dot(hn_hi, wr2_ref[:, 0:LANES], preferred_element_type=F32)
              + jnp.dot(hn_hi, wr2_ref[:, LANES:2 * LANES], preferred_element_type=F32)
              + jnp.dot(hn_lo, wr2_ref[:, 0:LANES], preferred_element_type=F32)) + br_ref[...]
    lt = logits.T
    row = lambda n: lt[n:n + 1, :]
    gmax, gidx = _first_max4(row(0), row(1), row(2), row(3))
    zg = (jnp.exp(row(0) - gmax) + jnp.exp(row(1) - gmax)
          + jnp.exp(row(2) - gmax) + jnp.exp(row(3) - gmax))
    g_gate = 1.0 / zg
    base = N_GROUPS
    sel = [jnp.where(gidx == 0, row(base + j),
                     jnp.where(gidx == 1, row(base + 4 + j),
                               jnp.where(gidx == 2, row(base + 8 + j), row(base + 12 + j))))
           for j in range(EXPERTS_PER_GROUP)]
    m1, i1 = _first_max4(*sel)
    ze = sum(jnp.exp(sj - m1) for sj in sel)
    rest = [jnp.where(i1 == j, -jnp.inf, sel[j]) for j in range(EXPERTS_PER_GROUP)]
    m2, i2 = _first_max4(*rest)
    p1 = 1.0 / ze
    p2 = jnp.exp(m2 - m1) / ze
    gate1 = g_gate * (p1 / (p1 + p2))
    gate2 = g_gate * (p2 / (p1 + p2))
    e1 = gidx * EXPERTS_PER_GROUP + i1
    e2 = gidx * EXPERTS_PER_GROUP + i2

    eio = lax.broadcasted_iota(I32, (N_EXPERTS, tm), 0)
    oh1 = eio == e1
    oh2 = eio == e2
    both = (oh1 | oh2).astype(F32)
    cnt = jnp.sum(both, axis=1, keepdims=True)
    grp = jnp.floor((cnt + (ROW_CHUNK - 1)) * (1.0 / ROW_CHUNK)) * ROW_CHUNK
    ti = lax.broadcasted_iota(I32, (tm, tm), 0)
    tj = lax.broadcasted_iota(I32, (tm, tm), 1)
    before = (ti < tj).astype(BF16)
    rank = jnp.dot(both.astype(BF16), before, preferred_element_type=F32)
    start1 = jnp.sum(jnp.where(eio < e1, grp, 0.0), axis=0, keepdims=True)
    start2 = jnp.sum(jnp.where(eio < e2, grp, 0.0), axis=0, keepdims=True)
    slot1 = start1 + jnp.sum(jnp.where(oh1, rank, 0.0), axis=0, keepdims=True)
    slot2 = start2 + jnp.sum(jnp.where(oh2, rank, 0.0), axis=0, keepdims=True)

    loc = xl_ref.shape[0]
    sio = lax.broadcasted_iota(I32, (loc, tm), 0)
    perm = ((sio == slot1.astype(I32)) | (sio == slot2.astype(I32))).astype(BF16)
    xs = jnp.dot(perm, hn_hi, preferred_element_type=F32)
    xl_ref[...] = xs.astype(BF16)

    rio = lax.broadcasted_iota(I32, (LANES, tm), 0)
    aux_t = jnp.where(rio == 0, slot1, jnp.where(rio == 1, slot2,
                      jnp.where(rio == 2, gate1, jnp.where(rio == 3, gate2, 0.0))))
    aux_ref[...] = aux_t.T
    gt_ref[0] = jnp.broadcast_to(grp, (N_EXPERTS, LANES))


def _out_route(y_lru, y_att, x2, wo_bf, g2, wr2, br):
    t = x2.shape[0]
    tm = min(TM_TOK, t)
    nt = t // tm
    loc = 2 * tm + N_EXPERTS * ROW_CHUNK
    const = lambda shape: pl.BlockSpec(shape, lambda i: (0,) * len(shape))
    return pl.pallas_call(
        _route_body,
        grid=(nt,),
        in_specs=[
            pl.BlockSpec((tm, LRU_WIDTH), lambda i: (i, 0)),
            pl.BlockSpec((tm, ATT_WIDTH), lambda i: (i, 0)),
            pl.BlockSpec((tm, D_MODEL), lambda i: (i, 0)),
            const((D_MODEL, D_MODEL)),
            const((1, D_MODEL)),
            const((D_MODEL, 2 * LANES)),
            const((1, LANES)),
        ],
        out_specs=[
            pl.BlockSpec((tm, D_MODEL), lambda i: (i, 0)),
            pl.BlockSpec((loc, D_MODEL), lambda i: (i, 0)),
            pl.BlockSpec((tm, LANES), lambda i: (i, 0)),
            pl.BlockSpec((1, N_EXPERTS, LANES), lambda i: (i, 0, 0)),
        ],
        out_shape=[
            jax.ShapeDtypeStruct((t, D_MODEL), F32),
            jax.ShapeDtypeStruct((nt * loc, D_MODEL), BF16),
            jax.ShapeDtypeStruct((t, LANES), F32),
            jax.ShapeDtypeStruct((nt, N_EXPERTS, LANES), F32),
        ],
        compiler_params=_cparams(1),
        name="out_route",
    )(y_lru, y_att, x2, wo_bf, g2, wr2, br)


CHUNKS_PER_TILE = TM_EXP // ROW_CHUNK


DUMP_CHUNKS = 2 * CHUNKS_PER_TILE


def _expert_body(tile_e_ref, nused_ref, src_ref, dst_ref, tail_row_ref, tail_n_ref,
                 xl_ref, wg_ref, wu_ref, wd_ref, yl_ref,
                 xbuf, ybuf, zbuf, gsem, ssem, zsem):
    j = pl.program_id(0)
    nused = nused_ref[0]
    nt = tail_n_ref.shape[0]
    slot = lax.rem(j, 2)
    dump_row = yl_ref.shape[0] - DUMP_CHUNKS * ROW_CHUNK

    def rows(c):
        if isinstance(c, int):
            return pl.ds(c * ROW_CHUNK, ROW_CHUNK)
        return pl.ds(pl.multiple_of(c * ROW_CHUNK, ROW_CHUNK), ROW_CHUNK)

    def start_gather(t, sl):
        for k in range(CHUNKS_PER_TILE):
            pltpu.make_async_copy(xl_ref.at[rows(src_ref[t * CHUNKS_PER_TILE + k])],
                                  xbuf.at[sl, rows(k)], gsem.at[sl]).start()

    def start_scatter(t, sl):
        for k in range(CHUNKS_PER_TILE):
            pltpu.make_async_copy(ybuf.at[sl, rows(k)],
                                  yl_ref.at[rows(dst_ref[t * CHUNKS_PER_TILE + k])], ssem.at[sl]).start()

    def wait_gather(sl):
        pltpu.make_async_copy(xl_ref.at[pl.ds(0, TM_EXP)], xbuf.at[sl], gsem.at[sl]).wait()

    def wait_scatter(sl):
        pltpu.make_async_copy(ybuf.at[sl], yl_ref.at[pl.ds(0, TM_EXP)], ssem.at[sl]).wait()

    def for_count(n, fn):
        def body(k, c):
            fn(k)
            return c
        lax.fori_loop(0, n, body, 0)

    def zero_fill(i, k):
        return pltpu.make_async_copy(
            zbuf, yl_ref.at[pl.ds(pl.multiple_of(tail_row_ref[i] + k * ROW_CHUNK, ROW_CHUNK), ROW_CHUNK)],
            zsem)

    @pl.when(j == 0)
    def _():
        zbuf[...] = jnp.zeros(zbuf.shape, BF16)
        ybuf[...] = jnp.zeros(ybuf.shape, BF16)
        for_count(nt, lambda i: for_count(tail_n_ref[i], lambda k: zero_fill(i, k).start()))
        start_gather(0, 0)
        for sl in range(2):
            for k in range(CHUNKS_PER_TILE):
                pltpu.make_async_copy(
                    ybuf.at[sl, rows(k)],
                    yl_ref.at[pl.ds(dump_row + (sl * CHUNKS_PER_TILE + k) * ROW_CHUNK, ROW_CHUNK)],
                    ssem.at[sl]).start()

    @pl.when(j < nused)
    def _():
        wait_gather(slot)
        wait_scatter(slot)
        start_gather(jnp.minimum(j + 1, nused - 1), 1 - slot)
        xb = xbuf[slot]
        y = None
        for c in range(2):
            cols = slice(c * (D_EXPERT // 2), (c + 1) * (D_EXPERT // 2))
            gate = jnp.dot(xb, wg_ref[0, :, cols], preferred_element_type=F32)
            up = jnp.dot(xb, wu_ref[0, :, cols], preferred_element_type=F32)
            hid = (jax.nn.silu(gate) * up).astype(BF16)
            part = jnp.dot(hid, wd_ref[0, cols, :], preferred_element_type=F32)
            y = part if y is None else y + part
        ybuf[slot] = y.astype(BF16)
        start_scatter(j, slot)

    @pl.when(j == nused - 1)
    def _():
        wait_gather(1 - slot)
        wait_scatter(1 - slot)
        wait_scatter(slot)
        for_count(nt, lambda i: for_count(tail_n_ref[i], lambda k: zero_fill(i, 0).wait()))


def _experts(xl, tables, wg_bf, wu_bf, wd_bf, n_tiles):
    tile_e, nused, src_map, dst_map, tail_row, tail_n = tables
    wmap = lambda j, te, *_: (te[j], 0, 0)
    grid_spec = pltpu.PrefetchScalarGridSpec(
        num_scalar_prefetch=6,
        grid=(n_tiles,),
        in_specs=[
            pl.BlockSpec(memory_space=pl.ANY),
            pl.BlockSpec((1, D_MODEL, D_EXPERT), wmap),
            pl.BlockSpec((1, D_MODEL, D_EXPERT), wmap),
            pl.BlockSpec((1, D_EXPERT, D_MODEL), wmap),
        ],
        out_specs=pl.BlockSpec(memory_space=pl.ANY),
        scratch_shapes=[
            pltpu.VMEM((2, TM_EXP, D_MODEL), BF16),
            pltpu.VMEM((2, TM_EXP, D_MODEL), BF16),
            pltpu.VMEM((ROW_CHUNK, D_MODEL), BF16),
            pltpu.SemaphoreType.DMA((2,)),
            pltpu.SemaphoreType.DMA((2,)),
            pltpu.SemaphoreType.DMA,
        ],
    )
    return pl.pallas_call(
        _expert_body,
        grid_spec=grid_spec,
        out_shape=jax.ShapeDtypeStruct((xl.shape[0] + DUMP_CHUNKS * ROW_CHUNK, D_MODEL), BF16),
        compiler_params=_cparams(1),
        name="experts",
    )(tile_e, nused, src_map, dst_map, tail_row, tail_n, xl, wg_bf, wu_bf, wd_bf)


def _combine_body(h1_ref, aux_ref, yl_ref, o_ref):
    tm = h1_ref.shape[0]
    loc = yl_ref.shape[0]
    aux = aux_ref[...]
    slot1 = aux[:, 0:1].astype(I32)
    slot2 = aux[:, 1:2].astype(I32)
    sio = lax.broadcasted_iota(I32, (tm, loc), 1)
    gm = jnp.where(sio == slot1, aux[:, 2:3], 0.0) + jnp.where(sio == slot2, aux[:, 3:4], 0.0)
    gm_hi = gm.astype(BF16)
    gm_lo = (gm - gm_hi.astype(F32)).astype(BF16)
    yb = yl_ref[...]
    o_ref[...] = (h1_ref[...] + jnp.dot(gm_hi, yb, preferred_element_type=F32)
                  + jnp.dot(gm_lo, yb, preferred_element_type=F32))


def _combine(h1, aux, yl, nt, tm, loc):
    return pl.pallas_call(
        _combine_body,
        grid=(nt,),
        in_specs=[
            pl.BlockSpec((tm, D_MODEL), lambda i: (i, 0)),
            pl.BlockSpec((tm, LANES), lambda i: (i, 0)),
            pl.BlockSpec((loc, D_MODEL), lambda i: (i, 0)),
        ],
        out_specs=pl.BlockSpec((tm, D_MODEL), lambda i: (i, 0)),
        out_shape=jax.ShapeDtypeStruct(h1.shape, F32),
        compiler_params=_cparams(1),
        name="combine",
    )(h1, aux, yl)


def _excl_cumsum(a, axis):
    return jnp.cumsum(a, axis=axis) - a


def _expert_tables(grp, n_tiles, loc):
    nt = grp.shape[0]
    gch = grp // ROW_CHUNK
    loc_start = _excl_cumsum(gch, 1)
    used = jnp.sum(gch, axis=1)
    col = jnp.sum(gch, axis=0)
    seg = ((col + CHUNKS_PER_TILE - 1) // CHUNKS_PER_TILE) * CHUNKS_PER_TILE
    seg_end = jnp.cumsum(seg)
    off = seg_end - seg
    tile_first = jnp.arange(n_tiles, dtype=I32) * CHUNKS_PER_TILE
    tile_e = jnp.minimum(jnp.sum(seg_end[None, :] <= tile_first[:, None], axis=1), N_EXPERTS - 1)
    nused = seg_end[-1:] // CHUNKS_PER_TILE
    onehot = tile_e[:, None] == jnp.arange(N_EXPERTS, dtype=I32)[None, :]
    pick = lambda tab: jnp.sum(jnp.where(onehot[:, :, None], tab.T[None], 0), axis=1)
    pick1 = lambda vec: jnp.sum(jnp.where(onehot, vec[None, :], 0), axis=1)
    first = tile_first - pick1(off)
    nvalid = jnp.clip(pick1(col) - first, 0, CHUNKS_PER_TILE)
    cum = jnp.cumsum(gch, axis=0)
    delta = jnp.arange(nt, dtype=I32)[:, None] * (loc // ROW_CHUNK) + loc_start - (cum - gch)
    step = delta - jnp.concatenate([jnp.zeros((1, N_EXPERTS), I32), delta[:-1]], axis=0)
    cum_prev = jnp.concatenate([jnp.full((1, N_EXPERTS), -1, I32), cum[:-1]], axis=0)
    cc = first[:, None] + jnp.arange(CHUNKS_PER_TILE, dtype=I32)[None, :]
    passed = pick(cum_prev)[:, None, :] <= cc[:, :, None]
    chunk_map = cc + jnp.sum(jnp.where(passed, pick(step)[:, None, :], 0), axis=2)
    k_in_tile = jnp.arange(CHUNKS_PER_TILE, dtype=I32)[None, :]
    valid = k_in_tile < nvalid[:, None]
    zero_chunk = loc // ROW_CHUNK - 1
    dump = nt * (loc // ROW_CHUNK) + (jnp.arange(n_tiles, dtype=I32)[:, None] % 2) * CHUNKS_PER_TILE + k_in_tile
    flat = lambda a: a.reshape(n_tiles * CHUNKS_PER_TILE)
    src_map = flat(jnp.where(valid, chunk_map, zero_chunk))
    dst_map = flat(jnp.where(valid, chunk_map, dump))

    tail_row = jnp.arange(nt, dtype=I32) * loc + used * ROW_CHUNK
    tail_n = loc // ROW_CHUNK - used
    as_i32 = lambda a: a.astype(I32)
    return tuple(map(as_i32, (tile_e, nused, src_map, dst_map, tail_row, tail_n)))


def kernel(x, norm1_g, w_in, conv_w, conv_b, w_gate_a, b_gate_a, w_gate_x, b_gate_x, lru_lambda,
           lru_out_g, q_norm_g, k_norm_g, lambda_q1, lambda_k1, lambda_q2, lambda_k2, sub_norm_g,
           w_out, norm2_g, w_router_group, b_router_group, w_router_expert, b_router_expert,
           w_expert_gate, w_expert_up, w_expert_down):
    b, s, d = x.shape
    assert d == D_MODEL and norm1_g.shape[0] == 1
    t = b * s
    l = 0
    x2 = x.reshape(t, d)

    w_in_bf = w_in[l].astype(BF16)
    scale = HEAD_DIM ** -0.5 * LOG2E
    qkg = jnp.stack([jnp.tile(q_norm_g[l], 2) * scale, jnp.tile(k_norm_g[l], 2)]).astype(F32)
    eye = jnp.eye(LRU_BLOCKS, dtype=F32)
    blockdiag = lambda w: jnp.einsum("ncd,nm->ncmd", w, eye).reshape(LRU_WIDTH, LRU_WIDTH)
    wgate = jnp.concatenate([blockdiag(w_gate_a[l]), blockdiag(w_gate_x[l])], axis=1).astype(BF16)
    bgate = jnp.concatenate([b_gate_a[l], b_gate_x[l]])[None, :]
    lam_params = jnp.stack([lambda_q1[l], lambda_k1[l], lambda_q2[l], lambda_k2[l]])
    wr = jnp.concatenate(
        [w_router_group[l], jnp.transpose(w_router_expert[l], (1, 0, 2)).reshape(d, N_EXPERTS)], axis=1)
    wr = jnp.pad(wr, ((0, 0), (0, LANES - wr.shape[1])))
    wr_hi = wr.astype(BF16)
    wr2 = jnp.concatenate([wr_hi, (wr - wr_hi.astype(F32)).astype(BF16)], axis=1)
    br = jnp.pad(jnp.concatenate([b_router_group[l], b_router_expert[l].reshape(-1)]),
                 (0, LANES - N_GROUPS - N_EXPERTS))[None, :]

    proj = _in_proj(x2, norm1_g[l][None, :], w_in_bf, qkg)
    y_lru = _lru(proj, b, s, conv_w[l], conv_b[l][None, :], wgate, bgate,
                 lru_lambda[l][None, :], lru_out_g[l][None, :])
    y_att = _attention(proj, b, s, lam_params, sub_norm_g[l][None, :])
    h1, xl, aux, gt = _out_route(y_lru, y_att, x2, w_out[l].astype(BF16), norm2_g[l][None, :],
                                 wr2, br)

    tm = min(TM_TOK, t)
    nt = t // tm
    loc = 2 * tm + N_EXPERTS * ROW_CHUNK
    max_rows = 2 * t + nt * N_EXPERTS * (ROW_CHUNK - 1) + N_EXPERTS * (TM_EXP - ROW_CHUNK)
    n_tiles = -(-max_rows // TM_EXP)
    grp = gt[:, :, 0].astype(I32)
    tables = _expert_tables(grp, n_tiles, loc)
    yl = _experts(xl, tables, w_expert_gate[l].astype(BF16), w_expert_up[l].astype(BF16),
                  w_expert_down[l].astype(BF16), n_tiles)
    out = _combine(h1, aux, yl, nt, tm, loc)
    return out.reshape(b, s, d)
```

```python
import functools
import math

import numpy as np
import jax
import jax.numpy as jnp
from jax import lax
from jax.experimental import pallas as pl
from jax.experimental.pallas import tpu as pltpu

F32 = jnp.float32
BF16 = jnp.bfloat16
I32 = jnp.int32

D_MODEL = 1024
LRU_WIDTH = 512
LRU_BLOCKS = 8
LRU_BLOCK_W = LRU_WIDTH // LRU_BLOCKS
CONV_W = 4
LRU_C = 8.0
ATT_WIDTH = 512
N_HEADS = 4
HEAD_DIM = 64
V_DIM = 128
IN_COLS = 2 * LRU_WIDTH + 3 * ATT_WIDTH
N_GROUPS = 4
EXPERTS_PER_GROUP = 4
N_EXPERTS = N_GROUPS * EXPERTS_PER_GROUP
D_EXPERT = D_MODEL // 2
CHUNK = 64
EPS = 1e-6
NEG_BIG = -1e30
LAMBDA_INIT = 0.8 - 0.6 * math.exp(-0.3 * 0)
LOG2E = math.log2(math.e)

LANES = 128
SUBLANES = 8

TM_PROJ = 512
TS_LRU = 256
TQ = 512
TM_TOK = 256
TM_EXP = 512
ROW_CHUNK = 2 * SUBLANES
VMEM_LIMIT = 56 * 1024 * 1024


def _cparams(n_axes):
    return pltpu.CompilerParams(
        dimension_semantics=("arbitrary",) * n_axes, vmem_limit_bytes=VMEM_LIMIT)


def _inproj_body(x_ref, g1_ref, w_ref, qkg_ref, o_ref, w_bf):
    @pl.when(pl.program_id(0) == 0)
    def _():
        w_bf[...] = w_ref[...].astype(BF16)

    x = x_ref[...]
    ms = jnp.mean(x * x, axis=-1, keepdims=True)
    hn = ((x * lax.rsqrt(ms + EPS)) * g1_ref[...]).astype(BF16)
    tm = x.shape[0]
    lo_half = lax.broadcasted_iota(I32, (tm, LANES), 1) < HEAD_DIM
    width = 512
    for c in range(IN_COLS // width):
        c0 = c * width
        acc = jnp.dot(hn, w_bf[:, c0:c0 + width], preferred_element_type=F32)
        if c in (2, 3):
            gain = qkg_ref[c - 2:c - 1, :]
            for b in range(width // LANES):
                blk = acc[:, b * LANES:(b + 1) * LANES]
                sq = blk * blk
                s_lo = jnp.sum(jnp.where(lo_half, sq, 0.0), axis=-1, keepdims=True)
                s_hi = jnp.sum(jnp.where(lo_half, 0.0, sq), axis=-1, keepdims=True)
                inv = jnp.where(lo_half,
                                lax.rsqrt(s_lo * (1.0 / HEAD_DIM) + EPS),
                                lax.rsqrt(s_hi * (1.0 / HEAD_DIM) + EPS))
                o_ref[:, c0 + b * LANES:c0 + (b + 1) * LANES] = ((blk * inv) * gain).astype(BF16)
        else:
            o_ref[:, c0:c0 + width] = acc.astype(BF16)


def _in_proj(x2, g1, w_in, qkg):
    t = x2.shape[0]
    tm = min(TM_PROJ, t)
    return pl.pallas_call(
        _inproj_body,
        grid=(t // tm,),
        in_specs=[
            pl.BlockSpec((tm, D_MODEL), lambda i: (i, 0)),
            pl.BlockSpec((1, D_MODEL), lambda i: (0, 0)),
            pl.BlockSpec((D_MODEL, IN_COLS), lambda i: (0, 0)),
            pl.BlockSpec((2, LANES), lambda i: (0, 0)),
        ],
        out_specs=pl.BlockSpec((tm, IN_COLS), lambda i: (i, 0)),
        out_shape=jax.ShapeDtypeStruct((t, IN_COLS), BF16),
        scratch_shapes=[pltpu.VMEM((D_MODEL, IN_COLS), BF16)],
        compiler_params=_cparams(1),
        name="in_proj",
    )(x2, g1, w_in, qkg)


def _lru_body(p_ref, cw_ref, cb_ref, wg_ref, bg_ref, lam_ref, og_ref, o_ref, xbuf, hbuf, hc):
    s = pl.program_id(1)
    ts = p_ref.shape[0]

    @pl.when(s == 0)
    def _():
        xbuf[0:SUBLANES, :] = jnp.zeros((SUBLANES, LRU_WIDTH), F32)
        hc[...] = jnp.zeros((1, LRU_WIDTH), F32)

    x = p_ref[:, 0:LRU_WIDTH].astype(F32)
    gl = p_ref[:, LRU_WIDTH:2 * LRU_WIDTH].astype(F32)
    xbuf[SUBLANES:SUBLANES + ts, :] = x
    xc = cb_ref[...]
    for j in range(CONV_W):
        r0 = SUBLANES - (CONV_W - 1) + j
        xc = xc + xbuf[r0:r0 + ts, :] * cw_ref[j:j + 1, :]
    xbuf[0:SUBLANES, :] = x[ts - SUBLANES:ts, :]

    z = jnp.dot(xc.astype(BF16), wg_ref[...], preferred_element_type=F32) + bg_ref[...]
    r = 0.5 * jnp.tanh(0.5 * z[:, 0:LRU_WIDTH]) + 0.5
    gi = 0.5 * jnp.tanh(0.5 * z[:, LRU_WIDTH:2 * LRU_WIDTH]) + 0.5
    nl = -lam_ref[...]
    softplus = jnp.maximum(nl, 0.0) + jnp.log1p(jnp.exp(-jnp.abs(nl)))
    log_a = (-LRU_C) * r * softplus
    a = jnp.exp(log_a)
    v = -jnp.tanh(log_a) * (a * a + 1.0)
    u = jnp.where(v > 0.0, v * lax.rsqrt(v), 0.0) * (gi * xc)

    row = lax.broadcasted_iota(I32, (ts, LRU_WIDTH), 0) & (SUBLANES - 1)
    ca, cb = a, u
    for d in (1, 2, 4):
        a_sh = pltpu.roll(ca, d, axis=0)
        b_sh = pltpu.roll(cb, d, axis=0)
        take = row >= d
        cb = jnp.where(take, ca * b_sh + cb, cb)
        ca = jnp.where(take, ca * a_sh, ca)
    h = hc[...]
    for blk in range(ts // SUBLANES):
        r0 = blk * SUBLANES
        hb = ca[r0:r0 + SUBLANES, :] * h + cb[r0:r0 + SUBLANES, :]
        hbuf[r0:r0 + SUBLANES, :] = hb
        h = hb[SUBLANES - 1:SUBLANES, :]
    hc[...] = h

    y = hbuf[...] * jax.nn.gelu(gl)
    ms = jnp.mean(y * y, axis=-1, keepdims=True)
    o_ref[...] = ((y * lax.rsqrt(ms + EPS)) * og_ref[...]).astype(o_ref.dtype)


def _lru(proj, b, s, conv_w, conv_b, wgate, bgate, lam, out_g):
    ts = min(TS_LRU, s)
    ns = s // ts
    vec = lambda n: pl.BlockSpec((1, n), lambda bi, si: (0, 0))
    return pl.pallas_call(
        _lru_body,
        grid=(b, ns),
        in_specs=[
            pl.BlockSpec((ts, 2 * LRU_WIDTH), lambda bi, si: (bi * ns + si, 0)),
            pl.BlockSpec((CONV_W, LRU_WIDTH), lambda bi, si: (0, 0)),
            vec(LRU_WIDTH),
            pl.BlockSpec((LRU_WIDTH, 2 * LRU_WIDTH), lambda bi, si: (0, 0)),
            vec(2 * LRU_WIDTH),
            vec(LRU_WIDTH),
            vec(LRU_WIDTH),
        ],
        out_specs=pl.BlockSpec((ts, LRU_WIDTH), lambda bi, si: (bi * ns + si, 0)),
        out_shape=jax.ShapeDtypeStruct((b * s, LRU_WIDTH), BF16),
        scratch_shapes=[
            pltpu.VMEM((ts + SUBLANES, LRU_WIDTH), F32),
            pltpu.VMEM((ts, LRU_WIDTH), F32),
            pltpu.VMEM((1, LRU_WIDTH), F32),
        ],
        compiler_params=_cparams(2),
        name="rg_lru",
    )(proj, conv_w, conv_b, wgate, bgate, lam, out_g)


SOFTMAX_ROWS = 32


def _attn_body(slope_ref, q_ref, k_ref, v_ref, bias_ref, lamp_ref, sg_ref, o_ref,
               qs_buf, s0, s1, p0, p1, a0, a1, m_buf, acc0, acc1, *, tq, nq):
    h = pl.program_id(1)
    slope = slope_ref[h]
    s_bufs, p_bufs, a_bufs, accs = (s0, s1), (p0, p1), (a0, a1), (acc0, acc1)
    lo_half = lax.broadcasted_iota(I32, (tq, LANES), 1) < HEAD_DIM
    for i in range(nq):
        q = q_ref[i * tq:(i + 1) * tq, :]
        zero = jnp.zeros_like(q)
        qs_buf[i, 0:tq, :] = jnp.where(lo_half, q, zero)
        qs_buf[i, tq:2 * tq, :] = jnp.where(lo_half, zero, q)
    ones = jnp.ones((tq, V_DIM), BF16)
    lp = lamp_ref[...]
    lam = (jnp.exp(jnp.sum(lp[0:1, :] * lp[1:2, :], axis=-1, keepdims=True))
           - jnp.exp(jnp.sum(lp[2:3, :] * lp[3:4, :], axis=-1, keepdims=True))
           + LAMBDA_INIT)
    pairs = [(i, j) for i in range(nq) for j in range(i + 1)]

    def scores(t):
        i, j = pairs[t]
        s_bufs[t % 2][...] = lax.dot_general(
            qs_buf[i], k_ref[j * tq:(j + 1) * tq, :], (((1,), (1,)), ((), ())),
            preferred_element_type=F32)

    def softmax(t):
        i, j = pairs[t]
        which = 1 if j == i else 0
        shift = slope * float(-(i - j) * tq)
        s_buf, p_buf, a_buf = s_bufs[t % 2], p_bufs[t % 2], a_bufs[t % 2]
        for r in range(2 * tq // SOFTMAX_ROWS):
            rows = slice(r * SOFTMAX_ROWS, (r + 1) * SOFTMAX_ROWS)
            sb = s_buf[rows, :] + bias_ref[0, which, rows, :]
            m_new = jnp.broadcast_to(jnp.max(sb, axis=-1, keepdims=True), (SOFTMAX_ROWS, LANES)) + shift
            if j > 0:
                m_old = m_buf[rows, :]
                m_new = jnp.maximum(m_old, m_new)
                a_buf[rows, :] = jnp.exp2(m_old - m_new)
            m_sub = m_new - shift
            p_buf[rows, :] = jnp.exp2(sb - jnp.concatenate([m_sub] * (tq // LANES), axis=1)).astype(BF16)
            m_buf[rows, :] = m_new

    def accumulate(t):
        i, j = pairs[t]
        acc = accs[i % 2]
        v_aug = jnp.concatenate([v_ref[j * tq:(j + 1) * tq, :], ones], axis=1)
        pv = jnp.dot(p_bufs[t % 2][...], v_aug, preferred_element_type=F32)
        if j == 0:
            acc[...] = pv
        else:
            alpha = a_bufs[t % 2][...]
            acc[...] = jnp.concatenate([alpha] * (2 * V_DIM // LANES), axis=1) * acc[...] + pv
        if j == i:
            o = (acc[0:tq, 0:V_DIM] / acc[0:tq, V_DIM:V_DIM + 1]
                 - lam * (acc[tq:2 * tq, 0:V_DIM] / acc[tq:2 * tq, V_DIM:V_DIM + 1]))
            ms = jnp.mean(o * o, axis=-1, keepdims=True)
            o = ((o * lax.rsqrt(ms + EPS)) * sg_ref[...]) * (1.0 - LAMBDA_INIT)
            o_ref[i * tq:(i + 1) * tq, :] = o.astype(o_ref.dtype)

    scores(0)
    for t in range(len(pairs)):
        if t + 1 < len(pairs):
            scores(t + 1)
        softmax(t)
        if t >= 1:
            accumulate(t - 1)
    accumulate(len(pairs) - 1)


def _alibi_tables(tq):
    slopes = np.exp2(-8.0 * np.arange(1, N_HEADS + 1, dtype=np.float64) / N_HEADS)
    qi = np.arange(tq)[:, None]
    kj = np.arange(tq)[None, :]
    off = -(slopes[:, None, None] * (qi - kj)[None])
    allowed = (kj // CHUNK) <= (qi // CHUNK)
    diag = np.where(allowed[None], -(slopes[:, None, None] * np.abs(qi - kj)[None]), NEG_BIG)
    tab = np.stack([off, diag], axis=1)
    tab = np.concatenate([tab, tab], axis=2)
    return jnp.asarray(tab * LOG2E, F32), jnp.asarray(slopes * LOG2E, F32)


def _attention(proj, b, s, lam_params, sub_g):
    tq = min(TQ, s)
    nq = s // tq
    bias, slopes = _alibi_tables(tq)
    qcol = 2 * LRU_WIDTH // LANES
    kcol = qcol + ATT_WIDTH // LANES
    vcol = kcol + ATT_WIDTH // LANES
    score_buf = pltpu.VMEM((2 * tq, tq), F32)
    prob_buf = pltpu.VMEM((2 * tq, tq), BF16)
    col_buf = pltpu.VMEM((2 * tq, LANES), F32)
    acc_buf = pltpu.VMEM((2 * tq, 2 * V_DIM), F32)
    grid_spec = pltpu.PrefetchScalarGridSpec(
        num_scalar_prefetch=1,
        grid=(b, N_HEADS),
        in_specs=[
            pl.BlockSpec((s, LANES), lambda bi, h, sl: (bi, qcol + h)),
            pl.BlockSpec((s, LANES), lambda bi, h, sl: (bi, kcol + h)),
            pl.BlockSpec((s, LANES), lambda bi, h, sl: (bi, vcol + h)),
            pl.BlockSpec((1, 2, 2 * tq, tq), lambda bi, h, sl: (h, 0, 0, 0)),
            pl.BlockSpec((4, HEAD_DIM), lambda bi, h, sl: (0, 0)),
            pl.BlockSpec((1, V_DIM), lambda bi, h, sl: (0, 0)),
        ],
        out_specs=pl.BlockSpec((s, V_DIM), lambda bi, h, sl: (bi, h)),
        scratch_shapes=[
            pltpu.VMEM((nq, 2 * tq, LANES), BF16),
            score_buf, score_buf, prob_buf, prob_buf, col_buf, col_buf, col_buf, acc_buf, acc_buf,
        ],
    )
    return pl.pallas_call(
        functools.partial(_attn_body, tq=tq, nq=nq),
        grid_spec=grid_spec,
        out_shape=jax.ShapeDtypeStruct((b * s, ATT_WIDTH), BF16),
        compiler_params=_cparams(2),
        name="diff_attn",
    )(slopes, proj, proj, proj, bias, lam_params, sub_g)


def _first_max4(v0, v1, v2, v3):
    m = jnp.maximum(jnp.maximum(v0, v1), jnp.maximum(v2, v3))
    idx = jnp.where(v0 == m, 0, jnp.where(v1 == m, 1, jnp.where(v2 == m, 2, 3))).astype(I32)
    return m, idx


def _route_body(yl_ref, ya_ref, x_ref, wo_ref, g2_ref, wr2_ref, br_ref,
                h1_ref, xl_ref, aux_ref, gt_ref, wo_bf):
    @pl.when(pl.program_id(0) == 0)
    def _():
        wo_bf[...] = wo_ref[...].astype(BF16)

    tm = x_ref.shape[0]
    mix = (jnp.dot(yl_ref[...], wo_bf[0:LRU_WIDTH, :], preferred_element_type=F32)
           + jnp.dot(ya_ref[...], wo_bf[LRU_WIDTH:, :], preferred_element_type=F32))
    h1 = x_ref[...] + mix
    h1_ref[...] = h1
    ms = jnp.mean(h1 * h1, axis=-1, keepdims=True)
    hn = (h1 * lax.rsqrt(ms + EPS)) * g2_ref[...]
    hn_hi = hn.astype(BF16)
    hn_lo = (hn - hn_hi.astype(F32)).astype(BF16)
    logits = (jnp.dot(hn_hi, wr2_ref[:, 0:LANES], preferred_element_type=F32)
              + jnp.dot(hn_hi, wr2_ref[:, LANES:2 * LANES], preferred_element_type=F32)
              + jnp.dot(hn_lo, wr2_ref[:, 0:LANES], preferred_element_type=F32)) + br_ref[...]
    lt = logits.T
    row = lambda n: lt[n:n + 1, :]
    gmax, gidx = _first_max4(row(0), row(1), row(2), row(3))
    zg = (jnp.exp(row(0) - gmax) + jnp.exp(row(1) - gmax)
          + jnp.exp(row(2) - gmax) + jnp.exp(row(3) - gmax))
    g_gate = 1.0 / zg
    base = N_GROUPS
    sel = [jnp.where(gidx == 0, row(base + j),
                     jnp.where(gidx == 1, row(base + 4 + j),
                               jnp.where(gidx == 2, row(base + 8 + j), row(base + 12 + j))))
           for j in range(EXPERTS_PER_GROUP)]
    m1, i1 = _first_max4(*sel)
    ze = sum(jnp.exp(sj - m1) for sj in sel)
    rest = [jnp.where(i1 == j, -jnp.inf, sel[j]) for j in range(EXPERTS_PER_GROUP)]
    m2, i2 = _first_max4(*rest)
    p1 = 1.0 / ze
    p2 = jnp.exp(m2 - m1) / ze
    gate1 = g_gate * (p1 / (p1 + p2))
    gate2 = g_gate * (p2 / (p1 + p2))
    e1 = gidx * EXPERTS_PER_GROUP + i1
    e2 = gidx * EXPERTS_PER_GROUP + i2

    eio = lax.broadcasted_iota(I32, (N_EXPERTS, tm), 0)
    oh1 = eio == e1
    oh2 = eio == e2
    both = (oh1 | oh2).astype(F32)
    cnt = jnp.sum(both, axis=1, keepdims=True)
    grp = jnp.floor((cnt + (ROW_CHUNK - 1)) * (1.0 / ROW_CHUNK)) * ROW_CHUNK
    ti = lax.broadcasted_iota(I32, (tm, tm), 0)
    tj = lax.broadcasted_iota(I32, (tm, tm), 1)
    before = (ti < tj).astype(BF16)
    rank = jnp.dot(both.astype(BF16), before, preferred_element_type=F32)
    start1 = jnp.sum(jnp.where(eio < e1, grp, 0.0), axis=0, keepdims=True)
    start2 = jnp.sum(jnp.where(eio < e2, grp, 0.0), axis=0, keepdims=True)
    slot1 = start1 + jnp.sum(jnp.where(oh1, rank, 0.0), axis=0, keepdims=True)
    slot2 = start2 + jnp.sum(jnp.where(oh2, rank, 0.0), axis=0, keepdims=True)

    loc = xl_ref.shape[0]
    sio = lax.broadcasted_iota(I32, (loc, tm), 0)
    perm = ((sio == slot1.astype(I32)) | (sio == slot2.astype(I32))).astype(BF16)
    xs = jnp.dot(perm, hn_hi, preferred_element_type=F32)
    xl_ref[...] = xs.astype(BF16)

    rio = lax.broadcasted_iota(I32, (LANES, tm), 0)
    aux_t = jnp.where(rio == 0, slot1, jnp.where(rio == 1, slot2,
                      jnp.where(rio == 2, gate1, jnp.where(rio == 3, gate2, 0.0))))
    aux_ref[...] = aux_t.T
    gt_ref[0] = jnp.broadcast_to(grp, (N_EXPERTS, LANES))


def _out_route(y_lru, y_att, x2, w_out, g2, wr2, br):
    t = x2.shape[0]
    tm = min(TM_TOK, t)
    nt = t // tm
    loc = 2 * tm + N_EXPERTS * ROW_CHUNK
    const = lambda shape: pl.BlockSpec(shape, lambda i: (0,) * len(shape))
    return pl.pallas_call(
        _route_body,
        grid=(nt,),
        in_specs=[
            pl.BlockSpec((tm, LRU_WIDTH), lambda i: (i, 0)),
            pl.BlockSpec((tm, ATT_WIDTH), lambda i: (i, 0)),
            pl.BlockSpec((tm, D_MODEL), lambda i: (i, 0)),
            const((D_MODEL, D_MODEL)),
            const((1, D_MODEL)),
            const((D_MODEL, 2 * LANES)),
            const((1, LANES)),
        ],
        out_specs=[
            pl.BlockSpec((tm, D_MODEL), lambda i: (i, 0)),
            pl.BlockSpec((loc, D_MODEL), lambda i: (i, 0)),
            pl.BlockSpec((tm, LANES), lambda i: (i, 0)),
            pl.BlockSpec((1, N_EXPERTS, LANES), lambda i: (i, 0, 0)),
        ],
        out_shape=[
            jax.ShapeDtypeStruct((t, D_MODEL), F32),
            jax.ShapeDtypeStruct((nt * loc, D_MODEL), BF16),
            jax.ShapeDtypeStruct((t, LANES), F32),
            jax.ShapeDtypeStruct((nt, N_EXPERTS, LANES), F32),
        ],
        scratch_shapes=[pltpu.VMEM((D_MODEL, D_MODEL), BF16)],
        compiler_params=_cparams(1),
        name="out_route",
    )(y_lru, y_att, x2, w_out, g2, wr2, br)


CHUNKS_PER_TILE = TM_EXP // ROW_CHUNK


DUMP_CHUNKS = 2 * CHUNKS_PER_TILE


def _expert_body(tile_e_ref, nused_ref, src_ref, dst_ref, tail_row_ref, tail_n_ref,
                 xl_ref, wg_ref, wu_ref, wd_ref, yl_ref,
                 xbuf, ybuf, zbuf, wg_bf, wu_bf, wd_bf, gsem, ssem, zsem):
    j = pl.program_id(0)
    nused = nused_ref[0]
    nt = tail_n_ref.shape[0]
    slot = lax.rem(j, 2)
    dump_row = yl_ref.shape[0] - DUMP_CHUNKS * ROW_CHUNK

    def rows(c):
        if isinstance(c, int):
            return pl.ds(c * ROW_CHUNK, ROW_CHUNK)
        return pl.ds(pl.multiple_of(c * ROW_CHUNK, ROW_CHUNK), ROW_CHUNK)

    def start_gather(t, sl):
        for k in range(CHUNKS_PER_TILE):
            pltpu.make_async_copy(xl_ref.at[rows(src_ref[t * CHUNKS_PER_TILE + k])],
                                  xbuf.at[sl, rows(k)], gsem.at[sl]).start()

    def start_scatter(t, sl):
        for k in range(CHUNKS_PER_TILE):
            pltpu.make_async_copy(ybuf.at[sl, rows(k)],
                                  yl_ref.at[rows(dst_ref[t * CHUNKS_PER_TILE + k])], ssem.at[sl]).start()

    def wait_gather(sl):
        pltpu.make_async_copy(xl_ref.at[pl.ds(0, TM_EXP)], xbuf.at[sl], gsem.at[sl]).wait()

    def wait_scatter(sl):
        pltpu.make_async_copy(ybuf.at[sl], yl_ref.at[pl.ds(0, TM_EXP)], ssem.at[sl]).wait()

    def for_count(n, fn):
        def body(k, c):
            fn(k)
            return c
        lax.fori_loop(0, n, body, 0)

    def zero_fill(i, k):
        return pltpu.make_async_copy(
            zbuf, yl_ref.at[pl.ds(pl.multiple_of(tail_row_ref[i] + k * ROW_CHUNK, ROW_CHUNK), ROW_CHUNK)],
            zsem)

    @pl.when(j == 0)
    def _():
        zbuf[...] = jnp.zeros(zbuf.shape, BF16)
        ybuf[...] = jnp.zeros(ybuf.shape, BF16)
        for_count(nt, lambda i: for_count(tail_n_ref[i], lambda k: zero_fill(i, k).start()))
        start_gather(0, 0)
        for sl in range(2):
            for k in range(CHUNKS_PER_TILE):
                pltpu.make_async_copy(
                    ybuf.at[sl, rows(k)],
                    yl_ref.at[pl.ds(dump_row + (sl * CHUNKS_PER_TILE + k) * ROW_CHUNK, ROW_CHUNK)],
                    ssem.at[sl]).start()

    new_expert = (j == 0) | (tile_e_ref[j] != tile_e_ref[jnp.maximum(j - 1, 0)])

    @pl.when(new_expert & (j < nused))
    def _():
        wg_bf[...] = wg_ref[0].astype(BF16)
        wu_bf[...] = wu_ref[0].astype(BF16)
        wd_bf[...] = wd_ref[0].astype(BF16)

    @pl.when(j < nused)
    def _():
        wait_gather(slot)
        wait_scatter(slot)
        start_gather(jnp.minimum(j + 1, nused - 1), 1 - slot)
        xb = xbuf[slot]
        y = None
        for c in range(2):
            cols = slice(c * (D_EXPERT // 2), (c + 1) * (D_EXPERT // 2))
            gate = jnp.dot(xb, wg_bf[:, cols], preferred_element_type=F32)
            up = jnp.dot(xb, wu_bf[:, cols], preferred_element_type=F32)
            hid = (jax.nn.silu(gate) * up).astype(BF16)
            part = jnp.dot(hid, wd_bf[cols, :], preferred_element_type=F32)
            y = part if y is None else y + part
        ybuf[slot] = y.astype(BF16)
        start_scatter(j, slot)

    @pl.when(j == nused - 1)
    def _():
        wait_gather(1 - slot)
        wait_scatter(1 - slot)
        wait_scatter(slot)
        for_count(nt, lambda i: for_count(tail_n_ref[i], lambda k: zero_fill(i, 0).wait()))


def _experts(xl, tables, w_gate, w_up, w_down, n_tiles):
    tile_e, nused, src_map, dst_map, tail_row, tail_n = tables
    wmap = lambda j, te, *_: (te[j], 0, 0)
    grid_spec = pltpu.PrefetchScalarGridSpec(
        num_scalar_prefetch=6,
        grid=(n_tiles,),
        in_specs=[
            pl.BlockSpec(memory_space=pl.ANY),
            pl.BlockSpec((1, D_MODEL, D_EXPERT), wmap),
            pl.BlockSpec((1, D_MODEL, D_EXPERT), wmap),
            pl.BlockSpec((1, D_EXPERT, D_MODEL), wmap),
        ],
        out_specs=pl.BlockSpec(memory_space=pl.ANY),
        scratch_shapes=[
            pltpu.VMEM((2, TM_EXP, D_MODEL), BF16),
            pltpu.VMEM((2, TM_EXP, D_MODEL), BF16),
            pltpu.VMEM((ROW_CHUNK, D_MODEL), BF16),
            pltpu.VMEM((D_MODEL, D_EXPERT), BF16),
            pltpu.VMEM((D_MODEL, D_EXPERT), BF16),
            pltpu.VMEM((D_EXPERT, D_MODEL), BF16),
            pltpu.SemaphoreType.DMA((2,)),
            pltpu.SemaphoreType.DMA((2,)),
            pltpu.SemaphoreType.DMA,
        ],
    )
    return pl.pallas_call(
        _expert_body,
        grid_spec=grid_spec,
        out_shape=jax.ShapeDtypeStruct((xl.shape[0] + DUMP_CHUNKS * ROW_CHUNK, D_MODEL), BF16),
        compiler_params=_cparams(1),
        name="experts",
    )(tile_e, nused, src_map, dst_map, tail_row, tail_n, xl, w_gate, w_up, w_down)


def _combine_body(h1_ref, aux_ref, yl_ref, o_ref):
    tm = h1_ref.shape[0]
    loc = yl_ref.shape[0]
    aux = aux_ref[...]
    slot1 = aux[:, 0:1].astype(I32)
    slot2 = aux[:, 1:2].astype(I32)
    sio = lax.broadcasted_iota(I32, (tm, loc), 1)
    gm = jnp.where(sio == slot1, aux[:, 2:3], 0.0) + jnp.where(sio == slot2, aux[:, 3:4], 0.0)
    gm_hi = gm.astype(BF16)
    gm_lo = (gm - gm_hi.astype(F32)).astype(BF16)
    yb = yl_ref[...]
    o_ref[...] = (h1_ref[...] + jnp.dot(gm_hi, yb, preferred_element_type=F32)
                  + jnp.dot(gm_lo, yb, preferred_element_type=F32))


def _combine(h1, aux, yl, nt, tm, loc):
    return pl.pallas_call(
        _combine_body,
        grid=(nt,),
        in_specs=[
            pl.BlockSpec((tm, D_MODEL), lambda i: (i, 0)),
            pl.BlockSpec((tm, LANES), lambda i: (i, 0)),
            pl.BlockSpec((loc, D_MODEL), lambda i: (i, 0)),
        ],
        out_specs=pl.BlockSpec((tm, D_MODEL), lambda i: (i, 0)),
        out_shape=jax.ShapeDtypeStruct(h1.shape, F32),
        compiler_params=_cparams(1),
        name="combine",
    )(h1, aux, yl)


def _excl_cumsum(a, axis):
    return jnp.cumsum(a, axis=axis) - a


def _expert_tables(grp, n_tiles, loc):
    nt = grp.shape[0]
    gch = grp // ROW_CHUNK
    loc_start = _excl_cumsum(gch, 1)
    used = jnp.sum(gch, axis=1)
    col = jnp.sum(gch, axis=0)
    seg = ((col + CHUNKS_PER_TILE - 1) // CHUNKS_PER_TILE) * CHUNKS_PER_TILE
    seg_end = jnp.cumsum(seg)
    off = seg_end - seg
    tile_first = jnp.arange(n_tiles, dtype=I32) * CHUNKS_PER_TILE
    tile_e = jnp.minimum(jnp.sum(seg_end[None, :] <= tile_first[:, None], axis=1), N_EXPERTS - 1)
    nused = seg_end[-1:] // CHUNKS_PER_TILE
    onehot = tile_e[:, None] == jnp.arange(N_EXPERTS, dtype=I32)[None, :]
    pick = lambda tab: jnp.sum(jnp.where(onehot[:, :, None], tab.T[None], 0), axis=1)
    pick1 = lambda vec: jnp.sum(jnp.where(onehot, vec[None, :], 0), axis=1)
    first = tile_first - pick1(off)
    nvalid = jnp.clip(pick1(col) - first, 0, CHUNKS_PER_TILE)
    cum = jnp.cumsum(gch, axis=0)
    delta = jnp.arange(nt, dtype=I32)[:, None] * (loc // ROW_CHUNK) + loc_start - (cum - gch)
    step = delta - jnp.concatenate([jnp.zeros((1, N_EXPERTS), I32), delta[:-1]], axis=0)
    cum_prev = jnp.concatenate([jnp.full((1, N_EXPERTS), -1, I32), cum[:-1]], axis=0)
    cc = first[:, None] + jnp.arange(CHUNKS_PER_TILE, dtype=I32)[None, :]
    passed = pick(cum_prev)[:, None, :] <= cc[:, :, None]
    chunk_map = cc + jnp.sum(jnp.where(passed, pick(step)[:, None, :], 0), axis=2)
    k_in_tile = jnp.arange(CHUNKS_PER_TILE, dtype=I32)[None, :]
    valid = k_in_tile < nvalid[:, None]
    zero_chunk = loc // ROW_CHUNK - 1
    dump = nt * (loc // ROW_CHUNK) + (jnp.arange(n_tiles, dtype=I32)[:, None] % 2) * CHUNKS_PER_TILE + k_in_tile
    flat = lambda a: a.reshape(n_tiles * CHUNKS_PER_TILE)
    src_map = flat(jnp.where(valid, chunk_map, zero_chunk))
    dst_map = flat(jnp.where(valid, chunk_map, dump))

    tail_row = jnp.arange(nt, dtype=I32) * loc + used * ROW_CHUNK
    tail_n = loc // ROW_CHUNK - used
    as_i32 = lambda a: a.astype(I32)
    return tuple(map(as_i32, (tile_e, nused, src_map, dst_map, tail_row, tail_n)))


def kernel(x, norm1_g, w_in, conv_w, conv_b, w_gate_a, b_gate_a, w_gate_x, b_gate_x, lru_lambda,
           lru_out_g, q_norm_g, k_norm_g, lambda_q1, lambda_k1, lambda_q2, lambda_k2, sub_norm_g,
           w_out, norm2_g, w_router_group, b_router_group, w_router_expert, b_router_expert,
           w_expert_gate, w_expert_up, w_expert_down):
    b, s, d = x.shape
    assert d == D_MODEL and norm1_g.shape[0] == 1
    t = b * s
    l = 0
    x2 = x.reshape(t, d)

    scale = HEAD_DIM ** -0.5 * LOG2E
    qkg = jnp.stack([jnp.tile(q_norm_g[l], 2) * scale, jnp.tile(k_norm_g[l], 2)]).astype(F32)
    eye = jnp.eye(LRU_BLOCKS, dtype=F32)
    blockdiag = lambda w: jnp.einsum("ncd,nm->ncmd", w, eye).reshape(LRU_WIDTH, LRU_WIDTH)
    wgate = jnp.concatenate([blockdiag(w_gate_a[l]), blockdiag(w_gate_x[l])], axis=1).astype(BF16)
    bgate = jnp.concatenate([b_gate_a[l], b_gate_x[l]])[None, :]
    lam_params = jnp.stack([lambda_q1[l], lambda_k1[l], lambda_q2[l], lambda_k2[l]])
    wr = jnp.concatenate(
        [w_router_group[l], jnp.transpose(w_router_expert[l], (1, 0, 2)).reshape(d, N_EXPERTS)], axis=1)
    wr = jnp.pad(wr, ((0, 0), (0, LANES - wr.shape[1])))
    wr_hi = wr.astype(BF16)
    wr2 = jnp.concatenate([wr_hi, (wr - wr_hi.astype(F32)).astype(BF16)], axis=1)
    br = jnp.pad(jnp.concatenate([b_router_group[l], b_router_expert[l].reshape(-1)]),
                 (0, LANES - N_GROUPS - N_EXPERTS))[None, :]

    proj = _in_proj(x2, norm1_g[l][None, :], w_in[l], qkg)
    y_lru = _lru(proj, b, s, conv_w[l], conv_b[l][None, :], wgate, bgate,
                 lru_lambda[l][None, :], lru_out_g[l][None, :])
    y_att = _attention(proj, b, s, lam_params, sub_norm_g[l][None, :])
    h1, xl, aux, gt = _out_route(y_lru, y_att, x2, w_out[l], norm2_g[l][None, :],
                                 wr2, br)

    tm = min(TM_TOK, t)
    nt = t // tm
    loc = 2 * tm + N_EXPERTS * ROW_CHUNK
    max_rows = 2 * t + nt * N_EXPERTS * (ROW_CHUNK - 1) + N_EXPERTS * (TM_EXP - ROW_CHUNK)
    n_tiles = -(-max_rows // TM_EXP)
    grp = gt[:, :, 0].astype(I32)
    tables = _expert_tables(grp, n_tiles, loc)
    yl = _experts(xl, tables, w_expert_gate[l], w_expert_up[l], w_expert_down[l], n_tiles)
    out = _combine(h1, aux, yl, nt, tm, loc)
    return out.reshape(b, s, d)
```

```python
import functools
import math

import numpy as np
import jax
import jax.numpy as jnp
from jax import lax
from jax.experimental import pallas as pl
from jax.experimental.pallas import tpu as pltpu

F32 = jnp.float32
BF16 = jnp.bfloat16
I32 = jnp.int32

D_MODEL = 1024
LRU_WIDTH = 512
LRU_BLOCKS = 8
LRU_BLOCK_W = LRU_WIDTH // LRU_BLOCKS
CONV_W = 4
LRU_C = 8.0
ATT_WIDTH = 512
N_HEADS = 4
HEAD_DIM = 64
V_DIM = 128
IN_COLS = 2 * LRU_WIDTH + 3 * ATT_WIDTH
N_GROUPS = 4
EXPERTS_PER_GROUP = 4
N_EXPERTS = N_GROUPS * EXPERTS_PER_GROUP
D_EXPERT = D_MODEL // 2
CHUNK = 64
EPS = 1e-6
NEG_BIG = -1e30
LAMBDA_INIT = 0.8 - 0.6 * math.exp(-0.3 * 0)
LOG2E = math.log2(math.e)

LANES = 128
SUBLANES = 8

TM_PROJ = 512
TQ = 512
TM_TOK = 256
TM_EXP = 512
ROW_CHUNK = 2 * SUBLANES
VMEM_LIMIT = 56 * 1024 * 1024


def _cparams(n_axes):
    return pltpu.CompilerParams(
        dimension_semantics=("arbitrary",) * n_axes, vmem_limit_bytes=VMEM_LIMIT)


QKV_COLS = 3 * ATT_WIDTH


def _project(x_ref, g1_ref, qkg_ref, w_bf, lru_cur, qkv_ref):
    x = x_ref[...]
    ms = jnp.mean(x * x, axis=-1, keepdims=True)
    hn = ((x * lax.rsqrt(ms + EPS)) * g1_ref[...]).astype(BF16)
    tm = x.shape[0]
    lo_half = lax.broadcasted_iota(I32, (tm, LANES), 1) < HEAD_DIM
    width = 512
    for c in range(IN_COLS // width):
        c0 = c * width
        acc = jnp.dot(hn, w_bf[:, c0:c0 + width], preferred_element_type=F32)
        if c < 2:
            lru_cur[:, c0:c0 + width] = acc.astype(BF16)
        elif c < 4:
            gain = qkg_ref[c - 2:c - 1, :]
            o0 = c0 - 2 * LRU_WIDTH
            for b in range(width // LANES):
                blk = acc[:, b * LANES:(b + 1) * LANES]
                sq = blk * blk
                s_lo = jnp.sum(jnp.where(lo_half, sq, 0.0), axis=-1, keepdims=True)
                s_hi = jnp.sum(jnp.where(lo_half, 0.0, sq), axis=-1, keepdims=True)
                inv = jnp.where(lo_half,
                                lax.rsqrt(s_lo * (1.0 / HEAD_DIM) + EPS),
                                lax.rsqrt(s_hi * (1.0 / HEAD_DIM) + EPS))
                qkv_ref[:, o0 + b * LANES:o0 + (b + 1) * LANES] = ((blk * inv) * gain).astype(BF16)
        else:
            qkv_ref[:, c0 - 2 * LRU_WIDTH:c0 - 2 * LRU_WIDTH + width] = acc.astype(BF16)


def _lru_tile(first, p_ref, cw_ref, cb_ref, wg_ref, bg_ref, lam_ref, og_ref, o_ref, xbuf, hbuf, hc):
    ts = p_ref.shape[0]
    x = p_ref[:, 0:LRU_WIDTH].astype(F32)
    gl = p_ref[:, LRU_WIDTH:2 * LRU_WIDTH].astype(F32)
    xbuf[0:SUBLANES, :] = jnp.where(first, 0.0, xbuf[0:SUBLANES, :])
    xbuf[SUBLANES:SUBLANES + ts, :] = x
    xc = cb_ref[...]
    for j in range(CONV_W):
        r0 = SUBLANES - (CONV_W - 1) + j
        xc = xc + xbuf[r0:r0 + ts, :] * cw_ref[j:j + 1, :]
    xbuf[0:SUBLANES, :] = x[ts - SUBLANES:ts, :]

    z = jnp.dot(xc.astype(BF16), wg_ref[...], preferred_element_type=F32) + bg_ref[...]
    r = 0.5 * jnp.tanh(0.5 * z[:, 0:LRU_WIDTH]) + 0.5
    gi = 0.5 * jnp.tanh(0.5 * z[:, LRU_WIDTH:2 * LRU_WIDTH]) + 0.5
    nl = -lam_ref[...]
    softplus = jnp.maximum(nl, 0.0) + jnp.log1p(jnp.exp(-jnp.abs(nl)))
    log_a = (-LRU_C) * r * softplus
    a = jnp.exp(log_a)
    v = -jnp.tanh(log_a) * (a * a + 1.0)
    u = jnp.where(v > 0.0, v * lax.rsqrt(v), 0.0) * (gi * xc)

    row = lax.broadcasted_iota(I32, (ts, LRU_WIDTH), 0) & (SUBLANES - 1)
    ca, cb = a, u
    for d in (1, 2, 4):
        a_sh = pltpu.roll(ca, d, axis=0)
        b_sh = pltpu.roll(cb, d, axis=0)
        take = row >= d
        cb = jnp.where(take, ca * b_sh + cb, cb)
        ca = jnp.where(take, ca * a_sh, ca)
    h = jnp.where(first, 0.0, hc[...])
    for blk in range(ts // SUBLANES):
        r0 = blk * SUBLANES
        hb = ca[r0:r0 + SUBLANES, :] * h + cb[r0:r0 + SUBLANES, :]
        hbuf[r0:r0 + SUBLANES, :] = hb
        h = hb[SUBLANES - 1:SUBLANES, :]
    hc[...] = h

    y = hbuf[...] * jax.nn.gelu(gl)
    ms = jnp.mean(y * y, axis=-1, keepdims=True)
    o_ref[...] = ((y * lax.rsqrt(ms + EPS)) * og_ref[...]).astype(o_ref.dtype)


def _inproj_lru_body(x_ref, g1_ref, w_ref, qkg_ref, cw_ref, cb_ref, wg_ref, bg_ref, lam_ref, og_ref,
                     qkv_ref, ylru_ref, w_bf, lru_cur, lru_prev, xbuf, hbuf, hc, *, tiles_per_seq):
    i = pl.program_id(0)

    @pl.when(i == 0)
    def _():
        w_bf[...] = w_ref[...].astype(BF16)
        lru_prev[...] = jnp.zeros(lru_prev.shape, BF16)
        xbuf[...] = jnp.zeros(xbuf.shape, F32)
        hc[...] = jnp.zeros(hc.shape, F32)

    first = lax.rem(i + (tiles_per_seq - 1), tiles_per_seq) == 0
    _lru_tile(first, lru_prev, cw_ref, cb_ref, wg_ref, bg_ref, lam_ref, og_ref, ylru_ref, xbuf, hbuf, hc)
    _project(x_ref, g1_ref, qkg_ref, w_bf, lru_cur, qkv_ref)
    lru_prev[...] = lru_cur[...]


def _in_proj_lru(x2, seq, g1, w_in, qkg, conv_w, conv_b, wgate, bgate, lam, out_g):
    t = x2.shape[0]
    tm = min(TM_PROJ, seq)
    n = t // tm
    const = lambda shape: pl.BlockSpec(shape, lambda i: (0,) * len(shape))
    return pl.pallas_call(
        functools.partial(_inproj_lru_body, tiles_per_seq=seq // tm),
        grid=(n + 1,),
        in_specs=[
            pl.BlockSpec((tm, D_MODEL), lambda i: (jnp.minimum(i, n - 1), 0)),
            const((1, D_MODEL)),
            const((D_MODEL, IN_COLS)),
            const((2, LANES)),
            const((CONV_W, LRU_WIDTH)),
            const((1, LRU_WIDTH)),
            const((LRU_WIDTH, 2 * LRU_WIDTH)),
            const((1, 2 * LRU_WIDTH)),
            const((1, LRU_WIDTH)),
            const((1, LRU_WIDTH)),
        ],
        out_specs=[
            pl.BlockSpec((tm, QKV_COLS), lambda i: (jnp.minimum(i, n - 1), 0)),
            pl.BlockSpec((tm, LRU_WIDTH), lambda i: (jnp.maximum(i - 1, 0), 0)),
        ],
        out_shape=[
            jax.ShapeDtypeStruct((t, QKV_COLS), BF16),
            jax.ShapeDtypeStruct((t, LRU_WIDTH), BF16),
        ],
        scratch_shapes=[
            pltpu.VMEM((D_MODEL, IN_COLS), BF16),
            pltpu.VMEM((tm, 2 * LRU_WIDTH), BF16),
            pltpu.VMEM((tm, 2 * LRU_WIDTH), BF16),
            pltpu.VMEM((tm + SUBLANES, LRU_WIDTH), F32),
            pltpu.VMEM((tm, LRU_WIDTH), F32),
            pltpu.VMEM((1, LRU_WIDTH), F32),
        ],
        compiler_params=_cparams(1),
        name="in_proj_lru",
    )(x2, g1, w_in, qkg, conv_w, conv_b, wgate, bgate, lam, out_g)


SOFTMAX_ROWS = 32


def _attn_body(slope_ref, q_ref, k_ref, v_ref, bias_ref, lamp_ref, sg_ref, o_ref,
               qs_buf, s0, s1, p0, p1, a0, a1, m_buf, acc0, acc1, *, tq, nq):
    h = pl.program_id(1)
    slope = slope_ref[h]
    s_bufs, p_bufs, a_bufs, accs = (s0, s1), (p0, p1), (a0, a1), (acc0, acc1)
    lo_half = lax.broadcasted_iota(I32, (tq, LANES), 1) < HEAD_DIM
    for i in range(nq):
        q = q_ref[i * tq:(i + 1) * tq, :]
        zero = jnp.zeros_like(q)
        qs_buf[i, 0:tq, :] = jnp.where(lo_half, q, zero)
        qs_buf[i, tq:2 * tq, :] = jnp.where(lo_half, zero, q)
    ones = jnp.ones((tq, V_DIM), BF16)
    lp = lamp_ref[...]
    lam = (jnp.exp(jnp.sum(lp[0:1, :] * lp[1:2, :], axis=-1, keepdims=True))
           - jnp.exp(jnp.sum(lp[2:3, :] * lp[3:4, :], axis=-1, keepdims=True))
           + LAMBDA_INIT)
    pairs = [(i, j) for i in range(nq) for j in range(i + 1)]

    def scores(t):
        i, j = pairs[t]
        s_bufs[t % 2][...] = lax.dot_general(
            qs_buf[i], k_ref[j * tq:(j + 1) * tq, :], (((1,), (1,)), ((), ())),
            preferred_element_type=F32)

    def softmax(t):
        i, j = pairs[t]
        which = 1 if j == i else 0
        shift = slope * float(-(i - j) * tq)
        s_buf, p_buf, a_buf = s_bufs[t % 2], p_bufs[t % 2], a_bufs[t % 2]
        for r in range(2 * tq // SOFTMAX_ROWS):
            rows = slice(r * SOFTMAX_ROWS, (r + 1) * SOFTMAX_ROWS)
            sb = s_buf[rows, :] + bias_ref[0, which, rows, :]
            m_new = jnp.broadcast_to(jnp.max(sb, axis=-1, keepdims=True), (SOFTMAX_ROWS, LANES)) + shift
            if j > 0:
                m_old = m_buf[rows, :]
                m_new = jnp.maximum(m_old, m_new)
                a_buf[rows, :] = jnp.exp2(m_old - m_new)
            m_sub = m_new - shift
            p_buf[rows, :] = jnp.exp2(sb - jnp.concatenate([m_sub] * (tq // LANES), axis=1)).astype(BF16)
            m_buf[rows, :] = m_new

    def accumulate(t):
        i, j = pairs[t]
        acc = accs[i % 2]
        v_aug = jnp.concatenate([v_ref[j * tq:(j + 1) * tq, :], ones], axis=1)
        pv = jnp.dot(p_bufs[t % 2][...], v_aug, preferred_element_type=F32)
        if j == 0:
            acc[...] = pv
        else:
            alpha = a_bufs[t % 2][...]
            acc[...] = jnp.concatenate([alpha] * (2 * V_DIM // LANES), axis=1) * acc[...] + pv
        if j == i:
            o = (acc[0:tq, 0:V_DIM] / acc[0:tq, V_DIM:V_DIM + 1]
                 - lam * (acc[tq:2 * tq, 0:V_DIM] / acc[tq:2 * tq, V_DIM:V_DIM + 1]))
            ms = jnp.mean(o * o, axis=-1, keepdims=True)
            o = ((o * lax.rsqrt(ms + EPS)) * sg_ref[...]) * (1.0 - LAMBDA_INIT)
            o_ref[i * tq:(i + 1) * tq, :] = o.astype(o_ref.dtype)

    scores(0)
    for t in range(len(pairs)):
        if t + 1 < len(pairs):
            scores(t + 1)
        softmax(t)
        if t >= 1:
            accumulate(t - 1)
    accumulate(len(pairs) - 1)


def _alibi_tables(tq):
    slopes = np.exp2(-8.0 * np.arange(1, N_HEADS + 1, dtype=np.float64) / N_HEADS)
    qi = np.arange(tq)[:, None]
    kj = np.arange(tq)[None, :]
    off = -(slopes[:, None, None] * (qi - kj)[None])
    allowed = (kj // CHUNK) <= (qi // CHUNK)
    diag = np.where(allowed[None], -(slopes[:, None, None] * np.abs(qi - kj)[None]), NEG_BIG)
    tab = np.stack([off, diag], axis=1)
    tab = np.concatenate([tab, tab], axis=2)
    return jnp.asarray(tab * LOG2E, F32), jnp.asarray(slopes * LOG2E, F32)


def _attention(proj, b, s, lam_params, sub_g):
    tq = min(TQ, s)
    nq = s // tq
    bias, slopes = _alibi_tables(tq)
    qcol = 0
    kcol = qcol + ATT_WIDTH // LANES
    vcol = kcol + ATT_WIDTH // LANES
    score_buf = pltpu.VMEM((2 * tq, tq), F32)
    prob_buf = pltpu.VMEM((2 * tq, tq), BF16)
    col_buf = pltpu.VMEM((2 * tq, LANES), F32)
    acc_buf = pltpu.VMEM((2 * tq, 2 * V_DIM), F32)
    grid_spec = pltpu.PrefetchScalarGridSpec(
        num_scalar_prefetch=1,
        grid=(b, N_HEADS),
        in_specs=[
            pl.BlockSpec((s, LANES), lambda bi, h, sl: (bi, qcol + h)),
            pl.BlockSpec((s, LANES), lambda bi, h, sl: (bi, kcol + h)),
            pl.BlockSpec((s, LANES), lambda bi, h, sl: (bi, vcol + h)),
            pl.BlockSpec((1, 2, 2 * tq, tq), lambda bi, h, sl: (h, 0, 0, 0)),
            pl.BlockSpec((4, HEAD_DIM), lambda bi, h, sl: (0, 0)),
            pl.BlockSpec((1, V_DIM), lambda bi, h, sl: (0, 0)),
        ],
        out_specs=pl.BlockSpec((s, V_DIM), lambda bi, h, sl: (bi, h)),
        scratch_shapes=[
            pltpu.VMEM((nq, 2 * tq, LANES), BF16),
            score_buf, score_buf, prob_buf, prob_buf, col_buf, col_buf, col_buf, acc_buf, acc_buf,
        ],
    )
    return pl.pallas_call(
        functools.partial(_attn_body, tq=tq, nq=nq),
        grid_spec=grid_spec,
        out_shape=jax.ShapeDtypeStruct((b * s, ATT_WIDTH), BF16),
        compiler_params=_cparams(2),
        name="diff_attn",
    )(slopes, proj, proj, proj, bias, lam_params, sub_g)


def _first_max4(v0, v1, v2, v3):
    m = jnp.maximum(jnp.maximum(v0, v1), jnp.maximum(v2, v3))
    idx = jnp.where(v0 == m, 0, jnp.where(v1 == m, 1, jnp.where(v2 == m, 2, 3))).astype(I32)
    return m, idx


def _route_body(yl_ref, ya_ref, x_ref, wo_ref, g2_ref, wr2_ref, br_ref,
                h1_ref, xl_ref, aux_ref, gt_ref, wo_bf):
    @pl.when(pl.program_id(0) == 0)
    def _():
        wo_bf[...] = wo_ref[...].astype(BF16)

    tm = x_ref.shape[0]
    mix = (jnp.dot(yl_ref[...], wo_bf[0:LRU_WIDTH, :], preferred_element_type=F32)
           + jnp.dot(ya_ref[...], wo_bf[LRU_WIDTH:, :], preferred_element_type=F32))
    h1 = x_ref[...] + mix
    h1_ref[...] = h1
    ms = jnp.mean(h1 * h1, axis=-1, keepdims=True)
    hn = (h1 * lax.rsqrt(ms + EPS)) * g2_ref[...]
    hn_hi = hn.astype(BF16)
    hn_lo = (hn - hn_hi.astype(F32)).astype(BF16)
    logits = (jnp.dot(hn_hi, wr2_ref[:, 0:LANES], preferred_element_type=F32)
              + jnp.dot(hn_hi, wr2_ref[:, LANES:2 * LANES], preferred_element_type=F32)
              + jnp.dot(hn_lo, wr2_ref[:, 0:LANES], preferred_element_type=F32)) + br_ref[...]
    lt = logits.T
    row = lambda n: lt[n:n + 1, :]
    gmax, gidx = _first_max4(row(0), row(1), row(2), row(3))
    zg = (jnp.exp(row(0) - gmax) + jnp.exp(row(1) - gmax)
          + jnp.exp(row(2) - gmax) + jnp.exp(row(3) - gmax))
    g_gate = 1.0 / zg
    base = N_GROUPS
    sel = [jnp.where(gidx == 0, row(base + j),
                     jnp.where(gidx == 1, row(base + 4 + j),
                               jnp.where(gidx == 2, row(base + 8 + j), row(base + 12 + j))))
           for j in range(EXPERTS_PER_GROUP)]
    m1, i1 = _first_max4(*sel)
    ze = sum(jnp.exp(sj - m1) for sj in sel)
    rest = [jnp.where(i1 == j, -jnp.inf, sel[j]) for j in range(EXPERTS_PER_GROUP)]
    m2, i2 = _first_max4(*rest)
    p1 = 1.0 / ze
    p2 = jnp.exp(m2 - m1) / ze
    gate1 = g_gate * (p1 / (p1 + p2))
    gate2 = g_gate * (p2 / (p1 + p2))
    e1 = gidx * EXPERTS_PER_GROUP + i1
    e2 = gidx * EXPERTS_PER_GROUP + i2

    eio = lax.broadcasted_iota(I32, (N_EXPERTS, tm), 0)
    oh1 = eio == e1
    oh2 = eio == e2
    both = (oh1 | oh2).astype(F32)
    cnt = jnp.sum(both, axis=1, keepdims=True)
    grp = jnp.floor((cnt + (ROW_CHUNK - 1)) * (1.0 / ROW_CHUNK)) * ROW_CHUNK
    ti = lax.broadcasted_iota(I32, (tm, tm), 0)
    tj = lax.broadcasted_iota(I32, (tm, tm), 1)
    before = (ti < tj).astype(BF16)
    rank = jnp.dot(both.astype(BF16), before, preferred_element_type=F32)
    start1 = jnp.sum(jnp.where(eio < e1, grp, 0.0), axis=0, keepdims=True)
    start2 = jnp.sum(jnp.where(eio < e2, grp, 0.0), axis=0, keepdims=True)
    slot1 = start1 + jnp.sum(jnp.where(oh1, rank, 0.0), axis=0, keepdims=True)
    slot2 = start2 + jnp.sum(jnp.where(oh2, rank, 0.0), axis=0, keepdims=True)

    loc = xl_ref.shape[0]
    sio = lax.broadcasted_iota(I32, (loc, tm), 0)
    perm = ((sio == slot1.astype(I32)) | (sio == slot2.astype(I32))).astype(BF16)
    xs = jnp.dot(perm, hn_hi, preferred_element_type=F32)
    xl_ref[...] = xs.astype(BF16)

    rio = lax.broadcasted_iota(I32, (LANES, tm), 0)
    aux_t = jnp.where(rio == 0, slot1, jnp.where(rio == 1, slot2,
                      jnp.where(rio == 2, gate1, jnp.where(rio == 3, gate2, 0.0))))
    aux_ref[...] = aux_t.T
    gt_ref[0] = jnp.broadcast_to(grp, (N_EXPERTS, LANES))


def _out_route(y_lru, y_att, x2, w_out, g2, wr2, br):
    t = x2.shape[0]
    tm = min(TM_TOK, t)
    nt = t // tm
    loc = 2 * tm + N_EXPERTS * ROW_CHUNK
    const = lambda shape: pl.BlockSpec(shape, lambda i: (0,) * len(shape))
    return pl.pallas_call(
        _route_body,
        grid=(nt,),
        in_specs=[
            pl.BlockSpec((tm, LRU_WIDTH), lambda i: (i, 0)),
            pl.BlockSpec((tm, ATT_WIDTH), lambda i: (i, 0)),
            pl.BlockSpec((tm, D_MODEL), lambda i: (i, 0)),
            const((D_MODEL, D_MODEL)),
            const((1, D_MODEL)),
            const((D_MODEL, 2 * LANES)),
            const((1, LANES)),
        ],
        out_specs=[
            pl.BlockSpec((tm, D_MODEL), lambda i: (i, 0)),
            pl.BlockSpec((loc, D_MODEL), lambda i: (i, 0)),
            pl.BlockSpec((tm, LANES), lambda i: (i, 0)),
            pl.BlockSpec((1, N_EXPERTS, LANES), lambda i: (i, 0, 0)),
        ],
        out_shape=[
            jax.ShapeDtypeStruct((t, D_MODEL), F32),
            jax.ShapeDtypeStruct((nt * loc, D_MODEL), BF16),
            jax.ShapeDtypeStruct((t, LANES), F32),
            jax.ShapeDtypeStruct((nt, N_EXPERTS, LANES), F32),
        ],
        scratch_shapes=[pltpu.VMEM((D_MODEL, D_MODEL), BF16)],
        compiler_params=_cparams(1),
        name="out_route",
    )(y_lru, y_att, x2, w_out, g2, wr2, br)


CHUNKS_PER_TILE = TM_EXP // ROW_CHUNK


DUMP_CHUNKS = 2 * CHUNKS_PER_TILE


def _expert_body(tile_e_ref, nused_ref, src_ref, dst_ref, tail_row_ref, tail_n_ref,
                 xl_ref, wg_ref, wu_ref, wd_ref, yl_ref,
                 xbuf, ybuf, zbuf, wg_bf, wu_bf, wd_bf, gsem, ssem, zsem):
    j = pl.program_id(0)
    nused = nused_ref[0]
    nt = tail_n_ref.shape[0]
    slot = lax.rem(j, 2)
    dump_row = yl_ref.shape[0] - DUMP_CHUNKS * ROW_CHUNK

    def rows(c):
        if isinstance(c, int):
            return pl.ds(c * ROW_CHUNK, ROW_CHUNK)
        return pl.ds(pl.multiple_of(c * ROW_CHUNK, ROW_CHUNK), ROW_CHUNK)

    def start_gather(t, sl):
        for k in range(CHUNKS_PER_TILE):
            pltpu.make_async_copy(xl_ref.at[rows(src_ref[t * CHUNKS_PER_TILE + k])],
                                  xbuf.at[sl, rows(k)], gsem.at[sl]).start()

    def start_scatter(t, sl):
        for k in range(CHUNKS_PER_TILE):
            pltpu.make_async_copy(ybuf.at[sl, rows(k)],
                                  yl_ref.at[rows(dst_ref[t * CHUNKS_PER_TILE + k])], ssem.at[sl]).start()

    def wait_gather(sl):
        pltpu.make_async_copy(xl_ref.at[pl.ds(0, TM_EXP)], xbuf.at[sl], gsem.at[sl]).wait()

    def wait_scatter(sl):
        pltpu.make_async_copy(ybuf.at[sl], yl_ref.at[pl.ds(0, TM_EXP)], ssem.at[sl]).wait()

    def for_count(n, fn):
        def body(k, c):
            fn(k)
            return c
        lax.fori_loop(0, n, body, 0)

    def zero_fill(i, k):
        return pltpu.make_async_copy(
            zbuf, yl_ref.at[pl.ds(pl.multiple_of(tail_row_ref[i] + k * ROW_CHUNK, ROW_CHUNK), ROW_CHUNK)],
            zsem)

    @pl.when(j == 0)
    def _():
        zbuf[...] = jnp.zeros(zbuf.shape, BF16)
        ybuf[...] = jnp.zeros(ybuf.shape, BF16)
        for_count(nt, lambda i: for_count(tail_n_ref[i], lambda k: zero_fill(i, k).start()))
        start_gather(0, 0)
        for sl in range(2):
            for k in range(CHUNKS_PER_TILE):
                pltpu.make_async_copy(
                    ybuf.at[sl, rows(k)],
                    yl_ref.at[pl.ds(dump_row + (sl * CHUNKS_PER_TILE + k) * ROW_CHUNK, ROW_CHUNK)],
                    ssem.at[sl]).start()

    new_expert = (j == 0) | (tile_e_ref[j] != tile_e_ref[jnp.maximum(j - 1, 0)])

    @pl.when(new_expert & (j < nused))
    def _():
        wg_bf[...] = wg_ref[0].astype(BF16)
        wu_bf[...] = wu_ref[0].astype(BF16)
        wd_bf[...] = wd_ref[0].astype(BF16)

    @pl.when(j < nused)
    def _():
        wait_gather(slot)
        wait_scatter(slot)
        start_gather(jnp.minimum(j + 1, nused - 1), 1 - slot)
        xb = xbuf[slot]
        y = None
        for c in range(2):
            cols = slice(c * (D_EXPERT // 2), (c + 1) * (D_EXPERT // 2))
            gate = jnp.dot(xb, wg_bf[:, cols], preferred_element_type=F32)
            up = jnp.dot(xb, wu_bf[:, cols], preferred_element_type=F32)
            hid = (jax.nn.silu(gate) * up).astype(BF16)
            part = jnp.dot(hid, wd_bf[cols, :], preferred_element_type=F32)
            y = part if y is None else y + part
        ybuf[slot] = y.astype(BF16)
        start_scatter(j, slot)

    @pl.when(j == nused - 1)
    def _():
        wait_gather(1 - slot)
        wait_scatter(1 - slot)
        wait_scatter(slot)
        for_count(nt, lambda i: for_count(tail_n_ref[i], lambda k: zero_fill(i, 0).wait()))


def _experts(xl, tables, w_gate, w_up, w_down, n_tiles):
    tile_e, nused, src_map, dst_map, tail_row, tail_n = tables
    wmap = lambda j, te, *_: (te[j], 0, 0)
    grid_spec = pltpu.PrefetchScalarGridSpec(
        num_scalar_prefetch=6,
        grid=(n_tiles,),
        in_specs=[
            pl.BlockSpec(memory_space=pl.ANY),
            pl.BlockSpec((1, D_MODEL, D_EXPERT), wmap),
            pl.BlockSpec((1, D_MODEL, D_EXPERT), wmap),
            pl.BlockSpec((1, D_EXPERT, D_MODEL), wmap),
        ],
        out_specs=pl.BlockSpec(memory_space=pl.ANY),
        scratch_shapes=[
            pltpu.VMEM((2, TM_EXP, D_MODEL), BF16),
            pltpu.VMEM((2, TM_EXP, D_MODEL), BF16),
            pltpu.VMEM((ROW_CHUNK, D_MODEL), BF16),
            pltpu.VMEM((D_MODEL, D_EXPERT), BF16),
            pltpu.VMEM((D_MODEL, D_EXPERT), BF16),
            pltpu.VMEM((D_EXPERT, D_MODEL), BF16),
            pltpu.SemaphoreType.DMA((2,)),
            pltpu.SemaphoreType.DMA((2,)),
            pltpu.SemaphoreType.DMA,
        ],
    )
    return pl.pallas_call(
        _expert_body,
        grid_spec=grid_spec,
        out_shape=jax.ShapeDtypeStruct((xl.shape[0] + DUMP_CHUNKS * ROW_CHUNK, D_MODEL), BF16),
        compiler_params=_cparams(1),
        name="experts",
    )(tile_e, nused, src_map, dst_map, tail_row, tail_n, xl, w_gate, w_up, w_down)


def _combine_body(h1_ref, aux_ref, yl_ref, o_ref):
    tm = h1_ref.shape[0]
    loc = yl_ref.shape[0]
    aux = aux_ref[...]
    slot1 = aux[:, 0:1].astype(I32)
    slot2 = aux[:, 1:2].astype(I32)
    sio = lax.broadcasted_iota(I32, (tm, loc), 1)
    gm = jnp.where(sio == slot1, aux[:, 2:3], 0.0) + jnp.where(sio == slot2, aux[:, 3:4], 0.0)
    o_ref[...] = h1_ref[...] + jnp.dot(gm.astype(BF16), yl_ref[...], preferred_element_type=F32)


def _combine(h1, aux, yl, nt, tm, loc):
    return pl.pallas_call(
        _combine_body,
        grid=(nt,),
        in_specs=[
            pl.BlockSpec((tm, D_MODEL), lambda i: (i, 0)),
            pl.BlockSpec((tm, LANES), lambda i: (i, 0)),
            pl.BlockSpec((loc, D_MODEL), lambda i: (i, 0)),
        ],
        out_specs=pl.BlockSpec((tm, D_MODEL), lambda i: (i, 0)),
        out_shape=jax.ShapeDtypeStruct(h1.shape, F32),
        compiler_params=_cparams(1),
        name="combine",
    )(h1, aux, yl)


def _excl_cumsum(a, axis):
    return jnp.cumsum(a, axis=axis) - a


def _expert_tables(grp, n_tiles, loc):
    nt = grp.shape[0]
    gch = grp // ROW_CHUNK
    loc_start = _excl_cumsum(gch, 1)
    used = jnp.sum(gch, axis=1)
    col = jnp.sum(gch, axis=0)
    seg = ((col + CHUNKS_PER_TILE - 1) // CHUNKS_PER_TILE) * CHUNKS_PER_TILE
    seg_end = jnp.cumsum(seg)
    off = seg_end - seg
    tile_first = jnp.arange(n_tiles, dtype=I32) * CHUNKS_PER_TILE
    tile_e = jnp.minimum(jnp.sum(seg_end[None, :] <= tile_first[:, None], axis=1), N_EXPERTS - 1)
    nused = seg_end[-1:] // CHUNKS_PER_TILE
    onehot = tile_e[:, None] == jnp.arange(N_EXPERTS, dtype=I32)[None, :]
    pick = lambda tab: jnp.sum(jnp.where(onehot[:, :, None], tab.T[None], 0), axis=1)
    pick1 = lambda vec: jnp.sum(jnp.where(onehot, vec[None, :], 0), axis=1)
    first = tile_first - pick1(off)
    nvalid = jnp.clip(pick1(col) - first, 0, CHUNKS_PER_TILE)
    cum = jnp.cumsum(gch, axis=0)
    delta = jnp.arange(nt, dtype=I32)[:, None] * (loc // ROW_CHUNK) + loc_start - (cum - gch)
    step = delta - jnp.concatenate([jnp.zeros((1, N_EXPERTS), I32), delta[:-1]], axis=0)
    cum_prev = jnp.concatenate([jnp.full((1, N_EXPERTS), -1, I32), cum[:-1]], axis=0)
    cc = first[:, None] + jnp.arange(CHUNKS_PER_TILE, dtype=I32)[None, :]
    passed = pick(cum_prev)[:, None, :] <= cc[:, :, None]
    chunk_map = cc + jnp.sum(jnp.where(passed, pick(step)[:, None, :], 0), axis=2)
    k_in_tile = jnp.arange(CHUNKS_PER_TILE, dtype=I32)[None, :]
    valid = k_in_tile < nvalid[:, None]
    zero_chunk = loc // ROW_CHUNK - 1
    dump = nt * (loc // ROW_CHUNK) + (jnp.arange(n_tiles, dtype=I32)[:, None] % 2) * CHUNKS_PER_TILE + k_in_tile
    flat = lambda a: a.reshape(n_tiles * CHUNKS_PER_TILE)
    src_map = flat(jnp.where(valid, chunk_map, zero_chunk))
    dst_map = flat(jnp.where(valid, chunk_map, dump))

    tail_row = jnp.arange(nt, dtype=I32) * loc + used * ROW_CHUNK
    tail_n = loc // ROW_CHUNK - used
    as_i32 = lambda a: a.astype(I32)
    return tuple(map(as_i32, (tile_e, nused, src_map, dst_map, tail_row, tail_n)))


def kernel(x, norm1_g, w_in, conv_w, conv_b, w_gate_a, b_gate_a, w_gate_x, b_gate_x, lru_lambda,
           lru_out_g, q_norm_g, k_norm_g, lambda_q1, lambda_k1, lambda_q2, lambda_k2, sub_norm_g,
           w_out, norm2_g, w_router_group, b_router_group, w_router_expert, b_router_expert,
           w_expert_gate, w_expert_up, w_expert_down):
    b, s, d = x.shape
    assert d == D_MODEL and norm1_g.shape[0] == 1
    t = b * s
    l = 0
    x2 = x.reshape(t, d)

    scale = HEAD_DIM ** -0.5 * LOG2E
    qkg = jnp.stack([jnp.tile(q_norm_g[l], 2) * scale, jnp.tile(k_norm_g[l], 2)]).astype(F32)
    eye = jnp.eye(LRU_BLOCKS, dtype=F32)
    blockdiag = lambda w: jnp.einsum("ncd,nm->ncmd", w, eye).reshape(LRU_WIDTH, LRU_WIDTH)
    wgate = jnp.concatenate([blockdiag(w_gate_a[l]), blockdiag(w_gate_x[l])], axis=1).astype(BF16)
    bgate = jnp.concatenate([b_gate_a[l], b_gate_x[l]])[None, :]
    lam_params = jnp.stack([lambda_q1[l], lambda_k1[l], lambda_q2[l], lambda_k2[l]])
    wr = jnp.concatenate(
        [w_router_group[l], jnp.transpose(w_router_expert[l], (1, 0, 2)).reshape(d, N_EXPERTS)], axis=1)
    wr = jnp.pad(wr, ((0, 0), (0, LANES - wr.shape[1])))
    wr_hi = wr.astype(BF16)
    wr2 = jnp.concatenate([wr_hi, (wr - wr_hi.astype(F32)).astype(BF16)], axis=1)
    br = jnp.pad(jnp.concatenate([b_router_group[l], b_router_expert[l].reshape(-1)]),
                 (0, LANES - N_GROUPS - N_EXPERTS))[None, :]

    proj, y_lru = _in_proj_lru(x2, s, norm1_g[l][None, :], w_in[l], qkg, conv_w[l], conv_b[l][None, :],
                               wgate, bgate, lru_lambda[l][None, :], lru_out_g[l][None, :])
    y_att = _attention(proj, b, s, lam_params, sub_norm_g[l][None, :])
    h1, xl, aux, gt = _out_route(y_lru, y_att, x2, w_out[l], norm2_g[l][None, :],
                                 wr2, br)

    tm = min(TM_TOK, t)
    nt = t // tm
    loc = 2 * tm + N_EXPERTS * ROW_CHUNK
    max_rows = 2 * t + nt * N_EXPERTS * (ROW_CHUNK - 1) + N_EXPERTS * (TM_EXP - ROW_CHUNK)
    n_tiles = -(-max_rows // TM_EXP)
    grp = gt[:, :, 0].astype(I32)
    tables = _expert_tables(grp, n_tiles, loc)
    yl = _experts(xl, tables, w_expert_gate[l], w_expert_up[l], w_expert_down[l], n_tiles)
    out = _combine(h1, aux, yl, nt, tm, loc)
    return out.reshape(b, s, d)
```

```python
import functools
import math

import numpy as np
import jax
import jax.numpy as jnp
from jax import lax
from jax.experimental import pallas as pl
from jax.experimental.pallas import tpu as pltpu

F32 = jnp.float32
BF16 = jnp.bfloat16
I32 = jnp.int32

D_MODEL = 1024
LRU_WIDTH = 512
LRU_BLOCKS = 8
LRU_BLOCK_W = LRU_WIDTH // LRU_BLOCKS
CONV_W = 4
LRU_C = 8.0
ATT_WIDTH = 512
N_HEADS = 4
HEAD_DIM = 64
V_DIM = 128
IN_COLS = 2 * LRU_WIDTH + 3 * ATT_WIDTH
N_GROUPS = 4
EXPERTS_PER_GROUP = 4
N_EXPERTS = N_GROUPS * EXPERTS_PER_GROUP
D_EXPERT = D_MODEL // 2
CHUNK = 64
EPS = 1e-6
NEG_BIG = -1e30
LAMBDA_INIT = 0.8 - 0.6 * math.exp(-0.3 * 0)
LOG2E = math.log2(math.e)

LANES = 128
SUBLANES = 8

TM_PROJ = 512
TQ = 512
TM_TOK = 256
TM_EXP = 512
ROW_CHUNK = 2 * SUBLANES
VMEM_LIMIT = 56 * 1024 * 1024


def _cparams(n_axes):
    return pltpu.CompilerParams(
        dimension_semantics=("arbitrary",) * n_axes, vmem_limit_bytes=VMEM_LIMIT)


QKV_COLS = 3 * ATT_WIDTH


def _project(x_ref, g1_ref, qkg_ref, w_bf, lru_cur, qkv_ref):
    x = x_ref[...]
    ms = jnp.mean(x * x, axis=-1, keepdims=True)
    hn = ((x * lax.rsqrt(ms + EPS)) * g1_ref[...]).astype(BF16)
    tm = x.shape[0]
    lo_half = lax.broadcasted_iota(I32, (tm, LANES), 1) < HEAD_DIM
    width = 512
    for c in range(IN_COLS // width):
        c0 = c * width
        acc = jnp.dot(hn, w_bf[:, c0:c0 + width], preferred_element_type=F32)
        if c < 2:
            lru_cur[:, c0:c0 + width] = acc.astype(BF16)
        elif c < 4:
            gain = qkg_ref[c - 2:c - 1, :]
            o0 = c0 - 2 * LRU_WIDTH
            for b in range(width // LANES):
                blk = acc[:, b * LANES:(b + 1) * LANES]
                sq = blk * blk
                s_lo = jnp.sum(jnp.where(lo_half, sq, 0.0), axis=-1, keepdims=True)
                s_hi = jnp.sum(jnp.where(lo_half, 0.0, sq), axis=-1, keepdims=True)
                inv = jnp.where(lo_half,
                                lax.rsqrt(s_lo * (1.0 / HEAD_DIM) + EPS),
                                lax.rsqrt(s_hi * (1.0 / HEAD_DIM) + EPS))
                qkv_ref[:, o0 + b * LANES:o0 + (b + 1) * LANES] = ((blk * inv) * gain).astype(BF16)
        else:
            qkv_ref[:, c0 - 2 * LRU_WIDTH:c0 - 2 * LRU_WIDTH + width] = acc.astype(BF16)


def _lru_tile(first, p_ref, cw_ref, cb_ref, wg_ref, bg_ref, lam_ref, og_ref, o_ref, xbuf, hbuf, hc):
    ts = p_ref.shape[0]
    x = p_ref[:, 0:LRU_WIDTH].astype(F32)
    gl = p_ref[:, LRU_WIDTH:2 * LRU_WIDTH].astype(F32)
    xbuf[0:SUBLANES, :] = jnp.where(first, 0.0, xbuf[0:SUBLANES, :])
    xbuf[SUBLANES:SUBLANES + ts, :] = x
    xc = cb_ref[...]
    for j in range(CONV_W):
        r0 = SUBLANES - (CONV_W - 1) + j
        xc = xc + xbuf[r0:r0 + ts, :] * cw_ref[j:j + 1, :]
    xbuf[0:SUBLANES, :] = x[ts - SUBLANES:ts, :]

    z = jnp.dot(xc.astype(BF16), wg_ref[...], preferred_element_type=F32) + bg_ref[...]
    r = 0.5 * jnp.tanh(0.5 * z[:, 0:LRU_WIDTH]) + 0.5
    gi = 0.5 * jnp.tanh(0.5 * z[:, LRU_WIDTH:2 * LRU_WIDTH]) + 0.5
    nl = -lam_ref[...]
    softplus = jnp.maximum(nl, 0.0) + jnp.log1p(jnp.exp(-jnp.abs(nl)))
    log_a = (-LRU_C) * r * softplus
    a = jnp.exp(log_a)
    v = -jnp.tanh(log_a) * (a * a + 1.0)
    u = jnp.where(v > 0.0, v * lax.rsqrt(v), 0.0) * (gi * xc)

    row = lax.broadcasted_iota(I32, (ts, LRU_WIDTH), 0) & (SUBLANES - 1)
    ca, cb = a, u
    for d in (1, 2, 4):
        a_sh = pltpu.roll(ca, d, axis=0)
        b_sh = pltpu.roll(cb, d, axis=0)
        take = row >= d
        cb = jnp.where(take, ca * b_sh + cb, cb)
        ca = jnp.where(take, ca * a_sh, ca)
    h = jnp.where(first, 0.0, hc[...])
    for blk in range(ts // SUBLANES):
        r0 = blk * SUBLANES
        hb = ca[r0:r0 + SUBLANES, :] * h + cb[r0:r0 + SUBLANES, :]
        hbuf[r0:r0 + SUBLANES, :] = hb
        h = hb[SUBLANES - 1:SUBLANES, :]
    hc[...] = h

    y = hbuf[...] * jax.nn.gelu(gl)
    ms = jnp.mean(y * y, axis=-1, keepdims=True)
    o_ref[...] = ((y * lax.rsqrt(ms + EPS)) * og_ref[...]).astype(o_ref.dtype)


def _inproj_lru_body(x_ref, g1_ref, w_ref, qkg_ref, cw_ref, cb_ref, wg_ref, bg_ref, lam_ref, og_ref,
                     qkv_ref, ylru_ref, w_bf, lru_cur, lru_prev, xbuf, hbuf, hc, *, tiles_per_seq):
    i = pl.program_id(0)

    @pl.when(i == 0)
    def _():
        w_bf[...] = w_ref[...].astype(BF16)
        lru_prev[...] = jnp.zeros(lru_prev.shape, BF16)
        xbuf[...] = jnp.zeros(xbuf.shape, F32)
        hc[...] = jnp.zeros(hc.shape, F32)

    first = lax.rem(i + (tiles_per_seq - 1), tiles_per_seq) == 0
    _lru_tile(first, lru_prev, cw_ref, cb_ref, wg_ref, bg_ref, lam_ref, og_ref, ylru_ref, xbuf, hbuf, hc)
    _project(x_ref, g1_ref, qkg_ref, w_bf, lru_cur, qkv_ref)
    lru_prev[...] = lru_cur[...]


def _in_proj_lru(x2, seq, g1, w_in, qkg, conv_w, conv_b, wgate, bgate, lam, out_g):
    t = x2.shape[0]
    tm = min(TM_PROJ, seq)
    n = t // tm
    const = lambda shape: pl.BlockSpec(shape, lambda i: (0,) * len(shape))
    return pl.pallas_call(
        functools.partial(_inproj_lru_body, tiles_per_seq=seq // tm),
        grid=(n + 1,),
        in_specs=[
            pl.BlockSpec((tm, D_MODEL), lambda i: (jnp.minimum(i, n - 1), 0)),
            const((1, D_MODEL)),
            const((D_MODEL, IN_COLS)),
            const((2, LANES)),
            const((CONV_W, LRU_WIDTH)),
            const((1, LRU_WIDTH)),
            const((LRU_WIDTH, 2 * LRU_WIDTH)),
            const((1, 2 * LRU_WIDTH)),
            const((1, LRU_WIDTH)),
            const((1, LRU_WIDTH)),
        ],
        out_specs=[
            pl.BlockSpec((tm, QKV_COLS), lambda i: (jnp.minimum(i, n - 1), 0)),
            pl.BlockSpec((tm, LRU_WIDTH), lambda i: (jnp.maximum(i - 1, 0), 0)),
        ],
        out_shape=[
            jax.ShapeDtypeStruct((t, QKV_COLS), BF16),
            jax.ShapeDtypeStruct((t, LRU_WIDTH), BF16),
        ],
        scratch_shapes=[
            pltpu.VMEM((D_MODEL, IN_COLS), BF16),
            pltpu.VMEM((tm, 2 * LRU_WIDTH), BF16),
            pltpu.VMEM((tm, 2 * LRU_WIDTH), BF16),
            pltpu.VMEM((tm + SUBLANES, LRU_WIDTH), F32),
            pltpu.VMEM((tm, LRU_WIDTH), F32),
            pltpu.VMEM((1, LRU_WIDTH), F32),
        ],
        compiler_params=_cparams(1),
        name="in_proj_lru",
    )(x2, g1, w_in, qkg, conv_w, conv_b, wgate, bgate, lam, out_g)


SOFTMAX_ROWS = 32


def _attn_body(slope_ref, q_ref, k_ref, v_ref, bias_ref, lamp_ref, sg_ref, o_ref,
               qs_buf, s0, s1, p0, p1, a0, a1, m_buf, acc0, acc1, *, tq, nq):
    h = pl.program_id(1)
    slope = slope_ref[h]
    s_bufs, p_bufs, a_bufs, accs = (s0, s1), (p0, p1), (a0, a1), (acc0, acc1)
    lo_half = lax.broadcasted_iota(I32, (tq, LANES), 1) < HEAD_DIM
    for i in range(nq):
        q = q_ref[i * tq:(i + 1) * tq, :]
        zero = jnp.zeros_like(q)
        qs_buf[i, 0:tq, :] = jnp.where(lo_half, q, zero)
        qs_buf[i, tq:2 * tq, :] = jnp.where(lo_half, zero, q)
    ones = jnp.ones((tq, V_DIM), BF16)
    lp = lamp_ref[...]
    lam = (jnp.exp(jnp.sum(lp[0:1, :] * lp[1:2, :], axis=-1, keepdims=True))
           - jnp.exp(jnp.sum(lp[2:3, :] * lp[3:4, :], axis=-1, keepdims=True))
           + LAMBDA_INIT)
    pairs = [(i, j) for i in range(nq) for j in range(i + 1)]
    hq = tq // 2

    def row_sets(i, j):
        if j < i or hq % CHUNK:
            return [((0, 2 * tq), tq)]
        return [((0, hq), hq), ((hq, tq), tq), ((tq, tq + hq), hq), ((tq + hq, 2 * tq), tq)]

    def scores(t):
        i, j = pairs[t]
        s_buf = s_bufs[t % 2]
        nt_dims = (((1,), (1,)), ((), ()))
        if j < i or hq % CHUNK:
            s_buf[...] = lax.dot_general(qs_buf[i], k_ref[j * tq:(j + 1) * tq, :], nt_dims,
                                         preferred_element_type=F32)
            return
        s_buf[:, 0:hq] = lax.dot_general(qs_buf[i], k_ref[j * tq:j * tq + hq, :], nt_dims,
                                         preferred_element_type=F32)
        for (r0, r1), width in row_sets(i, j):
            if width == tq:
                s_buf[r0:r1, hq:tq] = lax.dot_general(
                    qs_buf[i, r0:r1, :], k_ref[j * tq + hq:(j + 1) * tq, :], nt_dims,
                    preferred_element_type=F32)

    def softmax(t):
        i, j = pairs[t]
        which = 1 if j == i else 0
        shift = slope * float(-(i - j) * tq)
        s_buf, p_buf, a_buf = s_bufs[t % 2], p_bufs[t % 2], a_bufs[t % 2]
        for (r0, r1), width in row_sets(i, j):
            for rb in range(r0, r1, SOFTMAX_ROWS):
                rows = slice(rb, rb + SOFTMAX_ROWS)
                sb = s_buf[rows, 0:width] + bias_ref[0, which, rows, 0:width]
                m_new = jnp.broadcast_to(jnp.max(sb, axis=-1, keepdims=True), (SOFTMAX_ROWS, LANES)) + shift
                if j > 0:
                    m_old = m_buf[rows, :]
                    m_new = jnp.maximum(m_old, m_new)
                    a_buf[rows, :] = jnp.exp2(m_old - m_new)
                m_sub = m_new - shift
                p_buf[rows, 0:width] = jnp.exp2(
                    sb - jnp.concatenate([m_sub] * (width // LANES), axis=1)).astype(BF16)
                m_buf[rows, :] = m_new

    def accumulate(t):
        i, j = pairs[t]
        acc = accs[i % 2]
        v_aug = jnp.concatenate([v_ref[j * tq:(j + 1) * tq, :], ones], axis=1)
        for (r0, r1), width in row_sets(i, j):
            pv = jnp.dot(p_bufs[t % 2][r0:r1, 0:width], v_aug[0:width, :], preferred_element_type=F32)
            if j == 0:
                acc[r0:r1, :] = pv
            else:
                alpha = a_bufs[t % 2][r0:r1, :]
                acc[r0:r1, :] = jnp.concatenate([alpha] * (2 * V_DIM // LANES), axis=1) * acc[r0:r1, :] + pv
        if j == i:
            o = (acc[0:tq, 0:V_DIM] / acc[0:tq, V_DIM:V_DIM + 1]
                 - lam * (acc[tq:2 * tq, 0:V_DIM] / acc[tq:2 * tq, V_DIM:V_DIM + 1]))
            ms = jnp.mean(o * o, axis=-1, keepdims=True)
            o = ((o * lax.rsqrt(ms + EPS)) * sg_ref[...]) * (1.0 - LAMBDA_INIT)
            o_ref[i * tq:(i + 1) * tq, :] = o.astype(o_ref.dtype)

    scores(0)
    for t in range(len(pairs)):
        if t + 1 < len(pairs):
            scores(t + 1)
        softmax(t)
        if t >= 1:
            accumulate(t - 1)
    accumulate(len(pairs) - 1)


def _alibi_tables(tq):
    slopes = np.exp2(-8.0 * np.arange(1, N_HEADS + 1, dtype=np.float64) / N_HEADS)
    qi = np.arange(tq)[:, None]
    kj = np.arange(tq)[None, :]
    off = -(slopes[:, None, None] * (qi - kj)[None])
    allowed = (kj // CHUNK) <= (qi // CHUNK)
    diag = np.where(allowed[None], -(slopes[:, None, None] * np.abs(qi - kj)[None]), NEG_BIG)
    tab = np.stack([off, diag], axis=1)
    tab = np.concatenate([tab, tab], axis=2)
    return jnp.asarray(tab * LOG2E, F32), jnp.asarray(slopes * LOG2E, F32)


def _attention(proj, b, s, lam_params, sub_g):
    tq = min(TQ, s)
    nq = s // tq
    bias, slopes = _alibi_tables(tq)
    qcol = 0
    kcol = qcol + ATT_WIDTH // LANES
    vcol = kcol + ATT_WIDTH // LANES
    score_buf = pltpu.VMEM((2 * tq, tq), F32)
    prob_buf = pltpu.VMEM((2 * tq, tq), BF16)
    col_buf = pltpu.VMEM((2 * tq, LANES), F32)
    acc_buf = pltpu.VMEM((2 * tq, 2 * V_DIM), F32)
    grid_spec = pltpu.PrefetchScalarGridSpec(
        num_scalar_prefetch=1,
        grid=(b, N_HEADS),
        in_specs=[
            pl.BlockSpec((s, LANES), lambda bi, h, sl: (bi, qcol + h)),
            pl.BlockSpec((s, LANES), lambda bi, h, sl: (bi, kcol + h)),
            pl.BlockSpec((s, LANES), lambda bi, h, sl: (bi, vcol + h)),
            pl.BlockSpec((1, 2, 2 * tq, tq), lambda bi, h, sl: (h, 0, 0, 0)),
            pl.BlockSpec((4, HEAD_DIM), lambda bi, h, sl: (0, 0)),
            pl.BlockSpec((1, V_DIM), lambda bi, h, sl: (0, 0)),
        ],
        out_specs=pl.BlockSpec((s, V_DIM), lambda bi, h, sl: (bi, h)),
        scratch_shapes=[
            pltpu.VMEM((nq, 2 * tq, LANES), BF16),
            score_buf, score_buf, prob_buf, prob_buf, col_buf, col_buf, col_buf, acc_buf, acc_buf,
        ],
    )
    return pl.pallas_call(
        functools.partial(_attn_body, tq=tq, nq=nq),
        grid_spec=grid_spec,
        out_shape=jax.ShapeDtypeStruct((b * s, ATT_WIDTH), BF16),
        compiler_params=_cparams(2),
        name="diff_attn",
    )(slopes, proj, proj, proj, bias, lam_params, sub_g)


def _first_max4(v0, v1, v2, v3):
    m = jnp.maximum(jnp.maximum(v0, v1), jnp.maximum(v2, v3))
    idx = jnp.where(v0 == m, 0, jnp.where(v1 == m, 1, jnp.where(v2 == m, 2, 3))).astype(I32)
    return m, idx


def _route_body(yl_ref, ya_ref, x_ref, wo_ref, g2_ref, wr2_ref, br_ref,
                h1_ref, xl_ref, aux_ref, gt_ref, wo_bf):
    @pl.when(pl.program_id(0) == 0)
    def _():
        wo_bf[...] = wo_ref[...].astype(BF16)

    tm = x_ref.shape[0]
    mix = (jnp.dot(yl_ref[...], wo_bf[0:LRU_WIDTH, :], preferred_element_type=F32)
           + jnp.dot(ya_ref[...], wo_bf[LRU_WIDTH:, :], preferred_element_type=F32))
    h1 = x_ref[...] + mix
    h1_ref[...] = h1
    ms = jnp.mean(h1 * h1, axis=-1, keepdims=True)
    hn = (h1 * lax.rsqrt(ms + EPS)) * g2_ref[...]
    hn_hi = hn.astype(BF16)
    hn_lo = (hn - hn_hi.astype(F32)).astype(BF16)
    hh_hl = jnp.dot(hn_hi, wr2_ref[...], preferred_element_type=F32)
    logits = (hh_hl[:, 0:LANES] + hh_hl[:, LANES:2 * LANES]
              + jnp.dot(hn_lo, wr2_ref[:, 0:LANES], preferred_element_type=F32)) + br_ref[...]
    lt = logits.T
    row = lambda n: lt[n:n + 1, :]
    gmax, gidx = _first_max4(row(0), row(1), row(2), row(3))
    zg = (jnp.exp(row(0) - gmax) + jnp.exp(row(1) - gmax)
          + jnp.exp(row(2) - gmax) + jnp.exp(row(3) - gmax))
    g_gate = 1.0 / zg
    base = N_GROUPS
    sel = [jnp.where(gidx == 0, row(base + j),
                     jnp.where(gidx == 1, row(base + 4 + j),
                               jnp.where(gidx == 2, row(base + 8 + j), row(base + 12 + j))))
           for j in range(EXPERTS_PER_GROUP)]
    m1, i1 = _first_max4(*sel)
    ze = sum(jnp.exp(sj - m1) for sj in sel)
    rest = [jnp.where(i1 == j, -jnp.inf, sel[j]) for j in range(EXPERTS_PER_GROUP)]
    m2, i2 = _first_max4(*rest)
    p1 = 1.0 / ze
    p2 = jnp.exp(m2 - m1) / ze
    gate1 = g_gate * (p1 / (p1 + p2))
    gate2 = g_gate * (p2 / (p1 + p2))
    e1 = gidx * EXPERTS_PER_GROUP + i1
    e2 = gidx * EXPERTS_PER_GROUP + i2

    eio = lax.broadcasted_iota(I32, (N_EXPERTS, tm), 0)
    oh1 = eio == e1
    oh2 = eio == e2
    both = (oh1 | oh2).astype(F32)
    cnt = jnp.sum(both, axis=1, keepdims=True)
    grp = jnp.floor((cnt + (ROW_CHUNK - 1)) * (1.0 / ROW_CHUNK)) * ROW_CHUNK
    ti = lax.broadcasted_iota(I32, (tm, tm), 0)
    tj = lax.broadcasted_iota(I32, (tm, tm), 1)
    before = (ti < tj).astype(BF16)
    rank = jnp.dot(both.astype(BF16), before, preferred_element_type=F32)
    start1 = jnp.sum(jnp.where(eio < e1, grp, 0.0), axis=0, keepdims=True)
    start2 = jnp.sum(jnp.where(eio < e2, grp, 0.0), axis=0, keepdims=True)
    slot1 = start1 + jnp.sum(jnp.where(oh1, rank, 0.0), axis=0, keepdims=True)
    slot2 = start2 + jnp.sum(jnp.where(oh2, rank, 0.0), axis=0, keepdims=True)

    loc = xl_ref.shape[0]
    sio = lax.broadcasted_iota(I32, (loc, tm), 0)
    perm = ((sio == slot1.astype(I32)) | (sio == slot2.astype(I32))).astype(BF16)
    xs = jnp.dot(perm, hn_hi, preferred_element_type=F32)
    xl_ref[...] = xs.astype(BF16)

    rio = lax.broadcasted_iota(I32, (LANES, tm), 0)
    aux_t = jnp.where(rio == 0, slot1, jnp.where(rio == 1, slot2,
                      jnp.where(rio == 2, gate1, jnp.where(rio == 3, gate2, 0.0))))
    aux_ref[...] = aux_t.T
    gt_ref[0] = jnp.broadcast_to(grp, (N_EXPERTS, LANES))


def _out_route(y_lru, y_att, x2, w_out, g2, wr2, br):
    t = x2.shape[0]
    tm = min(TM_TOK, t)
    nt = t // tm
    loc = 2 * tm + N_EXPERTS * ROW_CHUNK
    const = lambda shape: pl.BlockSpec(shape, lambda i: (0,) * len(shape))
    return pl.pallas_call(
        _route_body,
        grid=(nt,),
        in_specs=[
            pl.BlockSpec((tm, LRU_WIDTH), lambda i: (i, 0)),
            pl.BlockSpec((tm, ATT_WIDTH), lambda i: (i, 0)),
            pl.BlockSpec((tm, D_MODEL), lambda i: (i, 0)),
            const((D_MODEL, D_MODEL)),
            const((1, D_MODEL)),
            const((D_MODEL, 2 * LANES)),
            const((1, LANES)),
        ],
        out_specs=[
            pl.BlockSpec((tm, D_MODEL), lambda i: (i, 0)),
            pl.BlockSpec((loc, D_MODEL), lambda i: (i, 0)),
            pl.BlockSpec((tm, LANES), lambda i: (i, 0)),
            pl.BlockSpec((1, N_EXPERTS, LANES), lambda i: (i, 0, 0)),
        ],
        out_shape=[
            jax.ShapeDtypeStruct((t, D_MODEL), F32),
            jax.ShapeDtypeStruct((nt * loc, D_MODEL), BF16),
            jax.ShapeDtypeStruct((t, LANES), F32),
            jax.ShapeDtypeStruct((nt, N_EXPERTS, LANES), F32),
        ],
        scratch_shapes=[pltpu.VMEM((D_MODEL, D_MODEL), BF16)],
        compiler_params=_cparams(1),
        name="out_route",
    )(y_lru, y_att, x2, w_out, g2, wr2, br)


CHUNKS_PER_TILE = TM_EXP // ROW_CHUNK


DUMP_CHUNKS = 2 * CHUNKS_PER_TILE


def _expert_body(tile_e_ref, nused_ref, src_ref, dst_ref, tail_row_ref, tail_n_ref,
                 xl_ref, wg_ref, wu_ref, wd_ref, yl_ref,
                 xbuf, ybuf, zbuf, wg_bf, wu_bf, wd_bf, gsem, ssem, zsem):
    j = pl.program_id(0)
    nused = nused_ref[0]
    nt = tail_n_ref.shape[0]
    slot = lax.rem(j, 2)
    dump_row = yl_ref.shape[0] - DUMP_CHUNKS * ROW_CHUNK

    def rows(c):
        if isinstance(c, int):
            return pl.ds(c * ROW_CHUNK, ROW_CHUNK)
        return pl.ds(pl.multiple_of(c * ROW_CHUNK, ROW_CHUNK), ROW_CHUNK)

    def start_gather(t, sl):
        for k in range(CHUNKS_PER_TILE):
            pltpu.make_async_copy(xl_ref.at[rows(src_ref[t * CHUNKS_PER_TILE + k])],
                                  xbuf.at[sl, rows(k)], gsem.at[sl]).start()

    def start_scatter(t, sl):
        for k in range(CHUNKS_PER_TILE):
            pltpu.make_async_copy(ybuf.at[sl, rows(k)],
                                  yl_ref.at[rows(dst_ref[t * CHUNKS_PER_TILE + k])], ssem.at[sl]).start()

    def wait_gather(sl):
        pltpu.make_async_copy(xl_ref.at[pl.ds(0, TM_EXP)], xbuf.at[sl], gsem.at[sl]).wait()

    def wait_scatter(sl):
        pltpu.make_async_copy(ybuf.at[sl], yl_ref.at[pl.ds(0, TM_EXP)], ssem.at[sl]).wait()

    def for_count(n, fn):
        def body(k, c):
            fn(k)
            return c
        lax.fori_loop(0, n, body, 0)

    def zero_fill(i, k):
        return pltpu.make_async_copy(
            zbuf, yl_ref.at[pl.ds(pl.multiple_of(tail_row_ref[i] + k * ROW_CHUNK, ROW_CHUNK), ROW_CHUNK)],
            zsem)

    @pl.when(j == 0)
    def _():
        zbuf[...] = jnp.zeros(zbuf.shape, BF16)
        ybuf[...] = jnp.zeros(ybuf.shape, BF16)
        for_count(nt, lambda i: for_count(tail_n_ref[i], lambda k: zero_fill(i, k).start()))
        start_gather(0, 0)
        for sl in range(2):
            for k in range(CHUNKS_PER_TILE):
                pltpu.make_async_copy(
                    ybuf.at[sl, rows(k)],
                    yl_ref.at[pl.ds(dump_row + (sl * CHUNKS_PER_TILE + k) * ROW_CHUNK, ROW_CHUNK)],
                    ssem.at[sl]).start()

    new_expert = (j == 0) | (tile_e_ref[j] != tile_e_ref[jnp.maximum(j - 1, 0)])

    @pl.when(new_expert & (j < nused))
    def _():
        wg_bf[...] = wg_ref[0].astype(BF16)
        wu_bf[...] = wu_ref[0].astype(BF16)
        wd_bf[...] = wd_ref[0].astype(BF16)

    @pl.when(j < nused)
    def _():
        wait_gather(slot)
        wait_scatter(slot)
        start_gather(jnp.minimum(j + 1, nused - 1), 1 - slot)
        xb = xbuf[slot]
        y = None
        for c in range(2):
            cols = slice(c * (D_EXPERT // 2), (c + 1) * (D_EXPERT // 2))
            gate = jnp.dot(xb, wg_bf[:, cols], preferred_element_type=F32)
            up = jnp.dot(xb, wu_bf[:, cols], preferred_element_type=F32)
            hid = (jax.nn.silu(gate) * up).astype(BF16)
            part = jnp.dot(hid, wd_bf[cols, :], preferred_element_type=F32)
            y = part if y is None else y + part
        ybuf[slot] = y.astype(BF16)
        start_scatter(j, slot)

    @pl.when(j == nused - 1)
    def _():
        wait_gather(1 - slot)
        wait_scatter(1 - slot)
        wait_scatter(slot)
        for_count(nt, lambda i: for_count(tail_n_ref[i], lambda k: zero_fill(i, 0).wait()))


def _experts(xl, tables, w_gate, w_up, w_down, n_tiles):
    tile_e, nused, src_map, dst_map, tail_row, tail_n = tables
    wmap = lambda j, te, *_: (te[j], 0, 0)
    grid_spec = pltpu.PrefetchScalarGridSpec(
        num_scalar_prefetch=6,
        grid=(n_tiles,),
        in_specs=[
            pl.BlockSpec(memory_space=pl.ANY),
            pl.BlockSpec((1, D_MODEL, D_EXPERT), wmap),
            pl.BlockSpec((1, D_MODEL, D_EXPERT), wmap),
            pl.BlockSpec((1, D_EXPERT, D_MODEL), wmap),
        ],
        out_specs=pl.BlockSpec(memory_space=pl.ANY),
        scratch_shapes=[
            pltpu.VMEM((2, TM_EXP, D_MODEL), BF16),
            pltpu.VMEM((2, TM_EXP, D_MODEL), BF16),
            pltpu.VMEM((ROW_CHUNK, D_MODEL), BF16),
            pltpu.VMEM((D_MODEL, D_EXPERT), BF16),
            pltpu.VMEM((D_MODEL, D_EXPERT), BF16),
            pltpu.VMEM((D_EXPERT, D_MODEL), BF16),
            pltpu.SemaphoreType.DMA((2,)),
            pltpu.SemaphoreType.DMA((2,)),
            pltpu.SemaphoreType.DMA,
        ],
    )
    return pl.pallas_call(
        _expert_body,
        grid_spec=grid_spec,
        out_shape=jax.ShapeDtypeStruct((xl.shape[0] + DUMP_CHUNKS * ROW_CHUNK, D_MODEL), BF16),
        compiler_params=_cparams(1),
        name="experts",
    )(tile_e, nused, src_map, dst_map, tail_row, tail_n, xl, w_gate, w_up, w_down)


def _combine_body(h1_ref, aux_ref, yl_ref, o_ref):
    tm = h1_ref.shape[0]
    loc = yl_ref.shape[0]
    aux = aux_ref[...]
    slot1 = aux[:, 0:1].astype(I32)
    slot2 = aux[:, 1:2].astype(I32)
    sio = lax.broadcasted_iota(I32, (tm, loc), 1)
    gm = jnp.where(sio == slot1, aux[:, 2:3], 0.0) + jnp.where(sio == slot2, aux[:, 3:4], 0.0)
    o_ref[...] = h1_ref[...] + jnp.dot(gm.astype(BF16), yl_ref[...], preferred_element_type=F32)


def _combine(h1, aux, yl, nt, tm, loc):
    return pl.pallas_call(
        _combine_body,
        grid=(nt,),
        in_specs=[
            pl.BlockSpec((tm, D_MODEL), lambda i: (i, 0)),
            pl.BlockSpec((tm, LANES), lambda i: (i, 0)),
            pl.BlockSpec((loc, D_MODEL), lambda i: (i, 0)),
        ],
        out_specs=pl.BlockSpec((tm, D_MODEL), lambda i: (i, 0)),
        out_shape=jax.ShapeDtypeStruct(h1.shape, F32),
        compiler_params=_cparams(1),
        name="combine",
    )(h1, aux, yl)


def _excl_cumsum(a, axis):
    return jnp.cumsum(a, axis=axis) - a


def _expert_tables(grp, n_tiles, loc):
    nt = grp.shape[0]
    gch = grp // ROW_CHUNK
    loc_start = _excl_cumsum(gch, 1)
    used = jnp.sum(gch, axis=1)
    col = jnp.sum(gch, axis=0)
    seg = ((col + CHUNKS_PER_TILE - 1) // CHUNKS_PER_TILE) * CHUNKS_PER_TILE
    seg_end = jnp.cumsum(seg)
    off = seg_end - seg
    tile_first = jnp.arange(n_tiles, dtype=I32) * CHUNKS_PER_TILE
    tile_e = jnp.minimum(jnp.sum(seg_end[None, :] <= tile_first[:, None], axis=1), N_EXPERTS - 1)
    nused = seg_end[-1:] // CHUNKS_PER_TILE
    onehot = tile_e[:, None] == jnp.arange(N_EXPERTS, dtype=I32)[None, :]
    pick = lambda tab: jnp.sum(jnp.where(onehot[:, :, None], tab.T[None], 0), axis=1)
    pick1 = lambda vec: jnp.sum(jnp.where(onehot, vec[None, :], 0), axis=1)
    first = tile_first - pick1(off)
    nvalid = jnp.clip(pick1(col) - first, 0, CHUNKS_PER_TILE)
    cum = jnp.cumsum(gch, axis=0)
    delta = jnp.arange(nt, dtype=I32)[:, None] * (loc // ROW_CHUNK) + loc_start - (cum - gch)
    step = delta - jnp.concatenate([jnp.zeros((1, N_EXPERTS), I32), delta[:-1]], axis=0)
    cum_prev = jnp.concatenate([jnp.full((1, N_EXPERTS), -1, I32), cum[:-1]], axis=0)
    cc = first[:, None] + jnp.arange(CHUNKS_PER_TILE, dtype=I32)[None, :]
    passed = pick(cum_prev)[:, None, :] <= cc[:, :, None]
    chunk_map = cc + jnp.sum(jnp.where(passed, pick(step)[:, None, :], 0), axis=2)
    k_in_tile = jnp.arange(CHUNKS_PER_TILE, dtype=I32)[None, :]
    valid = k_in_tile < nvalid[:, None]
    zero_chunk = loc // ROW_CHUNK - 1
    dump = nt * (loc // ROW_CHUNK) + (jnp.arange(n_tiles, dtype=I32)[:, None] % 2) * CHUNKS_PER_TILE + k_in_tile
    flat = lambda a: a.reshape(n_tiles * CHUNKS_PER_TILE)
    src_map = flat(jnp.where(valid, chunk_map, zero_chunk))
    dst_map = flat(jnp.where(valid, chunk_map, dump))

    tail_row = jnp.arange(nt, dtype=I32) * loc + used * ROW_CHUNK
    tail_n = loc // ROW_CHUNK - used
    as_i32 = lambda a: a.astype(I32)
    return tuple(map(as_i32, (tile_e, nused, src_map, dst_map, tail_row, tail_n)))


def kernel(x, norm1_g, w_in, conv_w, conv_b, w_gate_a, b_gate_a, w_gate_x, b_gate_x, lru_lambda,
           lru_out_g, q_norm_g, k_norm_g, lambda_q1, lambda_k1, lambda_q2, lambda_k2, sub_norm_g,
           w_out, norm2_g, w_router_group, b_router_group, w_router_expert, b_router_expert,
           w_expert_gate, w_expert_up, w_expert_down):
    b, s, d = x.shape
    assert d == D_MODEL and norm1_g.shape[0] == 1
    t = b * s
    l = 0
    x2 = x.reshape(t, d)

    scale = HEAD_DIM ** -0.5 * LOG2E
    qkg = jnp.stack([jnp.tile(q_norm_g[l], 2) * scale, jnp.tile(k_norm_g[l], 2)]).astype(F32)
    eye = jnp.eye(LRU_BLOCKS, dtype=F32)
    blockdiag = lambda w: jnp.einsum("ncd,nm->ncmd", w, eye).reshape(LRU_WIDTH, LRU_WIDTH)
    wgate = jnp.concatenate([blockdiag(w_gate_a[l]), blockdiag(w_gate_x[l])], axis=1).astype(BF16)
    bgate = jnp.concatenate([b_gate_a[l], b_gate_x[l]])[None, :]
    lam_params = jnp.stack([lambda_q1[l], lambda_k1[l], lambda_q2[l], lambda_k2[l]])
    wr = jnp.concatenate(
        [w_router_group[l], jnp.transpose(w_router_expert[l], (1, 0, 2)).reshape(d, N_EXPERTS)], axis=1)
    wr = jnp.pad(wr, ((0, 0), (0, LANES - wr.shape[1])))
    wr_hi = wr.astype(BF16)
    wr2 = jnp.concatenate([wr_hi, (wr - wr_hi.astype(F32)).astype(BF16)], axis=1)
    br = jnp.pad(jnp.concatenate([b_router_group[l], b_router_expert[l].reshape(-1)]),
                 (0, LANES - N_GROUPS - N_EXPERTS))[None, :]

    proj, y_lru = _in_proj_lru(x2, s, norm1_g[l][None, :], w_in[l], qkg, conv_w[l], conv_b[l][None, :],
                               wgate, bgate, lru_lambda[l][None, :], lru_out_g[l][None, :])
    y_att = _attention(proj, b, s, lam_params, sub_norm_g[l][None, :])
    h1, xl, aux, gt = _out_route(y_lru, y_att, x2, w_out[l], norm2_g[l][None, :],
                                 wr2, br)

    tm = min(TM_TOK, t)
    nt = t // tm
    loc = 2 * tm + N_EXPERTS * ROW_CHUNK
    max_rows = 2 * t + nt * N_EXPERTS * (ROW_CHUNK - 1) + N_EXPERTS * (TM_EXP - ROW_CHUNK)
    n_tiles = -(-max_rows // TM_EXP)
    grp = gt[:, :, 0].astype(I32)
    tables = _expert_tables(grp, n_tiles, loc)
    yl = _experts(xl, tables, w_expert_gate[l], w_expert_up[l], w_expert_down[l], n_tiles)
    out = _combine(h1, aux, yl, nt, tm, loc)
    return out.reshape(b, s, d)
```

```python
import functools
import math

import numpy as np
import jax
import jax.numpy as jnp
from jax import lax
from jax.experimental import pallas as pl
from jax.experimental.pallas import tpu as pltpu

F32 = jnp.float32
BF16 = jnp.bfloat16
I32 = jnp.int32

D_MODEL = 1024
LRU_WIDTH = 512
LRU_BLOCKS = 8
LRU_BLOCK_W = LRU_WIDTH // LRU_BLOCKS
CONV_W = 4
LRU_C = 8.0
ATT_WIDTH = 512
N_HEADS = 4
HEAD_DIM = 64
V_DIM = 128
IN_COLS = 2 * LRU_WIDTH + 3 * ATT_WIDTH
N_GROUPS = 4
EXPERTS_PER_GROUP = 4
N_EXPERTS = N_GROUPS * EXPERTS_PER_GROUP
D_EXPERT = D_MODEL // 2
CHUNK = 64
EPS = 1e-6
NEG_BIG = -1e30
LAMBDA_INIT = 0.8 - 0.6 * math.exp(-0.3 * 0)
LOG2E = math.log2(math.e)

LANES = 128
SUBLANES = 8

TM_PROJ = 512
TQ = 512
TM_TOK = 256
TM_EXP = 512
ROW_CHUNK = 2 * SUBLANES
VMEM_LIMIT = 56 * 1024 * 1024


def _cparams(n_axes):
    return pltpu.CompilerParams(
        dimension_semantics=("arbitrary",) * n_axes, vmem_limit_bytes=VMEM_LIMIT)


QKV_COLS = 3 * ATT_WIDTH


def _project(x_ref, g1_ref, qkg_ref, w_bf, lru_cur, qkv_ref):
    x = x_ref[...]
    ms = jnp.mean(x * x, axis=-1, keepdims=True)
    hn = ((x * lax.rsqrt(ms + EPS)) * g1_ref[...]).astype(BF16)
    tm = x.shape[0]
    lo_half = lax.broadcasted_iota(I32, (tm, LANES), 1) < HEAD_DIM
    width = 512
    for c in range(IN_COLS // width):
        c0 = c * width
        acc = jnp.dot(hn, w_bf[:, c0:c0 + width], preferred_element_type=F32)
        if c < 2:
            lru_cur[:, c0:c0 + width] = acc.astype(BF16)
        elif c < 4:
            gain = qkg_ref[c - 2:c - 1, :]
            o0 = c0 - 2 * LRU_WIDTH
            for b in range(width // LANES):
                blk = acc[:, b * LANES:(b + 1) * LANES]
                sq = blk * blk
                s_lo = jnp.sum(jnp.where(lo_half, sq, 0.0), axis=-1, keepdims=True)
                s_hi = jnp.sum(jnp.where(lo_half, 0.0, sq), axis=-1, keepdims=True)
                inv = jnp.where(lo_half,
                                lax.rsqrt(s_lo * (1.0 / HEAD_DIM) + EPS),
                                lax.rsqrt(s_hi * (1.0 / HEAD_DIM) + EPS))
                qkv_ref[:, o0 + b * LANES:o0 + (b + 1) * LANES] = ((blk * inv) * gain).astype(BF16)
        else:
            qkv_ref[:, c0 - 2 * LRU_WIDTH:c0 - 2 * LRU_WIDTH + width] = acc.astype(BF16)


CONV_BLOCK = 128


def _lru_tile(first, p_ref, shift_ref, cw_ref, cb_ref, wg_ref, bg_ref, lam_ref, og_ref, o_ref,
              xbuf, hbuf, hc):
    ts = p_ref.shape[0]
    x_bf = p_ref[:, 0:LRU_WIDTH]
    gl = p_ref[:, LRU_WIDTH:2 * LRU_WIDTH].astype(F32)
    xbuf[0:CONV_BLOCK, :] = jnp.where(first, jnp.zeros((), BF16), xbuf[ts:ts + CONV_BLOCK, :])
    xbuf[CONV_BLOCK:CONV_BLOCK + ts, :] = x_bf
    taps = []
    for j in range(CONV_W - 1):
        shifted = jnp.concatenate(
            [jnp.dot(shift_ref[j], xbuf[r0:r0 + 2 * CONV_BLOCK, :], preferred_element_type=F32)
             for r0 in range(0, ts, CONV_BLOCK)], axis=0)
        taps.append(shifted * cw_ref[j:j + 1, :])
    xc = ((cb_ref[...] + taps[0]) + taps[1]) + taps[2] + x_bf.astype(F32) * cw_ref[CONV_W - 1:CONV_W, :]

    z = jnp.dot(xc.astype(BF16), wg_ref[...], preferred_element_type=F32) + bg_ref[...]
    r = 0.5 * jnp.tanh(0.5 * z[:, 0:LRU_WIDTH]) + 0.5
    gi = 0.5 * jnp.tanh(0.5 * z[:, LRU_WIDTH:2 * LRU_WIDTH]) + 0.5
    nl = -lam_ref[...]
    softplus = jnp.maximum(nl, 0.0) + jnp.log1p(jnp.exp(-jnp.abs(nl)))
    log_a = (-LRU_C) * r * softplus
    a = jnp.exp(log_a)
    v = -jnp.tanh(log_a) * (a * a + 1.0)
    u = jnp.where(v > 0.0, v * lax.rsqrt(v), 0.0) * (gi * xc)

    row = lax.broadcasted_iota(I32, (ts, LRU_WIDTH), 0) & (SUBLANES - 1)
    ca, cb = a, u
    for d in (1, 2, 4):
        a_sh = pltpu.roll(ca, d, axis=0)
        b_sh = pltpu.roll(cb, d, axis=0)
        take = row >= d
        cb = jnp.where(take, ca * b_sh + cb, cb)
        ca = jnp.where(take, ca * a_sh, ca)
    h = jnp.where(first, 0.0, hc[...])
    for blk in range(ts // SUBLANES):
        r0 = blk * SUBLANES
        hb = ca[r0:r0 + SUBLANES, :] * h + cb[r0:r0 + SUBLANES, :]
        hbuf[r0:r0 + SUBLANES, :] = hb
        h = hb[SUBLANES - 1:SUBLANES, :]
    hc[...] = h

    y = hbuf[...] * jax.nn.gelu(gl)
    ms = jnp.mean(y * y, axis=-1, keepdims=True)
    o_ref[...] = ((y * lax.rsqrt(ms + EPS)) * og_ref[...]).astype(o_ref.dtype)


def _inproj_lru_body(x_ref, g1_ref, w_ref, qkg_ref, shift_ref, cw_ref, cb_ref, wg_ref, bg_ref, lam_ref,
                     og_ref, qkv_ref, ylru_ref, w_bf, lru_cur, lru_prev, xbuf, hbuf, hc, *,
                     tiles_per_seq):
    i = pl.program_id(0)

    @pl.when(i == 0)
    def _():
        w_bf[...] = w_ref[...].astype(BF16)
        lru_prev[...] = jnp.zeros(lru_prev.shape, BF16)
        xbuf[...] = jnp.zeros(xbuf.shape, BF16)
        hc[...] = jnp.zeros(hc.shape, F32)

    first = lax.rem(i + (tiles_per_seq - 1), tiles_per_seq) == 0
    _lru_tile(first, lru_prev, shift_ref, cw_ref, cb_ref, wg_ref, bg_ref, lam_ref, og_ref, ylru_ref,
              xbuf, hbuf, hc)
    _project(x_ref, g1_ref, qkg_ref, w_bf, lru_cur, qkv_ref)
    lru_prev[...] = lru_cur[...]


def _in_proj_lru(x2, seq, g1, w_in, qkg, conv_w, conv_b, wgate, bgate, lam, out_g):
    t = x2.shape[0]
    tm = min(TM_PROJ, seq)
    n = t // tm
    const = lambda shape: pl.BlockSpec(shape, lambda i: (0,) * len(shape))
    t_idx = np.arange(CONV_BLOCK)[:, None]
    c_idx = np.arange(2 * CONV_BLOCK)[None, :]
    shift = jnp.asarray(np.stack([c_idx == CONV_BLOCK + t_idx - (CONV_W - 1) + j
                                  for j in range(CONV_W - 1)]), BF16)
    return pl.pallas_call(
        functools.partial(_inproj_lru_body, tiles_per_seq=seq // tm),
        grid=(n + 1,),
        in_specs=[
            pl.BlockSpec((tm, D_MODEL), lambda i: (jnp.minimum(i, n - 1), 0)),
            const((1, D_MODEL)),
            const((D_MODEL, IN_COLS)),
            const((2, LANES)),
            const((CONV_W - 1, CONV_BLOCK, 2 * CONV_BLOCK)),
            const((CONV_W, LRU_WIDTH)),
            const((1, LRU_WIDTH)),
            const((LRU_WIDTH, 2 * LRU_WIDTH)),
            const((1, 2 * LRU_WIDTH)),
            const((1, LRU_WIDTH)),
            const((1, LRU_WIDTH)),
        ],
        out_specs=[
            pl.BlockSpec((tm, QKV_COLS), lambda i: (jnp.minimum(i, n - 1), 0)),
            pl.BlockSpec((tm, LRU_WIDTH), lambda i: (jnp.maximum(i - 1, 0), 0)),
        ],
        out_shape=[
            jax.ShapeDtypeStruct((t, QKV_COLS), BF16),
            jax.ShapeDtypeStruct((t, LRU_WIDTH), BF16),
        ],
        scratch_shapes=[
            pltpu.VMEM((D_MODEL, IN_COLS), BF16),
            pltpu.VMEM((tm, 2 * LRU_WIDTH), BF16),
            pltpu.VMEM((tm, 2 * LRU_WIDTH), BF16),
            pltpu.VMEM((tm + CONV_BLOCK, LRU_WIDTH), BF16),
            pltpu.VMEM((tm, LRU_WIDTH), F32),
            pltpu.VMEM((1, LRU_WIDTH), F32),
        ],
        compiler_params=_cparams(1),
        name="in_proj_lru",
    )(x2, g1, w_in, qkg, shift, conv_w, conv_b, wgate, bgate, lam, out_g)


SOFTMAX_ROWS = 32


def _attn_body(slope_ref, q_ref, k_ref, v_ref, bias_ref, lamp_ref, sg_ref, o_ref,
               qs_buf, s0, s1, p0, p1, a0, a1, m_buf, acc0, acc1, *, tq, nq):
    h = pl.program_id(1)
    slope = slope_ref[h]
    s_bufs, p_bufs, a_bufs, accs = (s0, s1), (p0, p1), (a0, a1), (acc0, acc1)
    lo_half = lax.broadcasted_iota(I32, (tq, LANES), 1) < HEAD_DIM
    for i in range(nq):
        q = q_ref[i * tq:(i + 1) * tq, :]
        zero = jnp.zeros_like(q)
        qs_buf[i, 0:tq, :] = jnp.where(lo_half, q, zero)
        qs_buf[i, tq:2 * tq, :] = jnp.where(lo_half, zero, q)
    ones = jnp.ones((tq, V_DIM), BF16)
    lp = lamp_ref[...]
    lam = (jnp.exp(jnp.sum(lp[0:1, :] * lp[1:2, :], axis=-1, keepdims=True))
           - jnp.exp(jnp.sum(lp[2:3, :] * lp[3:4, :], axis=-1, keepdims=True))
           + LAMBDA_INIT)
    pairs = [(i, j) for i in range(nq) for j in range(i + 1)]
    hq = tq // 2

    def row_sets(i, j):
        if j < i or hq % CHUNK:
            return [((0, 2 * tq), tq)]
        return [((0, hq), hq), ((hq, tq), tq), ((tq, tq + hq), hq), ((tq + hq, 2 * tq), tq)]

    def scores(t):
        i, j = pairs[t]
        s_buf = s_bufs[t % 2]
        nt_dims = (((1,), (1,)), ((), ()))
        if j < i or hq % CHUNK:
            s_buf[...] = lax.dot_general(qs_buf[i], k_ref[j * tq:(j + 1) * tq, :], nt_dims,
                                         preferred_element_type=F32)
            return
        s_buf[:, 0:hq] = lax.dot_general(qs_buf[i], k_ref[j * tq:j * tq + hq, :], nt_dims,
                                         preferred_element_type=F32)
        for (r0, r1), width in row_sets(i, j):
            if width == tq:
                s_buf[r0:r1, hq:tq] = lax.dot_general(
                    qs_buf[i, r0:r1, :], k_ref[j * tq + hq:(j + 1) * tq, :], nt_dims,
                    preferred_element_type=F32)

    def softmax(t):
        i, j = pairs[t]
        which = 1 if j == i else 0
        shift = slope * float(-(i - j) * tq)
        s_buf, p_buf, a_buf = s_bufs[t % 2], p_bufs[t % 2], a_bufs[t % 2]
        for (r0, r1), width in row_sets(i, j):
            for rb in range(r0, r1, SOFTMAX_ROWS):
                rows = slice(rb, rb + SOFTMAX_ROWS)
                sb = s_buf[rows, 0:width] + bias_ref[0, which, rows, 0:width]
                m_new = jnp.broadcast_to(jnp.max(sb, axis=-1, keepdims=True), (SOFTMAX_ROWS, LANES)) + shift
                if j > 0:
                    m_old = m_buf[rows, :]
                    m_new = jnp.maximum(m_old, m_new)
                    a_buf[rows, :] = jnp.exp2(m_old - m_new)
                m_sub = m_new - shift
                p_buf[rows, 0:width] = jnp.exp2(
                    sb - jnp.concatenate([m_sub] * (width // LANES), axis=1)).astype(BF16)
                m_buf[rows, :] = m_new

    def accumulate(t):
        i, j = pairs[t]
        acc = accs[i % 2]
        v_aug = jnp.concatenate([v_ref[j * tq:(j + 1) * tq, :], ones], axis=1)
        for (r0, r1), width in row_sets(i, j):
            pv = jnp.dot(p_bufs[t % 2][r0:r1, 0:width], v_aug[0:width, :], preferred_element_type=F32)
            if j == 0:
                acc[r0:r1, :] = pv
            else:
                alpha = a_bufs[t % 2][r0:r1, :]
                acc[r0:r1, :] = jnp.concatenate([alpha] * (2 * V_DIM // LANES), axis=1) * acc[r0:r1, :] + pv
        if j == i:
            o = (acc[0:tq, 0:V_DIM] / acc[0:tq, V_DIM:V_DIM + 1]
                 - lam * (acc[tq:2 * tq, 0:V_DIM] / acc[tq:2 * tq, V_DIM:V_DIM + 1]))
            ms = jnp.mean(o * o, axis=-1, keepdims=True)
            o = ((o * lax.rsqrt(ms + EPS)) * sg_ref[...]) * (1.0 - LAMBDA_INIT)
            o_ref[i * tq:(i + 1) * tq, :] = o.astype(o_ref.dtype)

    scores(0)
    for t in range(len(pairs)):
        if t + 1 < len(pairs):
            scores(t + 1)
        softmax(t)
        if t >= 1:
            accumulate(t - 1)
    accumulate(len(pairs) - 1)


def _alibi_tables(tq):
    slopes = np.exp2(-8.0 * np.arange(1, N_HEADS + 1, dtype=np.float64) / N_HEADS)
    qi = np.arange(tq)[:, None]
    kj = np.arange(tq)[None, :]
    off = -(slopes[:, None, None] * (qi - kj)[None])
    allowed = (kj // CHUNK) <= (qi // CHUNK)
    diag = np.where(allowed[None], -(slopes[:, None, None] * np.abs(qi - kj)[None]), NEG_BIG)
    tab = np.stack([off, diag], axis=1)
    tab = np.concatenate([tab, tab], axis=2)
    return jnp.asarray(tab * LOG2E, F32), jnp.asarray(slopes * LOG2E, F32)


def _attention(proj, b, s, lam_params, sub_g):
    tq = min(TQ, s)
    nq = s // tq
    bias, slopes = _alibi_tables(tq)
    qcol = 0
    kcol = qcol + ATT_WIDTH // LANES
    vcol = kcol + ATT_WIDTH // LANES
    score_buf = pltpu.VMEM((2 * tq, tq), F32)
    prob_buf = pltpu.VMEM((2 * tq, tq), BF16)
    col_buf = pltpu.VMEM((2 * tq, LANES), F32)
    acc_buf = pltpu.VMEM((2 * tq, 2 * V_DIM), F32)
    grid_spec = pltpu.PrefetchScalarGridSpec(
        num_scalar_prefetch=1,
        grid=(b, N_HEADS),
        in_specs=[
            pl.BlockSpec((s, LANES), lambda bi, h, sl: (bi, qcol + h)),
            pl.BlockSpec((s, LANES), lambda bi, h, sl: (bi, kcol + h)),
            pl.BlockSpec((s, LANES), lambda bi, h, sl: (bi, vcol + h)),
            pl.BlockSpec((1, 2, 2 * tq, tq), lambda bi, h, sl: (h, 0, 0, 0)),
            pl.BlockSpec((4, HEAD_DIM), lambda bi, h, sl: (0, 0)),
            pl.BlockSpec((1, V_DIM), lambda bi, h, sl: (0, 0)),
        ],
        out_specs=pl.BlockSpec((s, V_DIM), lambda bi, h, sl: (bi, h)),
        scratch_shapes=[
            pltpu.VMEM((nq, 2 * tq, LANES), BF16),
            score_buf, score_buf, prob_buf, prob_buf, col_buf, col_buf, col_buf, acc_buf, acc_buf,
        ],
    )
    return pl.pallas_call(
        functools.partial(_attn_body, tq=tq, nq=nq),
        grid_spec=grid_spec,
        out_shape=jax.ShapeDtypeStruct((b * s, ATT_WIDTH), BF16),
        compiler_params=_cparams(2),
        name="diff_attn",
    )(slopes, proj, proj, proj, bias, lam_params, sub_g)


def _first_max4(v0, v1, v2, v3):
    m = jnp.maximum(jnp.maximum(v0, v1), jnp.maximum(v2, v3))
    idx = jnp.where(v0 == m, 0, jnp.where(v1 == m, 1, jnp.where(v2 == m, 2, 3))).astype(I32)
    return m, idx


def _mix_and_logits(yl_ref, ya_ref, x_ref, wo_bf, g2_ref, wr2_ref, br_ref, h1_ref, hn_cur, lg_cur):
    mix = (jnp.dot(yl_ref[...], wo_bf[0:LRU_WIDTH, :], preferred_element_type=F32)
           + jnp.dot(ya_ref[...], wo_bf[LRU_WIDTH:, :], preferred_element_type=F32))
    h1 = x_ref[...] + mix
    h1_ref[...] = h1
    ms = jnp.mean(h1 * h1, axis=-1, keepdims=True)
    hn = (h1 * lax.rsqrt(ms + EPS)) * g2_ref[...]
    hn_hi = hn.astype(BF16)
    hn_lo = (hn - hn_hi.astype(F32)).astype(BF16)
    hn_cur[...] = hn_hi
    hh_hl = jnp.dot(hn_hi, wr2_ref[...], preferred_element_type=F32)
    lg_cur[...] = (hh_hl[:, 0:LANES] + hh_hl[:, LANES:2 * LANES]
                   + jnp.dot(hn_lo, wr2_ref[:, 0:LANES], preferred_element_type=F32)) + br_ref[...]


def _route_and_sort(hn_ref, lg_ref, xl_ref, aux_ref, gt_ref):
    tm = hn_ref.shape[0]
    hn_hi = hn_ref[...]
    lt = lg_ref[...].T
    row = lambda n: lt[n:n + 1, :]
    gmax, gidx = _first_max4(row(0), row(1), row(2), row(3))
    zg = (jnp.exp(row(0) - gmax) + jnp.exp(row(1) - gmax)
          + jnp.exp(row(2) - gmax) + jnp.exp(row(3) - gmax))
    g_gate = 1.0 / zg
    base = N_GROUPS
    sel = [jnp.where(gidx == 0, row(base + j),
                     jnp.where(gidx == 1, row(base + 4 + j),
                               jnp.where(gidx == 2, row(base + 8 + j), row(base + 12 + j))))
           for j in range(EXPERTS_PER_GROUP)]
    m1, i1 = _first_max4(*sel)
    ze = sum(jnp.exp(sj - m1) for sj in sel)
    rest = [jnp.where(i1 == j, -jnp.inf, sel[j]) for j in range(EXPERTS_PER_GROUP)]
    m2, i2 = _first_max4(*rest)
    p1 = 1.0 / ze
    p2 = jnp.exp(m2 - m1) / ze
    gate1 = g_gate * (p1 / (p1 + p2))
    gate2 = g_gate * (p2 / (p1 + p2))
    e1 = gidx * EXPERTS_PER_GROUP + i1
    e2 = gidx * EXPERTS_PER_GROUP + i2

    eio = lax.broadcasted_iota(I32, (N_EXPERTS, tm), 0)
    oh1 = eio == e1
    oh2 = eio == e2
    both = (oh1 | oh2).astype(F32)
    cnt = jnp.sum(both, axis=1, keepdims=True)
    grp = jnp.floor((cnt + (ROW_CHUNK - 1)) * (1.0 / ROW_CHUNK)) * ROW_CHUNK
    ti = lax.broadcasted_iota(I32, (tm, tm), 0)
    tj = lax.broadcasted_iota(I32, (tm, tm), 1)
    before = (ti < tj).astype(BF16)
    rank = jnp.dot(both.astype(BF16), before, preferred_element_type=F32)
    start1 = jnp.sum(jnp.where(eio < e1, grp, 0.0), axis=0, keepdims=True)
    start2 = jnp.sum(jnp.where(eio < e2, grp, 0.0), axis=0, keepdims=True)
    slot1 = start1 + jnp.sum(jnp.where(oh1, rank, 0.0), axis=0, keepdims=True)
    slot2 = start2 + jnp.sum(jnp.where(oh2, rank, 0.0), axis=0, keepdims=True)

    loc = xl_ref.shape[0]
    sio = lax.broadcasted_iota(I32, (loc, tm), 0)
    perm = ((sio == slot1.astype(I32)) | (sio == slot2.astype(I32))).astype(BF16)
    xs = jnp.dot(perm, hn_hi, preferred_element_type=F32)
    xl_ref[...] = xs.astype(BF16)

    rio = lax.broadcasted_iota(I32, (LANES, tm), 0)
    aux_t = jnp.where(rio == 0, slot1, jnp.where(rio == 1, slot2,
                      jnp.where(rio == 2, gate1, jnp.where(rio == 3, gate2, 0.0))))
    aux_ref[...] = aux_t.T
    gt_ref[0] = jnp.broadcast_to(grp, (N_EXPERTS, LANES))


def _route_body(yl_ref, ya_ref, x_ref, wo_ref, g2_ref, wr2_ref, br_ref,
                h1_ref, xl_ref, aux_ref, gt_ref, wo_bf, hn_cur, hn_prev, lg_cur, lg_prev):
    @pl.when(pl.program_id(0) == 0)
    def _():
        wo_bf[...] = wo_ref[...].astype(BF16)
        hn_prev[...] = jnp.zeros(hn_prev.shape, BF16)
        lg_prev[...] = jnp.zeros(lg_prev.shape, F32)

    _route_and_sort(hn_prev, lg_prev, xl_ref, aux_ref, gt_ref)
    _mix_and_logits(yl_ref, ya_ref, x_ref, wo_bf, g2_ref, wr2_ref, br_ref, h1_ref, hn_cur, lg_cur)
    hn_prev[...] = hn_cur[...]
    lg_prev[...] = lg_cur[...]


def _out_route(y_lru, y_att, x2, w_out, g2, wr2, br):
    t = x2.shape[0]
    tm = min(TM_TOK, t)
    nt = t // tm
    loc = 2 * tm + N_EXPERTS * ROW_CHUNK
    const = lambda shape: pl.BlockSpec(shape, lambda i: (0,) * len(shape))
    cur = lambda i: (jnp.minimum(i, nt - 1), 0)
    prev = lambda i: (jnp.maximum(i - 1, 0), 0)
    return pl.pallas_call(
        _route_body,
        grid=(nt + 1,),
        in_specs=[
            pl.BlockSpec((tm, LRU_WIDTH), cur),
            pl.BlockSpec((tm, ATT_WIDTH), cur),
            pl.BlockSpec((tm, D_MODEL), cur),
            const((D_MODEL, D_MODEL)),
            const((1, D_MODEL)),
            const((D_MODEL, 2 * LANES)),
            const((1, LANES)),
        ],
        out_specs=[
            pl.BlockSpec((tm, D_MODEL), cur),
            pl.BlockSpec((loc, D_MODEL), prev),
            pl.BlockSpec((tm, LANES), prev),
            pl.BlockSpec((1, N_EXPERTS, LANES), lambda i: (jnp.maximum(i - 1, 0), 0, 0)),
        ],
        out_shape=[
            jax.ShapeDtypeStruct((t, D_MODEL), F32),
            jax.ShapeDtypeStruct((nt * loc, D_MODEL), BF16),
            jax.ShapeDtypeStruct((t, LANES), F32),
            jax.ShapeDtypeStruct((nt, N_EXPERTS, LANES), F32),
        ],
        scratch_shapes=[
            pltpu.VMEM((D_MODEL, D_MODEL), BF16),
            pltpu.VMEM((tm, D_MODEL), BF16),
            pltpu.VMEM((tm, D_MODEL), BF16),
            pltpu.VMEM((tm, LANES), F32),
            pltpu.VMEM((tm, LANES), F32),
        ],
        compiler_params=_cparams(1),
        name="out_route",
    )(y_lru, y_att, x2, w_out, g2, wr2, br)


CHUNKS_PER_TILE = TM_EXP // ROW_CHUNK


DUMP_CHUNKS = 2 * CHUNKS_PER_TILE


def _expert_body(tile_e_ref, nused_ref, src_ref, dst_ref, tail_row_ref, tail_n_ref,
                 xl_ref, wg_ref, wu_ref, wd_ref, yl_ref,
                 xbuf, ybuf, zbuf, wg_bf, wu_bf, wd_bf, gsem, ssem, zsem):
    j = pl.program_id(0)
    nused = nused_ref[0]
    nt = tail_n_ref.shape[0]
    slot = lax.rem(j, 2)
    dump_row = yl_ref.shape[0] - DUMP_CHUNKS * ROW_CHUNK

    def rows(c):
        if isinstance(c, int):
            return pl.ds(c * ROW_CHUNK, ROW_CHUNK)
        return pl.ds(pl.multiple_of(c * ROW_CHUNK, ROW_CHUNK), ROW_CHUNK)

    def start_gather(t, sl):
        for k in range(CHUNKS_PER_TILE):
            pltpu.make_async_copy(xl_ref.at[rows(src_ref[t * CHUNKS_PER_TILE + k])],
                                  xbuf.at[sl, rows(k)], gsem.at[sl]).start()

    def start_scatter(t, sl):
        for k in range(CHUNKS_PER_TILE):
            pltpu.make_async_copy(ybuf.at[sl, rows(k)],
                                  yl_ref.at[rows(dst_ref[t * CHUNKS_PER_TILE + k])], ssem.at[sl]).start()

    def wait_gather(sl):
        pltpu.make_async_copy(xl_ref.at[pl.ds(0, TM_EXP)], xbuf.at[sl], gsem.at[sl]).wait()

    def wait_scatter(sl):
        pltpu.make_async_copy(ybuf.at[sl], yl_ref.at[pl.ds(0, TM_EXP)], ssem.at[sl]).wait()

    def for_count(n, fn):
        def body(k, c):
            fn(k)
            return c
        lax.fori_loop(0, n, body, 0)

    def zero_fill(i, k):
        return pltpu.make_async_copy(
            zbuf, yl_ref.at[pl.ds(pl.multiple_of(tail_row_ref[i] + k * ROW_CHUNK, ROW_CHUNK), ROW_CHUNK)],
            zsem)

    @pl.when(j == 0)
    def _():
        zbuf[...] = jnp.zeros(zbuf.shape, BF16)
        ybuf[...] = jnp.zeros(ybuf.shape, BF16)
        for_count(nt, lambda i: for_count(tail_n_ref[i], lambda k: zero_fill(i, k).start()))
        start_gather(0, 0)
        for sl in range(2):
            for k in range(CHUNKS_PER_TILE):
                pltpu.make_async_copy(
                    ybuf.at[sl, rows(k)],
                    yl_ref.at[pl.ds(dump_row + (sl * CHUNKS_PER_TILE + k) * ROW_CHUNK, ROW_CHUNK)],
                    ssem.at[sl]).start()

    new_expert = (j == 0) | (tile_e_ref[j] != tile_e_ref[jnp.maximum(j - 1, 0)])

    @pl.when(new_expert & (j < nused))
    def _():
        wg_bf[...] = wg_ref[0].astype(BF16)
        wu_bf[...] = wu_ref[0].astype(BF16)
        wd_bf[...] = wd_ref[0].astype(BF16)

    @pl.when(j < nused)
    def _():
        wait_gather(slot)
        wait_scatter(slot)
        start_gather(jnp.minimum(j + 1, nused - 1), 1 - slot)
        xb = xbuf[slot]
        y = None
        for c in range(2):
            cols = slice(c * (D_EXPERT // 2), (c + 1) * (D_EXPERT // 2))
            gate = jnp.dot(xb, wg_bf[:, cols], preferred_element_type=F32)
            up = jnp.dot(xb, wu_bf[:, cols], preferred_element_type=F32)
            hid = (jax.nn.silu(gate) * up).astype(BF16)
            part = jnp.dot(hid, wd_bf[cols, :], preferred_element_type=F32)
            y = part if y is None else y + part
        ybuf[slot] = y.astype(BF16)
        start_scatter(j, slot)

    @pl.when(j == nused - 1)
    def _():
        wait_gather(1 - slot)
        wait_scatter(1 - slot)
        wait_scatter(slot)
        for_count(nt, lambda i: for_count(tail_n_ref[i], lambda k: zero_fill(i, 0).wait()))


def _experts(xl, tables, w_gate, w_up, w_down, n_tiles):
    tile_e, nused, src_map, dst_map, tail_row, tail_n = tables
    wmap = lambda j, te, *_: (te[j], 0, 0)
    grid_spec = pltpu.PrefetchScalarGridSpec(
        num_scalar_prefetch=6,
        grid=(n_tiles,),
        in_specs=[
            pl.BlockSpec(memory_space=pl.ANY),
            pl.BlockSpec((1, D_MODEL, D_EXPERT), wmap),
            pl.BlockSpec((1, D_MODEL, D_EXPERT), wmap),
            pl.BlockSpec((1, D_EXPERT, D_MODEL), wmap),
        ],
        out_specs=pl.BlockSpec(memory_space=pl.ANY),
        scratch_shapes=[
            pltpu.VMEM((2, TM_EXP, D_MODEL), BF16),
            pltpu.VMEM((2, TM_EXP, D_MODEL), BF16),
            pltpu.VMEM((ROW_CHUNK, D_MODEL), BF16),
            pltpu.VMEM((D_MODEL, D_EXPERT), BF16),
            pltpu.VMEM((D_MODEL, D_EXPERT), BF16),
            pltpu.VMEM((D_EXPERT, D_MODEL), BF16),
            pltpu.SemaphoreType.DMA((2,)),
            pltpu.SemaphoreType.DMA((2,)),
            pltpu.SemaphoreType.DMA,
        ],
    )
    return pl.pallas_call(
        _expert_body,
        grid_spec=grid_spec,
        out_shape=jax.ShapeDtypeStruct((xl.shape[0] + DUMP_CHUNKS * ROW_CHUNK, D_MODEL), BF16),
        compiler_params=_cparams(1),
        name="experts",
    )(tile_e, nused, src_map, dst_map, tail_row, tail_n, xl, w_gate, w_up, w_down)


def _combine_body(h1_ref, aux_ref, yl_ref, o_ref):
    tm = h1_ref.shape[0]
    loc = yl_ref.shape[0]
    aux = aux_ref[...]
    slot1 = aux[:, 0:1].astype(I32)
    slot2 = aux[:, 1:2].astype(I32)
    sio = lax.broadcasted_iota(I32, (tm, loc), 1)
    gm = jnp.where(sio == slot1, aux[:, 2:3], 0.0) + jnp.where(sio == slot2, aux[:, 3:4], 0.0)
    o_ref[...] = h1_ref[...] + jnp.dot(gm.astype(BF16), yl_ref[...], preferred_element_type=F32)


def _combine(h1, aux, yl, nt, tm, loc):
    return pl.pallas_call(
        _combine_body,
        grid=(nt,),
        in_specs=[
            pl.BlockSpec((tm, D_MODEL), lambda i: (i, 0)),
            pl.BlockSpec((tm, LANES), lambda i: (i, 0)),
            pl.BlockSpec((loc, D_MODEL), lambda i: (i, 0)),
        ],
        out_specs=pl.BlockSpec((tm, D_MODEL), lambda i: (i, 0)),
        out_shape=jax.ShapeDtypeStruct(h1.shape, F32),
        compiler_params=_cparams(1),
        name="combine",
    )(h1, aux, yl)


def _excl_cumsum(a, axis):
    return jnp.cumsum(a, axis=axis) - a


def _expert_tables(grp, n_tiles, loc):
    nt = grp.shape[0]
    gch = grp // ROW_CHUNK
    loc_start = _excl_cumsum(gch, 1)
    used = jnp.sum(gch, axis=1)
    col = jnp.sum(gch, axis=0)
    seg = ((col + CHUNKS_PER_TILE - 1) // CHUNKS_PER_TILE) * CHUNKS_PER_TILE
    seg_end = jnp.cumsum(seg)
    off = seg_end - seg
    tile_first = jnp.arange(n_tiles, dtype=I32) * CHUNKS_PER_TILE
    tile_e = jnp.minimum(jnp.sum(seg_end[None, :] <= tile_first[:, None], axis=1), N_EXPERTS - 1)
    nused = seg_end[-1:] // CHUNKS_PER_TILE
    onehot = tile_e[:, None] == jnp.arange(N_EXPERTS, dtype=I32)[None, :]
    pick = lambda tab: jnp.sum(jnp.where(onehot[:, :, None], tab.T[None], 0), axis=1)
    pick1 = lambda vec: jnp.sum(jnp.where(onehot, vec[None, :], 0), axis=1)
    first = tile_first - pick1(off)
    nvalid = jnp.clip(pick1(col) - first, 0, CHUNKS_PER_TILE)
    cum = jnp.cumsum(gch, axis=0)
    delta = jnp.arange(nt, dtype=I32)[:, None] * (loc // ROW_CHUNK) + loc_start - (cum - gch)
    step = delta - jnp.concatenate([jnp.zeros((1, N_EXPERTS), I32), delta[:-1]], axis=0)
    cum_prev = jnp.concatenate([jnp.full((1, N_EXPERTS), -1, I32), cum[:-1]], axis=0)
    cc = first[:, None] + jnp.arange(CHUNKS_PER_TILE, dtype=I32)[None, :]
    passed = pick(cum_prev)[:, None, :] <= cc[:, :, None]
    chunk_map = cc + jnp.sum(jnp.where(passed, pick(step)[:, None, :], 0), axis=2)
    k_in_tile = jnp.arange(CHUNKS_PER_TILE, dtype=I32)[None, :]
    valid = k_in_tile < nvalid[:, None]
    zero_chunk = loc // ROW_CHUNK - 1
    dump = nt * (loc // ROW_CHUNK) + (jnp.arange(n_tiles, dtype=I32)[:, None] % 2) * CHUNKS_PER_TILE + k_in_tile
    flat = lambda a: a.reshape(n_tiles * CHUNKS_PER_TILE)
    src_map = flat(jnp.where(valid, chunk_map, zero_chunk))
    dst_map = flat(jnp.where(valid, chunk_map, dump))

    tail_row = jnp.arange(nt, dtype=I32) * loc + used * ROW_CHUNK
    tail_n = loc // ROW_CHUNK - used
    as_i32 = lambda a: a.astype(I32)
    return tuple(map(as_i32, (tile_e, nused, src_map, dst_map, tail_row, tail_n)))


def kernel(x, norm1_g, w_in, conv_w, conv_b, w_gate_a, b_gate_a, w_gate_x, b_gate_x, lru_lambda,
           lru_out_g, q_norm_g, k_norm_g, lambda_q1, lambda_k1, lambda_q2, lambda_k2, sub_norm_g,
           w_out, norm2_g, w_router_group, b_router_group, w_router_expert, b_router_expert,
           w_expert_gate, w_expert_up, w_expert_down):
    b, s, d = x.shape
    assert d == D_MODEL and norm1_g.shape[0] == 1
    t = b * s
    l = 0
    x2 = x.reshape(t, d)

    scale = HEAD_DIM ** -0.5 * LOG2E
    qkg = jnp.stack([jnp.tile(q_norm_g[l], 2) * scale, jnp.tile(k_norm_g[l], 2)]).astype(F32)
    eye = jnp.eye(LRU_BLOCKS, dtype=F32)
    blockdiag = lambda w: jnp.einsum("ncd,nm->ncmd", w, eye).reshape(LRU_WIDTH, LRU_WIDTH)
    wgate = jnp.concatenate([blockdiag(w_gate_a[l]), blockdiag(w_gate_x[l])], axis=1).astype(BF16)
    bgate = jnp.concatenate([b_gate_a[l], b_gate_x[l]])[None, :]
    lam_params = jnp.stack([lambda_q1[l], lambda_k1[l], lambda_q2[l], lambda_k2[l]])
    wr = jnp.concatenate(
        [w_router_group[l], jnp.transpose(w_router_expert[l], (1, 0, 2)).reshape(d, N_EXPERTS)], axis=1)
    wr = jnp.pad(wr, ((0, 0), (0, LANES - wr.shape[1])))
    wr_hi = wr.astype(BF16)
    wr2 = jnp.concatenate([wr_hi, (wr - wr_hi.astype(F32)).astype(BF16)], axis=1)
    br = jnp.pad(jnp.concatenate([b_router_group[l], b_router_expert[l].reshape(-1)]),
                 (0, LANES - N_GROUPS - N_EXPERTS))[None, :]

    proj, y_lru = _in_proj_lru(x2, s, norm1_g[l][None, :], w_in[l], qkg, conv_w[l], conv_b[l][None, :],
                               wgate, bgate, lru_lambda[l][None, :], lru_out_g[l][None, :])
    y_att = _attention(proj, b, s, lam_params, sub_norm_g[l][None, :])
    h1, xl, aux, gt = _out_route(y_lru, y_att, x2, w_out[l], norm2_g[l][None, :],
                                 wr2, br)

    tm = min(TM_TOK, t)
    nt = t // tm
    loc = 2 * tm + N_EXPERTS * ROW_CHUNK
    max_rows = 2 * t + nt * N_EXPERTS * (ROW_CHUNK - 1) + N_EXPERTS * (TM_EXP - ROW_CHUNK)
    n_tiles = -(-max_rows // TM_EXP)
    grp = gt[:, :, 0].astype(I32)
    tables = _expert_tables(grp, n_tiles, loc)
    yl = _experts(xl, tables, w_expert_gate[l], w_expert_up[l], w_expert_down[l], n_tiles)
    out = _combine(h1, aux, yl, nt, tm, loc)
    return out.reshape(b, s, d)
```

```python
import functools
import math

import numpy as np
import jax
import jax.numpy as jnp
from jax import lax
from jax.experimental import pallas as pl
from jax.experimental.pallas import tpu as pltpu

F32 = jnp.float32
BF16 = jnp.bfloat16
I32 = jnp.int32

D_MODEL = 1024
LRU_WIDTH = 512
LRU_BLOCKS = 8
LRU_BLOCK_W = LRU_WIDTH // LRU_BLOCKS
CONV_W = 4
LRU_C = 8.0
ATT_WIDTH = 512
N_HEADS = 4
HEAD_DIM = 64
V_DIM = 128
IN_COLS = 2 * LRU_WIDTH + 3 * ATT_WIDTH
N_GROUPS = 4
EXPERTS_PER_GROUP = 4
N_EXPERTS = N_GROUPS * EXPERTS_PER_GROUP
D_EXPERT = D_MODEL // 2
CHUNK = 64
EPS = 1e-6
NEG_BIG = -1e30
LAMBDA_INIT = 0.8 - 0.6 * math.exp(-0.3 * 0)
LOG2E = math.log2(math.e)

LANES = 128
SUBLANES = 8

TM_PROJ = 512
TQ = 512
TM_TOK = 512
TM_EXP = 512
ROW_CHUNK = 2 * SUBLANES
VMEM_LIMIT = 56 * 1024 * 1024


def _cparams(n_axes):
    return pltpu.CompilerParams(
        dimension_semantics=("arbitrary",) * n_axes, vmem_limit_bytes=VMEM_LIMIT)


QKV_COLS = 3 * ATT_WIDTH


def _project(x_ref, g1_ref, qkg_ref, w_bf, lru_cur, qkv_ref):
    x = x_ref[...]
    ms = jnp.mean(x * x, axis=-1, keepdims=True)
    hn = ((x * lax.rsqrt(ms + EPS)) * g1_ref[...]).astype(BF16)
    tm = x.shape[0]
    lo_half = lax.broadcasted_iota(I32, (tm, LANES), 1) < HEAD_DIM
    width = 512
    for c in range(IN_COLS // width):
        c0 = c * width
        acc = jnp.dot(hn, w_bf[:, c0:c0 + width], preferred_element_type=F32)
        if c < 2:
            lru_cur[:, c0:c0 + width] = acc.astype(BF16)
        elif c < 4:
            gain = qkg_ref[c - 2:c - 1, :]
            o0 = c0 - 2 * LRU_WIDTH
            for b in range(width // LANES):
                blk = acc[:, b * LANES:(b + 1) * LANES]
                sq = blk * blk
                s_lo = jnp.sum(jnp.where(lo_half, sq, 0.0), axis=-1, keepdims=True)
                s_hi = jnp.sum(jnp.where(lo_half, 0.0, sq), axis=-1, keepdims=True)
                inv = jnp.where(lo_half,
                                lax.rsqrt(s_lo * (1.0 / HEAD_DIM) + EPS),
                                lax.rsqrt(s_hi * (1.0 / HEAD_DIM) + EPS))
                qkv_ref[:, o0 + b * LANES:o0 + (b + 1) * LANES] = ((blk * inv) * gain).astype(BF16)
        else:
            qkv_ref[:, c0 - 2 * LRU_WIDTH:c0 - 2 * LRU_WIDTH + width] = acc.astype(BF16)


CONV_BLOCK = 128


def _lru_tile(first, p_ref, shift_ref, cw_ref, cb_ref, wg_ref, bg_ref, lam_ref, og_ref, o_ref,
              xbuf, hbuf, hc):
    ts = p_ref.shape[0]
    x_bf = p_ref[:, 0:LRU_WIDTH]
    gl = p_ref[:, LRU_WIDTH:2 * LRU_WIDTH].astype(F32)
    xbuf[0:CONV_BLOCK, :] = jnp.where(first, jnp.zeros((), BF16), xbuf[ts:ts + CONV_BLOCK, :])
    xbuf[CONV_BLOCK:CONV_BLOCK + ts, :] = x_bf
    taps = []
    for j in range(CONV_W - 1):
        shifted = jnp.concatenate(
            [jnp.dot(shift_ref[j], xbuf[r0:r0 + 2 * CONV_BLOCK, :], preferred_element_type=F32)
             for r0 in range(0, ts, CONV_BLOCK)], axis=0)
        taps.append(shifted * cw_ref[j:j + 1, :])
    xc = ((cb_ref[...] + taps[0]) + taps[1]) + taps[2] + x_bf.astype(F32) * cw_ref[CONV_W - 1:CONV_W, :]

    z = jnp.dot(xc.astype(BF16), wg_ref[...], preferred_element_type=F32) + bg_ref[...]
    r = 0.5 * jnp.tanh(0.5 * z[:, 0:LRU_WIDTH]) + 0.5
    gi = 0.5 * jnp.tanh(0.5 * z[:, LRU_WIDTH:2 * LRU_WIDTH]) + 0.5
    nl = -lam_ref[...]
    softplus = jnp.maximum(nl, 0.0) + jnp.log1p(jnp.exp(-jnp.abs(nl)))
    log_a = (-LRU_C) * r * softplus
    a = jnp.exp(log_a)
    v = -jnp.tanh(log_a) * (a * a + 1.0)
    u = jnp.where(v > 0.0, v * lax.rsqrt(v), 0.0) * (gi * xc)

    row = lax.broadcasted_iota(I32, (ts, LRU_WIDTH), 0) & (SUBLANES - 1)
    ca, cb = a, u
    for d in (1, 2, 4):
        a_sh = pltpu.roll(ca, d, axis=0)
        b_sh = pltpu.roll(cb, d, axis=0)
        take = row >= d
        cb = jnp.where(take, ca * b_sh + cb, cb)
        ca = jnp.where(take, ca * a_sh, ca)
    h = jnp.where(first, 0.0, hc[...])
    for blk in range(ts // SUBLANES):
        r0 = blk * SUBLANES
        hb = ca[r0:r0 + SUBLANES, :] * h + cb[r0:r0 + SUBLANES, :]
        hbuf[r0:r0 + SUBLANES, :] = hb
        h = hb[SUBLANES - 1:SUBLANES, :]
    hc[...] = h

    y = hbuf[...] * jax.nn.gelu(gl)
    ms = jnp.mean(y * y, axis=-1, keepdims=True)
    o_ref[...] = ((y * lax.rsqrt(ms + EPS)) * og_ref[...]).astype(o_ref.dtype)


def _inproj_lru_body(x_ref, g1_ref, w_ref, qkg_ref, shift_ref, cw_ref, cb_ref, wg_ref, bg_ref, lam_ref,
                     og_ref, qkv_ref, ylru_ref, w_bf, lru_cur, lru_prev, xbuf, hbuf, hc, *,
                     tiles_per_seq):
    i = pl.program_id(0)

    @pl.when(i == 0)
    def _():
        w_bf[...] = w_ref[...].astype(BF16)
        lru_prev[...] = jnp.zeros(lru_prev.shape, BF16)
        xbuf[...] = jnp.zeros(xbuf.shape, BF16)
        hc[...] = jnp.zeros(hc.shape, F32)

    first = lax.rem(i + (tiles_per_seq - 1), tiles_per_seq) == 0
    _lru_tile(first, lru_prev, shift_ref, cw_ref, cb_ref, wg_ref, bg_ref, lam_ref, og_ref, ylru_ref,
              xbuf, hbuf, hc)
    _project(x_ref, g1_ref, qkg_ref, w_bf, lru_cur, qkv_ref)
    lru_prev[...] = lru_cur[...]


def _in_proj_lru(x2, seq, g1, w_in, qkg, conv_w, conv_b, wgate, bgate, lam, out_g):
    t = x2.shape[0]
    tm = min(TM_PROJ, seq)
    n = t // tm
    const = lambda shape: pl.BlockSpec(shape, lambda i: (0,) * len(shape))
    t_idx = np.arange(CONV_BLOCK)[:, None]
    c_idx = np.arange(2 * CONV_BLOCK)[None, :]
    shift = jnp.asarray(np.stack([c_idx == CONV_BLOCK + t_idx - (CONV_W - 1) + j
                                  for j in range(CONV_W - 1)]), BF16)
    return pl.pallas_call(
        functools.partial(_inproj_lru_body, tiles_per_seq=seq // tm),
        grid=(n + 1,),
        in_specs=[
            pl.BlockSpec((tm, D_MODEL), lambda i: (jnp.minimum(i, n - 1), 0)),
            const((1, D_MODEL)),
            const((D_MODEL, IN_COLS)),
            const((2, LANES)),
            const((CONV_W - 1, CONV_BLOCK, 2 * CONV_BLOCK)),
            const((CONV_W, LRU_WIDTH)),
            const((1, LRU_WIDTH)),
            const((LRU_WIDTH, 2 * LRU_WIDTH)),
            const((1, 2 * LRU_WIDTH)),
            const((1, LRU_WIDTH)),
            const((1, LRU_WIDTH)),
        ],
        out_specs=[
            pl.BlockSpec((tm, QKV_COLS), lambda i: (jnp.minimum(i, n - 1), 0)),
            pl.BlockSpec((tm, LRU_WIDTH), lambda i: (jnp.maximum(i - 1, 0), 0)),
        ],
        out_shape=[
            jax.ShapeDtypeStruct((t, QKV_COLS), BF16),
            jax.ShapeDtypeStruct((t, LRU_WIDTH), BF16),
        ],
        scratch_shapes=[
            pltpu.VMEM((D_MODEL, IN_COLS), BF16),
            pltpu.VMEM((tm, 2 * LRU_WIDTH), BF16),
            pltpu.VMEM((tm, 2 * LRU_WIDTH), BF16),
            pltpu.VMEM((tm + CONV_BLOCK, LRU_WIDTH), BF16),
            pltpu.VMEM((tm, LRU_WIDTH), F32),
            pltpu.VMEM((1, LRU_WIDTH), F32),
        ],
        compiler_params=_cparams(1),
        name="in_proj_lru",
    )(x2, g1, w_in, qkg, shift, conv_w, conv_b, wgate, bgate, lam, out_g)


SOFTMAX_ROWS = 32


def _attn_body(slope_ref, q_ref, k_ref, v_ref, bias_ref, lamp_ref, sg_ref, o_ref,
               qs_buf, s0, s1, p0, p1, a0, a1, m_buf, acc0, acc1, *, tq, nq):
    h = pl.program_id(1)
    slope = slope_ref[h]
    s_bufs, p_bufs, a_bufs, accs = (s0, s1), (p0, p1), (a0, a1), (acc0, acc1)
    lo_half = lax.broadcasted_iota(I32, (tq, LANES), 1) < HEAD_DIM
    for i in range(nq):
        q = q_ref[i * tq:(i + 1) * tq, :]
        zero = jnp.zeros_like(q)
        qs_buf[i, 0:tq, :] = jnp.where(lo_half, q, zero)
        qs_buf[i, tq:2 * tq, :] = jnp.where(lo_half, zero, q)
    ones = jnp.ones((tq, V_DIM), BF16)
    lp = lamp_ref[...]
    lam = (jnp.exp(jnp.sum(lp[0:1, :] * lp[1:2, :], axis=-1, keepdims=True))
           - jnp.exp(jnp.sum(lp[2:3, :] * lp[3:4, :], axis=-1, keepdims=True))
           + LAMBDA_INIT)
    pairs = [(i, j) for i in range(nq) for j in range(i + 1)]
    hq = tq // 2

    def row_sets(i, j):
        if j < i or hq % CHUNK:
            return [((0, 2 * tq), tq)]
        return [((0, hq), hq), ((hq, tq), tq), ((tq, tq + hq), hq), ((tq + hq, 2 * tq), tq)]

    def scores(t):
        i, j = pairs[t]
        s_buf = s_bufs[t % 2]
        nt_dims = (((1,), (1,)), ((), ()))
        if j < i or hq % CHUNK:
            s_buf[...] = lax.dot_general(qs_buf[i], k_ref[j * tq:(j + 1) * tq, :], nt_dims,
                                         preferred_element_type=F32)
            return
        s_buf[:, 0:hq] = lax.dot_general(qs_buf[i], k_ref[j * tq:j * tq + hq, :], nt_dims,
                                         preferred_element_type=F32)
        for (r0, r1), width in row_sets(i, j):
            if width == tq:
                s_buf[r0:r1, hq:tq] = lax.dot_general(
                    qs_buf[i, r0:r1, :], k_ref[j * tq + hq:(j + 1) * tq, :], nt_dims,
                    preferred_element_type=F32)

    def softmax(t):
        i, j = pairs[t]
        which = 1 if j == i else 0
        shift = slope * float(-(i - j) * tq)
        s_buf, p_buf, a_buf = s_bufs[t % 2], p_bufs[t % 2], a_bufs[t % 2]
        for (r0, r1), width in row_sets(i, j):
            for rb in range(r0, r1, SOFTMAX_ROWS):
                rows = slice(rb, rb + SOFTMAX_ROWS)
                sb = s_buf[rows, 0:width] + bias_ref[0, which, rows, 0:width]
                m_new = jnp.broadcast_to(jnp.max(sb, axis=-1, keepdims=True), (SOFTMAX_ROWS, LANES)) + shift
                if j > 0:
                    m_old = m_buf[rows, :]
                    m_new = jnp.maximum(m_old, m_new)
                    a_buf[rows, :] = jnp.exp2(m_old - m_new)
                m_sub = m_new - shift
                p_buf[rows, 0:width] = jnp.exp2(
                    sb - jnp.concatenate([m_sub] * (width // LANES), axis=1)).astype(BF16)
                m_buf[rows, :] = m_new

    def accumulate(t):
        i, j = pairs[t]
        acc = accs[i % 2]
        v_aug = jnp.concatenate([v_ref[j * tq:(j + 1) * tq, :], ones], axis=1)
        for (r0, r1), width in row_sets(i, j):
            pv = jnp.dot(p_bufs[t % 2][r0:r1, 0:width], v_aug[0:width, :], preferred_element_type=F32)
            if j == 0:
                acc[r0:r1, :] = pv
            else:
                alpha = a_bufs[t % 2][r0:r1, :]
                acc[r0:r1, :] = jnp.concatenate([alpha] * (2 * V_DIM // LANES), axis=1) * acc[r0:r1, :] + pv
        if j == i:
            o = (acc[0:tq, 0:V_DIM] / acc[0:tq, V_DIM:V_DIM + 1]
                 - lam * (acc[tq:2 * tq, 0:V_DIM] / acc[tq:2 * tq, V_DIM:V_DIM + 1]))
            ms = jnp.mean(o * o, axis=-1, keepdims=True)
            o = ((o * lax.rsqrt(ms + EPS)) * sg_ref[...]) * (1.0 - LAMBDA_INIT)
            o_ref[i * tq:(i + 1) * tq, :] = o.astype(o_ref.dtype)

    scores(0)
    for t in range(len(pairs)):
        if t + 1 < len(pairs):
            scores(t + 1)
        softmax(t)
        if t >= 1:
            accumulate(t - 1)
    accumulate(len(pairs) - 1)


def _alibi_tables(tq):
    slopes = np.exp2(-8.0 * np.arange(1, N_HEADS + 1, dtype=np.float64) / N_HEADS)
    qi = np.arange(tq)[:, None]
    kj = np.arange(tq)[None, :]
    off = -(slopes[:, None, None] * (qi - kj)[None])
    allowed = (kj // CHUNK) <= (qi // CHUNK)
    diag = np.where(allowed[None], -(slopes[:, None, None] * np.abs(qi - kj)[None]), NEG_BIG)
    tab = np.stack([off, diag], axis=1)
    tab = np.concatenate([tab, tab], axis=2)
    return jnp.asarray(tab * LOG2E, F32), jnp.asarray(slopes * LOG2E, F32)


def _attention(proj, b, s, lam_params, sub_g):
    tq = min(TQ, s)
    nq = s // tq
    bias, slopes = _alibi_tables(tq)
    qcol = 0
    kcol = qcol + ATT_WIDTH // LANES
    vcol = kcol + ATT_WIDTH // LANES
    score_buf = pltpu.VMEM((2 * tq, tq), F32)
    prob_buf = pltpu.VMEM((2 * tq, tq), BF16)
    col_buf = pltpu.VMEM((2 * tq, LANES), F32)
    acc_buf = pltpu.VMEM((2 * tq, 2 * V_DIM), F32)
    grid_spec = pltpu.PrefetchScalarGridSpec(
        num_scalar_prefetch=1,
        grid=(b, N_HEADS),
        in_specs=[
            pl.BlockSpec((s, LANES), lambda bi, h, sl: (bi, qcol + h)),
            pl.BlockSpec((s, LANES), lambda bi, h, sl: (bi, kcol + h)),
            pl.BlockSpec((s, LANES), lambda bi, h, sl: (bi, vcol + h)),
            pl.BlockSpec((1, 2, 2 * tq, tq), lambda bi, h, sl: (h, 0, 0, 0)),
            pl.BlockSpec((4, HEAD_DIM), lambda bi, h, sl: (0, 0)),
            pl.BlockSpec((1, V_DIM), lambda bi, h, sl: (0, 0)),
        ],
        out_specs=pl.BlockSpec((s, V_DIM), lambda bi, h, sl: (bi, h)),
        scratch_shapes=[
            pltpu.VMEM((nq, 2 * tq, LANES), BF16),
            score_buf, score_buf, prob_buf, prob_buf, col_buf, col_buf, col_buf, acc_buf, acc_buf,
        ],
    )
    return pl.pallas_call(
        functools.partial(_attn_body, tq=tq, nq=nq),
        grid_spec=grid_spec,
        out_shape=jax.ShapeDtypeStruct((b * s, ATT_WIDTH), BF16),
        compiler_params=_cparams(2),
        name="diff_attn",
    )(slopes, proj, proj, proj, bias, lam_params, sub_g)


def _first_max4(v0, v1, v2, v3):
    m = jnp.maximum(jnp.maximum(v0, v1), jnp.maximum(v2, v3))
    idx = jnp.where(v0 == m, 0, jnp.where(v1 == m, 1, jnp.where(v2 == m, 2, 3))).astype(I32)
    return m, idx


def _mix_and_logits(yl_ref, ya_ref, x_ref, wo_bf, g2_ref, wr2_ref, br_ref, h1_ref, hn_cur, lg_cur):
    mix = (jnp.dot(yl_ref[...], wo_bf[0:LRU_WIDTH, :], preferred_element_type=F32)
           + jnp.dot(ya_ref[...], wo_bf[LRU_WIDTH:, :], preferred_element_type=F32))
    h1 = x_ref[...] + mix
    h1_ref[...] = h1
    ms = jnp.mean(h1 * h1, axis=-1, keepdims=True)
    hn = (h1 * lax.rsqrt(ms + EPS)) * g2_ref[...]
    hn_hi = hn.astype(BF16)
    hn_lo = (hn - hn_hi.astype(F32)).astype(BF16)
    hn_cur[...] = hn_hi
    hh_hl = jnp.dot(hn_hi, wr2_ref[...], preferred_element_type=F32)
    lg_cur[...] = (hh_hl[:, 0:LANES] + hh_hl[:, LANES:2 * LANES]
                   + jnp.dot(hn_lo, wr2_ref[:, 0:LANES], preferred_element_type=F32)) + br_ref[...]


def _route_and_sort(hn_ref, lg_ref, xl_ref, aux_ref, gt_ref):
    tm = hn_ref.shape[0]
    hn_hi = hn_ref[...]
    lt = lg_ref[...].T
    row = lambda n: lt[n:n + 1, :]
    gmax, gidx = _first_max4(row(0), row(1), row(2), row(3))
    zg = (jnp.exp(row(0) - gmax) + jnp.exp(row(1) - gmax)
          + jnp.exp(row(2) - gmax) + jnp.exp(row(3) - gmax))
    g_gate = 1.0 / zg
    base = N_GROUPS
    sel = [jnp.where(gidx == 0, row(base + j),
                     jnp.where(gidx == 1, row(base + 4 + j),
                               jnp.where(gidx == 2, row(base + 8 + j), row(base + 12 + j))))
           for j in range(EXPERTS_PER_GROUP)]
    m1, i1 = _first_max4(*sel)
    ze = sum(jnp.exp(sj - m1) for sj in sel)
    rest = [jnp.where(i1 == j, -jnp.inf, sel[j]) for j in range(EXPERTS_PER_GROUP)]
    m2, i2 = _first_max4(*rest)
    p1 = 1.0 / ze
    p2 = jnp.exp(m2 - m1) / ze
    gate1 = g_gate * (p1 / (p1 + p2))
    gate2 = g_gate * (p2 / (p1 + p2))
    e1 = gidx * EXPERTS_PER_GROUP + i1
    e2 = gidx * EXPERTS_PER_GROUP + i2

    eio = lax.broadcasted_iota(I32, (N_EXPERTS, tm), 0)
    oh1 = eio == e1
    oh2 = eio == e2
    both = (oh1 | oh2).astype(F32)
    cnt = jnp.sum(both, axis=1, keepdims=True)
    grp = jnp.floor((cnt + (ROW_CHUNK - 1)) * (1.0 / ROW_CHUNK)) * ROW_CHUNK
    ti = lax.broadcasted_iota(I32, (tm, tm), 0)
    tj = lax.broadcasted_iota(I32, (tm, tm), 1)
    before = (ti < tj).astype(BF16)
    rank = jnp.dot(both.astype(BF16), before, preferred_element_type=F32)
    start1 = jnp.sum(jnp.where(eio < e1, grp, 0.0), axis=0, keepdims=True)
    start2 = jnp.sum(jnp.where(eio < e2, grp, 0.0), axis=0, keepdims=True)
    slot1 = start1 + jnp.sum(jnp.where(oh1, rank, 0.0), axis=0, keepdims=True)
    slot2 = start2 + jnp.sum(jnp.where(oh2, rank, 0.0), axis=0, keepdims=True)

    loc = xl_ref.shape[0]
    sio = lax.broadcasted_iota(I32, (loc, tm), 0)
    perm = ((sio == slot1.astype(I32)) | (sio == slot2.astype(I32))).astype(BF16)
    xs = jnp.dot(perm, hn_hi, preferred_element_type=F32)
    xl_ref[...] = xs.astype(BF16)

    rio = lax.broadcasted_iota(I32, (LANES, tm), 0)
    aux_t = jnp.where(rio == 0, slot1, jnp.where(rio == 1, slot2,
                      jnp.where(rio == 2, gate1, jnp.where(rio == 3, gate2, 0.0))))
    aux_ref[...] = aux_t.T
    gt_ref[0] = jnp.broadcast_to(grp, (N_EXPERTS, LANES))


def _route_body(yl_ref, ya_ref, x_ref, wo_ref, g2_ref, wr2_ref, br_ref,
                h1_ref, xl_ref, aux_ref, gt_ref, wo_bf, hn_cur, hn_prev, lg_cur, lg_prev):
    @pl.when(pl.program_id(0) == 0)
    def _():
        wo_bf[...] = wo_ref[...].astype(BF16)
        hn_prev[...] = jnp.zeros(hn_prev.shape, BF16)
        lg_prev[...] = jnp.zeros(lg_prev.shape, F32)

    _route_and_sort(hn_prev, lg_prev, xl_ref, aux_ref, gt_ref)
    _mix_and_logits(yl_ref, ya_ref, x_ref, wo_bf, g2_ref, wr2_ref, br_ref, h1_ref, hn_cur, lg_cur)
    hn_prev[...] = hn_cur[...]
    lg_prev[...] = lg_cur[...]


def _out_route(y_lru, y_att, x2, w_out, g2, wr2, br):
    t = x2.shape[0]
    tm = min(TM_TOK, t)
    nt = t // tm
    loc = 2 * tm + N_EXPERTS * ROW_CHUNK
    const = lambda shape: pl.BlockSpec(shape, lambda i: (0,) * len(shape))
    cur = lambda i: (jnp.minimum(i, nt - 1), 0)
    prev = lambda i: (jnp.maximum(i - 1, 0), 0)
    return pl.pallas_call(
        _route_body,
        grid=(nt + 1,),
        in_specs=[
            pl.BlockSpec((tm, LRU_WIDTH), cur),
            pl.BlockSpec((tm, ATT_WIDTH), cur),
            pl.BlockSpec((tm, D_MODEL), cur),
            const((D_MODEL, D_MODEL)),
            const((1, D_MODEL)),
            const((D_MODEL, 2 * LANES)),
            const((1, LANES)),
        ],
        out_specs=[
            pl.BlockSpec((tm, D_MODEL), cur),
            pl.BlockSpec((loc, D_MODEL), prev),
            pl.BlockSpec((tm, LANES), prev),
            pl.BlockSpec((1, N_EXPERTS, LANES), lambda i: (jnp.maximum(i - 1, 0), 0, 0)),
        ],
        out_shape=[
            jax.ShapeDtypeStruct((t, D_MODEL), F32),
            jax.ShapeDtypeStruct((nt * loc, D_MODEL), BF16),
            jax.ShapeDtypeStruct((t, LANES), F32),
            jax.ShapeDtypeStruct((nt, N_EXPERTS, LANES), F32),
        ],
        scratch_shapes=[
            pltpu.VMEM((D_MODEL, D_MODEL), BF16),
            pltpu.VMEM((tm, D_MODEL), BF16),
            pltpu.VMEM((tm, D_MODEL), BF16),
            pltpu.VMEM((tm, LANES), F32),
            pltpu.VMEM((tm, LANES), F32),
        ],
        compiler_params=_cparams(1),
        name="out_route",
    )(y_lru, y_att, x2, w_out, g2, wr2, br)


CHUNKS_PER_TILE = TM_EXP // ROW_CHUNK


DUMP_CHUNKS = 2 * CHUNKS_PER_TILE


def _expert_body(tile_e_ref, nused_ref, src_ref, dst_ref, tail_row_ref, tail_n_ref,
                 xl_ref, wg_ref, wu_ref, wd_ref, yl_ref,
                 xbuf, ybuf, zbuf, wg_bf, wu_bf, wd_bf, gsem, ssem, zsem):
    j = pl.program_id(0)
    nused = nused_ref[0]
    nt = tail_n_ref.shape[0]
    slot = lax.rem(j, 2)
    dump_row = yl_ref.shape[0] - DUMP_CHUNKS * ROW_CHUNK

    def rows(c):
        if isinstance(c, int):
            return pl.ds(c * ROW_CHUNK, ROW_CHUNK)
        return pl.ds(pl.multiple_of(c * ROW_CHUNK, ROW_CHUNK), ROW_CHUNK)

    def start_gather(t, sl):
        for k in range(CHUNKS_PER_TILE):
            pltpu.make_async_copy(xl_ref.at[rows(src_ref[t * CHUNKS_PER_TILE + k])],
                                  xbuf.at[sl, rows(k)], gsem.at[sl]).start()

    def start_scatter(t, sl):
        for k in range(CHUNKS_PER_TILE):
            pltpu.make_async_copy(ybuf.at[sl, rows(k)],
                                  yl_ref.at[rows(dst_ref[t * CHUNKS_PER_TILE + k])], ssem.at[sl]).start()

    def wait_gather(sl):
        pltpu.make_async_copy(xl_ref.at[pl.ds(0, TM_EXP)], xbuf.at[sl], gsem.at[sl]).wait()

    def wait_scatter(sl):
        pltpu.make_async_copy(ybuf.at[sl], yl_ref.at[pl.ds(0, TM_EXP)], ssem.at[sl]).wait()

    def for_count(n, fn):
        def body(k, c):
            fn(k)
            return c
        lax.fori_loop(0, n, body, 0)

    def zero_fill(i, k):
        return pltpu.make_async_copy(
            zbuf, yl_ref.at[pl.ds(pl.multiple_of(tail_row_ref[i] + k * ROW_CHUNK, ROW_CHUNK), ROW_CHUNK)],
            zsem)

    @pl.when(j == 0)
    def _():
        zbuf[...] = jnp.zeros(zbuf.shape, BF16)
        ybuf[...] = jnp.zeros(ybuf.shape, BF16)
        for_count(nt, lambda i: for_count(tail_n_ref[i], lambda k: zero_fill(i, k).start()))
        start_gather(0, 0)
        for sl in range(2):
            for k in range(CHUNKS_PER_TILE):
                pltpu.make_async_copy(
                    ybuf.at[sl, rows(k)],
                    yl_ref.at[pl.ds(dump_row + (sl * CHUNKS_PER_TILE + k) * ROW_CHUNK, ROW_CHUNK)],
                    ssem.at[sl]).start()

    new_expert = (j == 0) | (tile_e_ref[j] != tile_e_ref[jnp.maximum(j - 1, 0)])

    @pl.when(new_expert & (j < nused))
    def _():
        wg_bf[...] = wg_ref[0].astype(BF16)
        wu_bf[...] = wu_ref[0].astype(BF16)
        wd_bf[...] = wd_ref[0].astype(BF16)

    @pl.when(j < nused)
    def _():
        wait_gather(slot)
        wait_scatter(slot)
        start_gather(jnp.minimum(j + 1, nused - 1), 1 - slot)
        xb = xbuf[slot]
        y = None
        for c in range(2):
            cols = slice(c * (D_EXPERT // 2), (c + 1) * (D_EXPERT // 2))
            gate = jnp.dot(xb, wg_bf[:, cols], preferred_element_type=F32)
            up = jnp.dot(xb, wu_bf[:, cols], preferred_element_type=F32)
            hid = (jax.nn.silu(gate) * up).astype(BF16)
            part = jnp.dot(hid, wd_bf[cols, :], preferred_element_type=F32)
            y = part if y is None else y + part
        ybuf[slot] = y.astype(BF16)
        start_scatter(j, slot)

    @pl.when(j == nused - 1)
    def _():
        wait_gather(1 - slot)
        wait_scatter(1 - slot)
        wait_scatter(slot)
        for_count(nt, lambda i: for_count(tail_n_ref[i], lambda k: zero_fill(i, 0).wait()))


def _experts(xl, tables, w_gate, w_up, w_down, n_tiles):
    tile_e, nused, src_map, dst_map, tail_row, tail_n = tables
    wmap = lambda j, te, *_: (te[j], 0, 0)
    grid_spec = pltpu.PrefetchScalarGridSpec(
        num_scalar_prefetch=6,
        grid=(n_tiles,),
        in_specs=[
            pl.BlockSpec(memory_space=pl.ANY),
            pl.BlockSpec((1, D_MODEL, D_EXPERT), wmap),
            pl.BlockSpec((1, D_MODEL, D_EXPERT), wmap),
            pl.BlockSpec((1, D_EXPERT, D_MODEL), wmap),
        ],
        out_specs=pl.BlockSpec(memory_space=pl.ANY),
        scratch_shapes=[
            pltpu.VMEM((2, TM_EXP, D_MODEL), BF16),
            pltpu.VMEM((2, TM_EXP, D_MODEL), BF16),
            pltpu.VMEM((ROW_CHUNK, D_MODEL), BF16),
            pltpu.VMEM((D_MODEL, D_EXPERT), BF16),
            pltpu.VMEM((D_MODEL, D_EXPERT), BF16),
            pltpu.VMEM((D_EXPERT, D_MODEL), BF16),
            pltpu.SemaphoreType.DMA((2,)),
            pltpu.SemaphoreType.DMA((2,)),
            pltpu.SemaphoreType.DMA,
        ],
    )
    return pl.pallas_call(
        _expert_body,
        grid_spec=grid_spec,
        out_shape=jax.ShapeDtypeStruct((xl.shape[0] + DUMP_CHUNKS * ROW_CHUNK, D_MODEL), BF16),
        compiler_params=_cparams(1),
        name="experts",
    )(tile_e, nused, src_map, dst_map, tail_row, tail_n, xl, w_gate, w_up, w_down)


def _combine_body(h1_ref, aux_ref, yl_ref, o_ref):
    tm = h1_ref.shape[0]
    loc = yl_ref.shape[0]
    aux = aux_ref[...]
    slot1 = aux[:, 0:1].astype(I32)
    slot2 = aux[:, 1:2].astype(I32)
    sio = lax.broadcasted_iota(I32, (tm, loc), 1)
    gm = jnp.where(sio == slot1, aux[:, 2:3], 0.0) + jnp.where(sio == slot2, aux[:, 3:4], 0.0)
    o_ref[...] = h1_ref[...] + jnp.dot(gm.astype(BF16), yl_ref[...], preferred_element_type=F32)


def _combine(h1, aux, yl, nt, tm, loc):
    return pl.pallas_call(
        _combine_body,
        grid=(nt,),
        in_specs=[
            pl.BlockSpec((tm, D_MODEL), lambda i: (i, 0)),
            pl.BlockSpec((tm, LANES), lambda i: (i, 0)),
            pl.BlockSpec((loc, D_MODEL), lambda i: (i, 0)),
        ],
        out_specs=pl.BlockSpec((tm, D_MODEL), lambda i: (i, 0)),
        out_shape=jax.ShapeDtypeStruct(h1.shape, F32),
        compiler_params=_cparams(1),
        name="combine",
    )(h1, aux, yl)


def _excl_cumsum(a, axis):
    return jnp.cumsum(a, axis=axis) - a


def _expert_tables(grp, n_tiles, loc):
    nt = grp.shape[0]
    gch = grp // ROW_CHUNK
    loc_start = _excl_cumsum(gch, 1)
    used = jnp.sum(gch, axis=1)
    col = jnp.sum(gch, axis=0)
    seg = ((col + CHUNKS_PER_TILE - 1) // CHUNKS_PER_TILE) * CHUNKS_PER_TILE
    seg_end = jnp.cumsum(seg)
    off = seg_end - seg
    tile_first = jnp.arange(n_tiles, dtype=I32) * CHUNKS_PER_TILE
    tile_e = jnp.minimum(jnp.sum(seg_end[None, :] <= tile_first[:, None], axis=1), N_EXPERTS - 1)
    nused = seg_end[-1:] // CHUNKS_PER_TILE
    onehot = tile_e[:, None] == jnp.arange(N_EXPERTS, dtype=I32)[None, :]
    pick = lambda tab: jnp.sum(jnp.where(onehot[:, :, None], tab.T[None], 0), axis=1)
    pick1 = lambda vec: jnp.sum(jnp.where(onehot, vec[None, :], 0), axis=1)
    first = tile_first - pick1(off)
    nvalid = jnp.clip(pick1(col) - first, 0, CHUNKS_PER_TILE)
    cum = jnp.cumsum(gch, axis=0)
    delta = jnp.arange(nt, dtype=I32)[:, None] * (loc // ROW_CHUNK) + loc_start - (cum - gch)
    step = delta - jnp.concatenate([jnp.zeros((1, N_EXPERTS), I32), delta[:-1]], axis=0)
    cum_prev = jnp.concatenate([jnp.full((1, N_EXPERTS), -1, I32), cum[:-1]], axis=0)
    cc = first[:, None] + jnp.arange(CHUNKS_PER_TILE, dtype=I32)[None, :]
    passed = pick(cum_prev)[:, None, :] <= cc[:, :, None]
    chunk_map = cc + jnp.sum(jnp.where(passed, pick(step)[:, None, :], 0), axis=2)
    k_in_tile = jnp.arange(CHUNKS_PER_TILE, dtype=I32)[None, :]
    valid = k_in_tile < nvalid[:, None]
    zero_chunk = loc // ROW_CHUNK - 1
    dump = nt * (loc // ROW_CHUNK) + (jnp.arange(n_tiles, dtype=I32)[:, None] % 2) * CHUNKS_PER_TILE + k_in_tile
    flat = lambda a: a.reshape(n_tiles * CHUNKS_PER_TILE)
    src_map = flat(jnp.where(valid, chunk_map, zero_chunk))
    dst_map = flat(jnp.where(valid, chunk_map, dump))

    tail_row = jnp.arange(nt, dtype=I32) * loc + used * ROW_CHUNK
    tail_n = loc // ROW_CHUNK - used
    as_i32 = lambda a: a.astype(I32)
    return tuple(map(as_i32, (tile_e, nused, src_map, dst_map, tail_row, tail_n)))


def kernel(x, norm1_g, w_in, conv_w, conv_b, w_gate_a, b_gate_a, w_gate_x, b_gate_x, lru_lambda,
           lru_out_g, q_norm_g, k_norm_g, lambda_q1, lambda_k1, lambda_q2, lambda_k2, sub_norm_g,
           w_out, norm2_g, w_router_group, b_router_group, w_router_expert, b_router_expert,
           w_expert_gate, w_expert_up, w_expert_down):
    b, s, d = x.shape
    assert d == D_MODEL and norm1_g.shape[0] == 1
    t = b * s
    l = 0
    x2 = x.reshape(t, d)

    scale = HEAD_DIM ** -0.5 * LOG2E
    qkg = jnp.stack([jnp.tile(q_norm_g[l], 2) * scale, jnp.tile(k_norm_g[l], 2)]).astype(F32)
    eye = jnp.eye(LRU_BLOCKS, dtype=F32)
    blockdiag = lambda w: jnp.einsum("ncd,nm->ncmd", w, eye).reshape(LRU_WIDTH, LRU_WIDTH)
    wgate = jnp.concatenate([blockdiag(w_gate_a[l]), blockdiag(w_gate_x[l])], axis=1).astype(BF16)
    bgate = jnp.concatenate([b_gate_a[l], b_gate_x[l]])[None, :]
    lam_params = jnp.stack([lambda_q1[l], lambda_k1[l], lambda_q2[l], lambda_k2[l]])
    wr = jnp.concatenate(
        [w_router_group[l], jnp.transpose(w_router_expert[l], (1, 0, 2)).reshape(d, N_EXPERTS)], axis=1)
    wr = jnp.pad(wr, ((0, 0), (0, LANES - wr.shape[1])))
    wr_hi = wr.astype(BF16)
    wr2 = jnp.concatenate([wr_hi, (wr - wr_hi.astype(F32)).astype(BF16)], axis=1)
    br = jnp.pad(jnp.concatenate([b_router_group[l], b_router_expert[l].reshape(-1)]),
                 (0, LANES - N_GROUPS - N_EXPERTS))[None, :]

    proj, y_lru = _in_proj_lru(x2, s, norm1_g[l][None, :], w_in[l], qkg, conv_w[l], conv_b[l][None, :],
                               wgate, bgate, lru_lambda[l][None, :], lru_out_g[l][None, :])
    y_att = _attention(proj, b, s, lam_params, sub_norm_g[l][None, :])
    h1, xl, aux, gt = _out_route(y_lru, y_att, x2, w_out[l], norm2_g[l][None, :],
                                 wr2, br)

    tm = min(TM_TOK, t)
    nt = t // tm
    loc = 2 * tm + N_EXPERTS * ROW_CHUNK
    max_rows = 2 * t + nt * N_EXPERTS * (ROW_CHUNK - 1) + N_EXPERTS * (TM_EXP - ROW_CHUNK)
    n_tiles = -(-max_rows // TM_EXP)
    grp = gt[:, :, 0].astype(I32)
    tables = _expert_tables(grp, n_tiles, loc)
    yl = _experts(xl, tables, w_expert_gate[l], w_expert_up[l], w_expert_down[l], n_tiles)
    out = _combine(h1, aux, yl, nt, tm, loc)
    return out.reshape(b, s, d)
```

```python
import functools
import math

import numpy as np
import jax
import jax.numpy as jnp
from jax import lax
from jax.experimental import pallas as pl
from jax.experimental.pallas import tpu as pltpu

F32 = jnp.float32
BF16 = jnp.bfloat16
I32 = jnp.int32

D_MODEL = 1024
LRU_WIDTH = 512
LRU_BLOCKS = 8
LRU_BLOCK_W = LRU_WIDTH // LRU_BLOCKS
CONV_W = 4
LRU_C = 8.0
ATT_WIDTH = 512
N_HEADS = 4
HEAD_DIM = 64
V_DIM = 128
IN_COLS = 2 * LRU_WIDTH + 3 * ATT_WIDTH
N_GROUPS = 4
EXPERTS_PER_GROUP = 4
N_EXPERTS = N_GROUPS * EXPERTS_PER_GROUP
D_EXPERT = D_MODEL // 2
CHUNK = 64
EPS = 1e-6
NEG_BIG = -1e30
LAMBDA_INIT = 0.8 - 0.6 * math.exp(-0.3 * 0)
LOG2E = math.log2(math.e)

LANES = 128
SUBLANES = 8

TM_PROJ = 512
TQ = 512
TM_TOK = 512
TM_EXP = 512
ROW_CHUNK = 2 * SUBLANES
VMEM_LIMIT = 56 * 1024 * 1024


def _cparams(n_axes):
    return pltpu.CompilerParams(
        dimension_semantics=("arbitrary",) * n_axes, vmem_limit_bytes=VMEM_LIMIT)


def _inproj_body(x_ref, g1_ref, w_ref, qkg_ref, o_ref, w_bf):
    @pl.when(pl.program_id(0) == 0)
    def _():
        w_bf[...] = w_ref[...].astype(BF16)

    x = x_ref[...]
    ms = jnp.mean(x * x, axis=-1, keepdims=True)
    hn = ((x * lax.rsqrt(ms + EPS)) * g1_ref[...]).astype(BF16)
    tm = x.shape[0]
    lo_half = lax.broadcasted_iota(I32, (tm, LANES), 1) < HEAD_DIM
    width = 512
    for c in range(IN_COLS // width):
        c0 = c * width
        acc = jnp.dot(hn, w_bf[:, c0:c0 + width], preferred_element_type=F32)
        if c in (2, 3):
            gain = qkg_ref[c - 2:c - 1, :]
            for b in range(width // LANES):
                blk = acc[:, b * LANES:(b + 1) * LANES]
                sq = blk * blk
                s_lo = jnp.sum(jnp.where(lo_half, sq, 0.0), axis=-1, keepdims=True)
                s_hi = jnp.sum(jnp.where(lo_half, 0.0, sq), axis=-1, keepdims=True)
                inv = jnp.where(lo_half,
                                lax.rsqrt(s_lo * (1.0 / HEAD_DIM) + EPS),
                                lax.rsqrt(s_hi * (1.0 / HEAD_DIM) + EPS))
                o_ref[:, c0 + b * LANES:c0 + (b + 1) * LANES] = ((blk * inv) * gain).astype(BF16)
        else:
            o_ref[:, c0:c0 + width] = acc.astype(BF16)


def _in_proj(x2, g1, w_in, qkg):
    t = x2.shape[0]
    tm = min(TM_PROJ, t)
    return pl.pallas_call(
        _inproj_body,
        grid=(t // tm,),
        in_specs=[
            pl.BlockSpec((tm, D_MODEL), lambda i: (i, 0)),
            pl.BlockSpec((1, D_MODEL), lambda i: (0, 0)),
            pl.BlockSpec((D_MODEL, IN_COLS), lambda i: (0, 0)),
            pl.BlockSpec((2, LANES), lambda i: (0, 0)),
        ],
        out_specs=pl.BlockSpec((tm, IN_COLS), lambda i: (i, 0)),
        out_shape=jax.ShapeDtypeStruct((t, IN_COLS), BF16),
        scratch_shapes=[pltpu.VMEM((D_MODEL, IN_COLS), BF16)],
        compiler_params=_cparams(1),
        name="in_proj",
    )(x2, g1, w_in, qkg)


CONV_BLOCK = 128


def _lru_tile(first, p_ref, shift_ref, cw_ref, cb_ref, wg_ref, bg_ref, lam_ref, og_ref, o_ref,
              xbuf, hbuf, hc):
    ts = p_ref.shape[0]
    x_bf = p_ref[:, 0:LRU_WIDTH]
    gl = p_ref[:, LRU_WIDTH:2 * LRU_WIDTH].astype(F32)
    xbuf[0:CONV_BLOCK, :] = jnp.where(first, jnp.zeros((), BF16), xbuf[ts:ts + CONV_BLOCK, :])
    xbuf[CONV_BLOCK:CONV_BLOCK + ts, :] = x_bf
    taps = []
    for j in range(CONV_W - 1):
        shifted = jnp.concatenate(
            [jnp.dot(shift_ref[j], xbuf[r0:r0 + 2 * CONV_BLOCK, :], preferred_element_type=F32)
             for r0 in range(0, ts, CONV_BLOCK)], axis=0)
        taps.append(shifted * cw_ref[j:j + 1, :])
    xc = ((cb_ref[...] + taps[0]) + taps[1]) + taps[2] + x_bf.astype(F32) * cw_ref[CONV_W - 1:CONV_W, :]

    z = jnp.dot(xc.astype(BF16), wg_ref[...], preferred_element_type=F32) + bg_ref[...]
    r = 0.5 * jnp.tanh(0.5 * z[:, 0:LRU_WIDTH]) + 0.5
    gi = 0.5 * jnp.tanh(0.5 * z[:, LRU_WIDTH:2 * LRU_WIDTH]) + 0.5
    nl = -lam_ref[...]
    softplus = jnp.maximum(nl, 0.0) + jnp.log1p(jnp.exp(-jnp.abs(nl)))
    log_a = (-LRU_C) * r * softplus
    a = jnp.exp(log_a)
    v = -jnp.tanh(log_a) * (a * a + 1.0)
    u = jnp.where(v > 0.0, v * lax.rsqrt(v), 0.0) * (gi * xc)

    row = lax.broadcasted_iota(I32, (ts, LRU_WIDTH), 0) & (SUBLANES - 1)
    ca, cb = a, u
    for d in (1, 2, 4):
        a_sh = pltpu.roll(ca, d, axis=0)
        b_sh = pltpu.roll(cb, d, axis=0)
        take = row >= d
        cb = jnp.where(take, ca * b_sh + cb, cb)
        ca = jnp.where(take, ca * a_sh, ca)
    h = jnp.where(first, 0.0, hc[...])
    for blk in range(ts // SUBLANES):
        r0 = blk * SUBLANES
        hb = ca[r0:r0 + SUBLANES, :] * h + cb[r0:r0 + SUBLANES, :]
        hbuf[r0:r0 + SUBLANES, :] = hb
        h = hb[SUBLANES - 1:SUBLANES, :]
    hc[...] = h

    y = hbuf[...] * jax.nn.gelu(gl)
    ms = jnp.mean(y * y, axis=-1, keepdims=True)
    o_ref[...] = ((y * lax.rsqrt(ms + EPS)) * og_ref[...]).astype(o_ref.dtype)


def _conv_shift_matrices():
    t_idx = np.arange(CONV_BLOCK)[:, None]
    c_idx = np.arange(2 * CONV_BLOCK)[None, :]
    return jnp.asarray(np.stack([c_idx == CONV_BLOCK + t_idx - (CONV_W - 1) + j
                                 for j in range(CONV_W - 1)]), BF16)


SOFTMAX_ROWS = 32


def _attn_body(slope_ref, q_ref, k_ref, v_ref, bias_ref, lamp_ref, sg_ref, o_ref,
               qs_buf, s0, s1, p0, p1, a0, a1, m_buf, acc0, acc1, *, tq, nq):
    h = pl.program_id(1)
    slope = slope_ref[h]
    s_bufs, p_bufs, a_bufs, accs = (s0, s1), (p0, p1), (a0, a1), (acc0, acc1)
    lo_half = lax.broadcasted_iota(I32, (tq, LANES), 1) < HEAD_DIM
    for i in range(nq):
        q = q_ref[i * tq:(i + 1) * tq, :]
        zero = jnp.zeros_like(q)
        qs_buf[i, 0:tq, :] = jnp.where(lo_half, q, zero)
        qs_buf[i, tq:2 * tq, :] = jnp.where(lo_half, zero, q)
    ones = jnp.ones((tq, V_DIM), BF16)
    lp = lamp_ref[...]
    lam = (jnp.exp(jnp.sum(lp[0:1, :] * lp[1:2, :], axis=-1, keepdims=True))
           - jnp.exp(jnp.sum(lp[2:3, :] * lp[3:4, :], axis=-1, keepdims=True))
           + LAMBDA_INIT)
    pairs = [(i, j) for i in range(nq) for j in range(i + 1)]
    hq = tq // 2

    def row_sets(i, j):
        if j < i or hq % CHUNK:
            return [((0, 2 * tq), tq)]
        return [((0, hq), hq), ((hq, tq), tq), ((tq, tq + hq), hq), ((tq + hq, 2 * tq), tq)]

    def scores(t):
        i, j = pairs[t]
        s_buf = s_bufs[t % 2]
        nt_dims = (((1,), (1,)), ((), ()))
        if j < i or hq % CHUNK:
            s_buf[...] = lax.dot_general(qs_buf[i], k_ref[j * tq:(j + 1) * tq, :], nt_dims,
                                         preferred_element_type=F32)
            return
        s_buf[:, 0:hq] = lax.dot_general(qs_buf[i], k_ref[j * tq:j * tq + hq, :], nt_dims,
                                         preferred_element_type=F32)
        for (r0, r1), width in row_sets(i, j):
            if width == tq:
                s_buf[r0:r1, hq:tq] = lax.dot_general(
                    qs_buf[i, r0:r1, :], k_ref[j * tq + hq:(j + 1) * tq, :], nt_dims,
                    preferred_element_type=F32)

    def softmax(t):
        i, j = pairs[t]
        which = 1 if j == i else 0
        shift = slope * float(-(i - j) * tq)
        s_buf, p_buf, a_buf = s_bufs[t % 2], p_bufs[t % 2], a_bufs[t % 2]
        for (r0, r1), width in row_sets(i, j):
            for rb in range(r0, r1, SOFTMAX_ROWS):
                rows = slice(rb, rb + SOFTMAX_ROWS)
                sb = s_buf[rows, 0:width] + bias_ref[0, which, rows, 0:width]
                m_new = jnp.broadcast_to(jnp.max(sb, axis=-1, keepdims=True), (SOFTMAX_ROWS, LANES)) + shift
                if j > 0:
                    m_old = m_buf[rows, :]
                    m_new = jnp.maximum(m_old, m_new)
                    a_buf[rows, :] = jnp.exp2(m_old - m_new)
                m_sub = m_new - shift
                p_buf[rows, 0:width] = jnp.exp2(
                    sb - jnp.concatenate([m_sub] * (width // LANES), axis=1)).astype(BF16)
                m_buf[rows, :] = m_new

    def accumulate(t):
        i, j = pairs[t]
        acc = accs[i % 2]
        v_aug = jnp.concatenate([v_ref[j * tq:(j + 1) * tq, :], ones], axis=1)
        for (r0, r1), width in row_sets(i, j):
            pv = jnp.dot(p_bufs[t % 2][r0:r1, 0:width], v_aug[0:width, :], preferred_element_type=F32)
            if j == 0:
                acc[r0:r1, :] = pv
            else:
                alpha = a_bufs[t % 2][r0:r1, :]
                acc[r0:r1, :] = jnp.concatenate([alpha] * (2 * V_DIM // LANES), axis=1) * acc[r0:r1, :] + pv
        if j == i:
            o = (acc[0:tq, 0:V_DIM] / acc[0:tq, V_DIM:V_DIM + 1]
                 - lam * (acc[tq:2 * tq, 0:V_DIM] / acc[tq:2 * tq, V_DIM:V_DIM + 1]))
            ms = jnp.mean(o * o, axis=-1, keepdims=True)
            o = ((o * lax.rsqrt(ms + EPS)) * sg_ref[...]) * (1.0 - LAMBDA_INIT)
            o_ref[i * tq:(i + 1) * tq, :] = o.astype(o_ref.dtype)

    scores(0)
    for t in range(len(pairs)):
        if t + 1 < len(pairs):
            scores(t + 1)
        softmax(t)
        if t >= 1:
            accumulate(t - 1)
    accumulate(len(pairs) - 1)


def _alibi_tables(tq):
    slopes = np.exp2(-8.0 * np.arange(1, N_HEADS + 1, dtype=np.float64) / N_HEADS)
    qi = np.arange(tq)[:, None]
    kj = np.arange(tq)[None, :]
    off = -(slopes[:, None, None] * (qi - kj)[None])
    allowed = (kj // CHUNK) <= (qi // CHUNK)
    diag = np.where(allowed[None], -(slopes[:, None, None] * np.abs(qi - kj)[None]), NEG_BIG)
    tab = np.stack([off, diag], axis=1)
    tab = np.concatenate([tab, tab], axis=2)
    return jnp.asarray(tab * LOG2E, F32), jnp.asarray(slopes * LOG2E, F32)


def _attention(proj, b, s, lam_params, sub_g):
    tq = min(TQ, s)
    nq = s // tq
    bias, slopes = _alibi_tables(tq)
    qcol = 2 * LRU_WIDTH // LANES
    kcol = qcol + ATT_WIDTH // LANES
    vcol = kcol + ATT_WIDTH // LANES
    score_buf = pltpu.VMEM((2 * tq, tq), F32)
    prob_buf = pltpu.VMEM((2 * tq, tq), BF16)
    col_buf = pltpu.VMEM((2 * tq, LANES), F32)
    acc_buf = pltpu.VMEM((2 * tq, 2 * V_DIM), F32)
    grid_spec = pltpu.PrefetchScalarGridSpec(
        num_scalar_prefetch=1,
        grid=(b, N_HEADS),
        in_specs=[
            pl.BlockSpec((s, LANES), lambda bi, h, sl: (bi, qcol + h)),
            pl.BlockSpec((s, LANES), lambda bi, h, sl: (bi, kcol + h)),
            pl.BlockSpec((s, LANES), lambda bi, h, sl: (bi, vcol + h)),
            pl.BlockSpec((1, 2, 2 * tq, tq), lambda bi, h, sl: (h, 0, 0, 0)),
            pl.BlockSpec((4, HEAD_DIM), lambda bi, h, sl: (0, 0)),
            pl.BlockSpec((1, V_DIM), lambda bi, h, sl: (0, 0)),
        ],
        out_specs=pl.BlockSpec((s, V_DIM), lambda bi, h, sl: (bi, h)),
        scratch_shapes=[
            pltpu.VMEM((nq, 2 * tq, LANES), BF16),
            score_buf, score_buf, prob_buf, prob_buf, col_buf, col_buf, col_buf, acc_buf, acc_buf,
        ],
    )
    return pl.pallas_call(
        functools.partial(_attn_body, tq=tq, nq=nq),
        grid_spec=grid_spec,
        out_shape=jax.ShapeDtypeStruct((b * s, ATT_WIDTH), BF16),
        compiler_params=_cparams(2),
        name="diff_attn",
    )(slopes, proj, proj, proj, bias, lam_params, sub_g)


def _first_max4(v0, v1, v2, v3):
    m = jnp.maximum(jnp.maximum(v0, v1), jnp.maximum(v2, v3))
    idx = jnp.where(v0 == m, 0, jnp.where(v1 == m, 1, jnp.where(v2 == m, 2, 3))).astype(I32)
    return m, idx


def _mix_and_logits(yl_ref, ya_ref, x_ref, wo_bf, g2_ref, wr2_ref, br_ref, h1_ref, hn_cur, lg_cur):
    mix = (jnp.dot(yl_ref[...], wo_bf[0:LRU_WIDTH, :], preferred_element_type=F32)
           + jnp.dot(ya_ref[...], wo_bf[LRU_WIDTH:, :], preferred_element_type=F32))
    h1 = x_ref[...] + mix
    h1_ref[...] = h1
    ms = jnp.mean(h1 * h1, axis=-1, keepdims=True)
    hn = (h1 * lax.rsqrt(ms + EPS)) * g2_ref[...]
    hn_hi = hn.astype(BF16)
    hn_lo = (hn - hn_hi.astype(F32)).astype(BF16)
    hn_cur[...] = hn_hi
    hh_hl = jnp.dot(hn_hi, wr2_ref[...], preferred_element_type=F32)
    lg_cur[...] = (hh_hl[:, 0:LANES] + hh_hl[:, LANES:2 * LANES]
                   + jnp.dot(hn_lo, wr2_ref[:, 0:LANES], preferred_element_type=F32)) + br_ref[...]


def _route_and_sort(hn_ref, lg_ref, xl_ref, aux_ref, gt_ref):
    tm = hn_ref.shape[0]
    hn_hi = hn_ref[...]
    lt = lg_ref[...].T
    row = lambda n: lt[n:n + 1, :]
    gmax, gidx = _first_max4(row(0), row(1), row(2), row(3))
    zg = (jnp.exp(row(0) - gmax) + jnp.exp(row(1) - gmax)
          + jnp.exp(row(2) - gmax) + jnp.exp(row(3) - gmax))
    g_gate = 1.0 / zg
    base = N_GROUPS
    sel = [jnp.where(gidx == 0, row(base + j),
                     jnp.where(gidx == 1, row(base + 4 + j),
                               jnp.where(gidx == 2, row(base + 8 + j), row(base + 12 + j))))
           for j in range(EXPERTS_PER_GROUP)]
    m1, i1 = _first_max4(*sel)
    ze = sum(jnp.exp(sj - m1) for sj in sel)
    rest = [jnp.where(i1 == j, -jnp.inf, sel[j]) for j in range(EXPERTS_PER_GROUP)]
    m2, i2 = _first_max4(*rest)
    p1 = 1.0 / ze
    p2 = jnp.exp(m2 - m1) / ze
    gate1 = g_gate * (p1 / (p1 + p2))
    gate2 = g_gate * (p2 / (p1 + p2))
    e1 = gidx * EXPERTS_PER_GROUP + i1
    e2 = gidx * EXPERTS_PER_GROUP + i2

    eio = lax.broadcasted_iota(I32, (N_EXPERTS, tm), 0)
    oh1 = eio == e1
    oh2 = eio == e2
    both = (oh1 | oh2).astype(F32)
    cnt = jnp.sum(both, axis=1, keepdims=True)
    grp = jnp.floor((cnt + (ROW_CHUNK - 1)) * (1.0 / ROW_CHUNK)) * ROW_CHUNK
    ti = lax.broadcasted_iota(I32, (tm, tm), 0)
    tj = lax.broadcasted_iota(I32, (tm, tm), 1)
    before = (ti < tj).astype(BF16)
    rank = jnp.dot(both.astype(BF16), before, preferred_element_type=F32)
    start1 = jnp.sum(jnp.where(eio < e1, grp, 0.0), axis=0, keepdims=True)
    start2 = jnp.sum(jnp.where(eio < e2, grp, 0.0), axis=0, keepdims=True)
    slot1 = start1 + jnp.sum(jnp.where(oh1, rank, 0.0), axis=0, keepdims=True)
    slot2 = start2 + jnp.sum(jnp.where(oh2, rank, 0.0), axis=0, keepdims=True)

    loc = xl_ref.shape[0]
    sio = lax.broadcasted_iota(I32, (loc, tm), 0)
    perm = ((sio == slot1.astype(I32)) | (sio == slot2.astype(I32))).astype(BF16)
    xs = jnp.dot(perm, hn_hi, preferred_element_type=F32)
    xl_ref[...] = xs.astype(BF16)

    rio = lax.broadcasted_iota(I32, (LANES, tm), 0)
    aux_t = jnp.where(rio == 0, slot1, jnp.where(rio == 1, slot2,
                      jnp.where(rio == 2, gate1, jnp.where(rio == 3, gate2, 0.0))))
    aux_ref[...] = aux_t.T
    gt_ref[0] = jnp.broadcast_to(grp, (N_EXPERTS, LANES))


def _route_body(p_ref, ya_ref, x_ref, wo_ref, g2_ref, wr2_ref, br_ref,
                shift_ref, cw_ref, cb_ref, wg_ref, bg_ref, lam_ref, og_ref,
                h1_ref, xl_ref, aux_ref, gt_ref,
                wo_bf, yl_new, yl_hand, hn_cur, hn_prev, lg_cur, lg_prev, xbuf, hbuf, hc, *,
                tiles_per_seq, n_tiles):
    s = pl.program_id(0)

    @pl.when(s == 0)
    def _():
        wo_bf[...] = wo_ref[...].astype(BF16)
        yl_hand[...] = jnp.zeros(yl_hand.shape, BF16)
        hn_prev[...] = jnp.zeros(hn_prev.shape, BF16)
        lg_prev[...] = jnp.zeros(lg_prev.shape, F32)
        xbuf[...] = jnp.zeros(xbuf.shape, BF16)
        hc[...] = jnp.zeros(hc.shape, F32)

    _route_and_sort(hn_prev, lg_prev, xl_ref, aux_ref, gt_ref)
    _mix_and_logits(yl_hand, ya_ref, x_ref, wo_bf, g2_ref, wr2_ref, br_ref, h1_ref, hn_cur, lg_cur)
    first = lax.rem(s, tiles_per_seq) == 0
    _lru_tile(first, p_ref, shift_ref, cw_ref, cb_ref, wg_ref, bg_ref, lam_ref, og_ref, yl_new,
              xbuf, hbuf, hc)
    hn_prev[...] = hn_cur[...]
    lg_prev[...] = lg_cur[...]
    yl_hand[...] = jnp.where(s < n_tiles, yl_new[...], yl_hand[...])


def _out_route(proj, y_att, x2, seq, w_out, g2, wr2, br, conv_w, conv_b, wgate, bgate, lam, out_g):
    t = x2.shape[0]
    tm = min(TM_TOK, seq)
    nt = t // tm
    loc = 2 * tm + N_EXPERTS * ROW_CHUNK
    const = lambda shape: pl.BlockSpec(shape, lambda i: (0,) * len(shape))
    lag = lambda k: (lambda i: (jnp.clip(i - k, 0, nt - 1), 0))
    return pl.pallas_call(
        functools.partial(_route_body, tiles_per_seq=seq // tm, n_tiles=nt),
        grid=(nt + 2,),
        in_specs=[
            pl.BlockSpec((tm, 2 * LRU_WIDTH), lag(0)),
            pl.BlockSpec((tm, ATT_WIDTH), lag(1)),
            pl.BlockSpec((tm, D_MODEL), lag(1)),
            const((D_MODEL, D_MODEL)),
            const((1, D_MODEL)),
            const((D_MODEL, 2 * LANES)),
            const((1, LANES)),
            const((CONV_W - 1, CONV_BLOCK, 2 * CONV_BLOCK)),
            const((CONV_W, LRU_WIDTH)),
            const((1, LRU_WIDTH)),
            const((LRU_WIDTH, 2 * LRU_WIDTH)),
            const((1, 2 * LRU_WIDTH)),
            const((1, LRU_WIDTH)),
            const((1, LRU_WIDTH)),
        ],
        out_specs=[
            pl.BlockSpec((tm, D_MODEL), lag(1)),
            pl.BlockSpec((loc, D_MODEL), lag(2)),
            pl.BlockSpec((tm, LANES), lag(2)),
            pl.BlockSpec((1, N_EXPERTS, LANES), lambda i: (jnp.clip(i - 2, 0, nt - 1), 0, 0)),
        ],
        out_shape=[
            jax.ShapeDtypeStruct((t, D_MODEL), F32),
            jax.ShapeDtypeStruct((nt * loc, D_MODEL), BF16),
            jax.ShapeDtypeStruct((t, LANES), F32),
            jax.ShapeDtypeStruct((nt, N_EXPERTS, LANES), F32),
        ],
        scratch_shapes=[
            pltpu.VMEM((D_MODEL, D_MODEL), BF16),
            pltpu.VMEM((tm, LRU_WIDTH), BF16),
            pltpu.VMEM((tm, LRU_WIDTH), BF16),
            pltpu.VMEM((tm, D_MODEL), BF16),
            pltpu.VMEM((tm, D_MODEL), BF16),
            pltpu.VMEM((tm, LANES), F32),
            pltpu.VMEM((tm, LANES), F32),
            pltpu.VMEM((tm + CONV_BLOCK, LRU_WIDTH), BF16),
            pltpu.VMEM((tm, LRU_WIDTH), F32),
            pltpu.VMEM((1, LRU_WIDTH), F32),
        ],
        compiler_params=_cparams(1),
        name="out_route_lru",
    )(proj, y_att, x2, w_out, g2, wr2, br, _conv_shift_matrices(), conv_w, conv_b, wgate, bgate, lam,
      out_g)


CHUNKS_PER_TILE = TM_EXP // ROW_CHUNK


DUMP_CHUNKS = 2 * CHUNKS_PER_TILE


def _expert_body(tile_e_ref, nused_ref, src_ref, dst_ref, tail_row_ref, tail_n_ref,
                 xl_ref, wg_ref, wu_ref, wd_ref, yl_ref,
                 xbuf, ybuf, zbuf, wg_bf, wu_bf, wd_bf, gsem, ssem, zsem):
    j = pl.program_id(0)
    nused = nused_ref[0]
    nt = tail_n_ref.shape[0]
    slot = lax.rem(j, 2)
    dump_row = yl_ref.shape[0] - DUMP_CHUNKS * ROW_CHUNK

    def rows(c):
        if isinstance(c, int):
            return pl.ds(c * ROW_CHUNK, ROW_CHUNK)
        return pl.ds(pl.multiple_of(c * ROW_CHUNK, ROW_CHUNK), ROW_CHUNK)

    def start_gather(t, sl):
        for k in range(CHUNKS_PER_TILE):
            pltpu.make_async_copy(xl_ref.at[rows(src_ref[t * CHUNKS_PER_TILE + k])],
                                  xbuf.at[sl, rows(k)], gsem.at[sl]).start()

    def start_scatter(t, sl):
        for k in range(CHUNKS_PER_TILE):
            pltpu.make_async_copy(ybuf.at[sl, rows(k)],
                                  yl_ref.at[rows(dst_ref[t * CHUNKS_PER_TILE + k])], ssem.at[sl]).start()

    def wait_gather(sl):
        pltpu.make_async_copy(xl_ref.at[pl.ds(0, TM_EXP)], xbuf.at[sl], gsem.at[sl]).wait()

    def wait_scatter(sl):
        pltpu.make_async_copy(ybuf.at[sl], yl_ref.at[pl.ds(0, TM_EXP)], ssem.at[sl]).wait()

    def for_count(n, fn):
        def body(k, c):
            fn(k)
            return c
        lax.fori_loop(0, n, body, 0)

    def zero_fill(i, k):
        return pltpu.make_async_copy(
            zbuf, yl_ref.at[pl.ds(pl.multiple_of(tail_row_ref[i] + k * ROW_CHUNK, ROW_CHUNK), ROW_CHUNK)],
            zsem)

    @pl.when(j == 0)
    def _():
        zbuf[...] = jnp.zeros(zbuf.shape, BF16)
        ybuf[...] = jnp.zeros(ybuf.shape, BF16)
        for_count(nt, lambda i: for_count(tail_n_ref[i], lambda k: zero_fill(i, k).start()))
        start_gather(0, 0)
        for sl in range(2):
            for k in range(CHUNKS_PER_TILE):
                pltpu.make_async_copy(
                    ybuf.at[sl, rows(k)],
                    yl_ref.at[pl.ds(dump_row + (sl * CHUNKS_PER_TILE + k) * ROW_CHUNK, ROW_CHUNK)],
                    ssem.at[sl]).start()

    new_expert = (j == 0) | (tile_e_ref[j] != tile_e_ref[jnp.maximum(j - 1, 0)])

    @pl.when(new_expert & (j < nused))
    def _():
        wg_bf[...] = wg_ref[0].astype(BF16)
        wu_bf[...] = wu_ref[0].astype(BF16)
        wd_bf[...] = wd_ref[0].astype(BF16)

    @pl.when(j < nused)
    def _():
        wait_gather(slot)
        wait_scatter(slot)
        start_gather(jnp.minimum(j + 1, nused - 1), 1 - slot)
        xb = xbuf[slot]
        y = None
        for c in range(2):
            cols = slice(c * (D_EXPERT // 2), (c + 1) * (D_EXPERT // 2))
            gate = jnp.dot(xb, wg_bf[:, cols], preferred_element_type=F32)
            up = jnp.dot(xb, wu_bf[:, cols], preferred_element_type=F32)
            hid = (jax.nn.silu(gate) * up).astype(BF16)
            part = jnp.dot(hid, wd_bf[cols, :], preferred_element_type=F32)
            y = part if y is None else y + part
        ybuf[slot] = y.astype(BF16)
        start_scatter(j, slot)

    @pl.when(j == nused - 1)
    def _():
        wait_gather(1 - slot)
        wait_scatter(1 - slot)
        wait_scatter(slot)
        for_count(nt, lambda i: for_count(tail_n_ref[i], lambda k: zero_fill(i, 0).wait()))


def _experts(xl, tables, w_gate, w_up, w_down, n_tiles):
    tile_e, nused, src_map, dst_map, tail_row, tail_n = tables
    wmap = lambda j, te, *_: (te[j], 0, 0)
    grid_spec = pltpu.PrefetchScalarGridSpec(
        num_scalar_prefetch=6,
        grid=(n_tiles,),
        in_specs=[
            pl.BlockSpec(memory_space=pl.ANY),
            pl.BlockSpec((1, D_MODEL, D_EXPERT), wmap),
            pl.BlockSpec((1, D_MODEL, D_EXPERT), wmap),
            pl.BlockSpec((1, D_EXPERT, D_MODEL), wmap),
        ],
        out_specs=pl.BlockSpec(memory_space=pl.ANY),
        scratch_shapes=[
            pltpu.VMEM((2, TM_EXP, D_MODEL), BF16),
            pltpu.VMEM((2, TM_EXP, D_MODEL), BF16),
            pltpu.VMEM((ROW_CHUNK, D_MODEL), BF16),
            pltpu.VMEM((D_MODEL, D_EXPERT), BF16),
            pltpu.VMEM((D_MODEL, D_EXPERT), BF16),
            pltpu.VMEM((D_EXPERT, D_MODEL), BF16),
            pltpu.SemaphoreType.DMA((2,)),
            pltpu.SemaphoreType.DMA((2,)),
            pltpu.SemaphoreType.DMA,
        ],
    )
    return pl.pallas_call(
        _expert_body,
        grid_spec=grid_spec,
        out_shape=jax.ShapeDtypeStruct((xl.shape[0] + DUMP_CHUNKS * ROW_CHUNK, D_MODEL), BF16),
        compiler_params=_cparams(1),
        name="experts",
    )(tile_e, nused, src_map, dst_map, tail_row, tail_n, xl, w_gate, w_up, w_down)


def _combine_body(h1_ref, aux_ref, yl_ref, o_ref):
    tm = h1_ref.shape[0]
    loc = yl_ref.shape[0]
    aux = aux_ref[...]
    slot1 = aux[:, 0:1].astype(I32)
    slot2 = aux[:, 1:2].astype(I32)
    sio = lax.broadcasted_iota(I32, (tm, loc), 1)
    gm = jnp.where(sio == slot1, aux[:, 2:3], 0.0) + jnp.where(sio == slot2, aux[:, 3:4], 0.0)
    o_ref[...] = h1_ref[...] + jnp.dot(gm.astype(BF16), yl_ref[...], preferred_element_type=F32)


def _combine(h1, aux, yl, nt, tm, loc):
    return pl.pallas_call(
        _combine_body,
        grid=(nt,),
        in_specs=[
            pl.BlockSpec((tm, D_MODEL), lambda i: (i, 0)),
            pl.BlockSpec((tm, LANES), lambda i: (i, 0)),
            pl.BlockSpec((loc, D_MODEL), lambda i: (i, 0)),
        ],
        out_specs=pl.BlockSpec((tm, D_MODEL), lambda i: (i, 0)),
        out_shape=jax.ShapeDtypeStruct(h1.shape, F32),
        compiler_params=_cparams(1),
        name="combine",
    )(h1, aux, yl)


def _excl_cumsum(a, axis):
    return jnp.cumsum(a, axis=axis) - a


def _expert_tables(grp, n_tiles, loc):
    nt = grp.shape[0]
    gch = grp // ROW_CHUNK
    loc_start = _excl_cumsum(gch, 1)
    used = jnp.sum(gch, axis=1)
    col = jnp.sum(gch, axis=0)
    seg = ((col + CHUNKS_PER_TILE - 1) // CHUNKS_PER_TILE) * CHUNKS_PER_TILE
    seg_end = jnp.cumsum(seg)
    off = seg_end - seg
    tile_first = jnp.arange(n_tiles, dtype=I32) * CHUNKS_PER_TILE
    tile_e = jnp.minimum(jnp.sum(seg_end[None, :] <= tile_first[:, None], axis=1), N_EXPERTS - 1)
    nused = seg_end[-1:] // CHUNKS_PER_TILE
    onehot = tile_e[:, None] == jnp.arange(N_EXPERTS, dtype=I32)[None, :]
    pick = lambda tab: jnp.sum(jnp.where(onehot[:, :, None], tab.T[None], 0), axis=1)
    pick1 = lambda vec: jnp.sum(jnp.where(onehot, vec[None, :], 0), axis=1)
    first = tile_first - pick1(off)
    nvalid = jnp.clip(pick1(col) - first, 0, CHUNKS_PER_TILE)
    cum = jnp.cumsum(gch, axis=0)
    delta = jnp.arange(nt, dtype=I32)[:, None] * (loc // ROW_CHUNK) + loc_start - (cum - gch)
    step = delta - jnp.concatenate([jnp.zeros((1, N_EXPERTS), I32), delta[:-1]], axis=0)
    cum_prev = jnp.concatenate([jnp.full((1, N_EXPERTS), -1, I32), cum[:-1]], axis=0)
    cc = first[:, None] + jnp.arange(CHUNKS_PER_TILE, dtype=I32)[None, :]
    passed = pick(cum_prev)[:, None, :] <= cc[:, :, None]
    chunk_map = cc + jnp.sum(jnp.where(passed, pick(step)[:, None, :], 0), axis=2)
    k_in_tile = jnp.arange(CHUNKS_PER_TILE, dtype=I32)[None, :]
    valid = k_in_tile < nvalid[:, None]
    zero_chunk = loc // ROW_CHUNK - 1
    dump = nt * (loc // ROW_CHUNK) + (jnp.arange(n_tiles, dtype=I32)[:, None] % 2) * CHUNKS_PER_TILE + k_in_tile
    flat = lambda a: a.reshape(n_tiles * CHUNKS_PER_TILE)
    src_map = flat(jnp.where(valid, chunk_map, zero_chunk))
    dst_map = flat(jnp.where(valid, chunk_map, dump))

    tail_row = jnp.arange(nt, dtype=I32) * loc + used * ROW_CHUNK
    tail_n = loc // ROW_CHUNK - used
    as_i32 = lambda a: a.astype(I32)
    return tuple(map(as_i32, (tile_e, nused, src_map, dst_map, tail_row, tail_n)))


def kernel(x, norm1_g, w_in, conv_w, conv_b, w_gate_a, b_gate_a, w_gate_x, b_gate_x, lru_lambda,
           lru_out_g, q_norm_g, k_norm_g, lambda_q1, lambda_k1, lambda_q2, lambda_k2, sub_norm_g,
           w_out, norm2_g, w_router_group, b_router_group, w_router_expert, b_router_expert,
           w_expert_gate, w_expert_up, w_expert_down):
    b, s, d = x.shape
    assert d == D_MODEL and norm1_g.shape[0] == 1
    t = b * s
    l = 0
    x2 = x.reshape(t, d)

    scale = HEAD_DIM ** -0.5 * LOG2E
    qkg = jnp.stack([jnp.tile(q_norm_g[l], 2) * scale, jnp.tile(k_norm_g[l], 2)]).astype(F32)
    eye = jnp.eye(LRU_BLOCKS, dtype=F32)
    blockdiag = lambda w: jnp.einsum("ncd,nm->ncmd", w, eye).reshape(LRU_WIDTH, LRU_WIDTH)
    wgate = jnp.concatenate([blockdiag(w_gate_a[l]), blockdiag(w_gate_x[l])], axis=1).astype(BF16)
    bgate = jnp.concatenate([b_gate_a[l], b_gate_x[l]])[None, :]
    lam_params = jnp.stack([lambda_q1[l], lambda_k1[l], lambda_q2[l], lambda_k2[l]])
    wr = jnp.concatenate(
        [w_router_group[l], jnp.transpose(w_router_expert[l], (1, 0, 2)).reshape(d, N_EXPERTS)], axis=1)
    wr = jnp.pad(wr, ((0, 0), (0, LANES - wr.shape[1])))
    wr_hi = wr.astype(BF16)
    wr2 = jnp.concatenate([wr_hi, (wr - wr_hi.astype(F32)).astype(BF16)], axis=1)
    br = jnp.pad(jnp.concatenate([b_router_group[l], b_router_expert[l].reshape(-1)]),
                 (0, LANES - N_GROUPS - N_EXPERTS))[None, :]

    proj = _in_proj(x2, norm1_g[l][None, :], w_in[l], qkg)
    y_att = _attention(proj, b, s, lam_params, sub_norm_g[l][None, :])
    h1, xl, aux, gt = _out_route(proj, y_att, x2, s, w_out[l], norm2_g[l][None, :], wr2, br,
                                 conv_w[l], conv_b[l][None, :], wgate, bgate,
                                 lru_lambda[l][None, :], lru_out_g[l][None, :])

    tm = min(TM_TOK, s)
    nt = t // tm
    loc = 2 * tm + N_EXPERTS * ROW_CHUNK
    max_rows = 2 * t + nt * N_EXPERTS * (ROW_CHUNK - 1) + N_EXPERTS * (TM_EXP - ROW_CHUNK)
    n_tiles = -(-max_rows // TM_EXP)
    grp = gt[:, :, 0].astype(I32)
    tables = _expert_tables(grp, n_tiles, loc)
    yl = _experts(xl, tables, w_expert_gate[l], w_expert_up[l], w_expert_down[l], n_tiles)
    out = _combine(h1, aux, yl, nt, tm, loc)
    return out.reshape(b, s, d)
```

```python
import functools
import math

import numpy as np
import jax
import jax.numpy as jnp
from jax import lax
from jax.experimental import pallas as pl
from jax.experimental.pallas import tpu as pltpu

F32 = jnp.float32
BF16 = jnp.bfloat16
I32 = jnp.int32

D_MODEL = 1024
LRU_WIDTH = 512
LRU_BLOCKS = 8
LRU_BLOCK_W = LRU_WIDTH // LRU_BLOCKS
CONV_W = 4
LRU_C = 8.0
ATT_WIDTH = 512
N_HEADS = 4
HEAD_DIM = 64
V_DIM = 128
IN_COLS = 2 * LRU_WIDTH + 3 * ATT_WIDTH
N_GROUPS = 4
EXPERTS_PER_GROUP = 4
N_EXPERTS = N_GROUPS * EXPERTS_PER_GROUP
D_EXPERT = D_MODEL // 2
CHUNK = 64
EPS = 1e-6
NEG_BIG = -1e30
LAMBDA_INIT = 0.8 - 0.6 * math.exp(-0.3 * 0)
LOG2E = math.log2(math.e)

LANES = 128
SUBLANES = 8

TM_PROJ = 512
TQ = 512
TM_TOK = 512
TM_EXP = 512
ROW_CHUNK = 2 * SUBLANES
VMEM_LIMIT = 56 * 1024 * 1024


def _cparams(n_axes):
    return pltpu.CompilerParams(
        dimension_semantics=("arbitrary",) * n_axes, vmem_limit_bytes=VMEM_LIMIT)


def _inproj_body(x_ref, g1_ref, w_ref, qkg_ref, o_ref, w_bf):
    @pl.when(pl.program_id(0) == 0)
    def _():
        w_bf[...] = w_ref[...].astype(BF16)

    x = x_ref[...]
    ms = jnp.mean(x * x, axis=-1, keepdims=True)
    hn = ((x * lax.rsqrt(ms + EPS)) * g1_ref[...]).astype(BF16)
    tm = x.shape[0]
    lo_half = lax.broadcasted_iota(I32, (tm, LANES), 1) < HEAD_DIM
    width = 512
    for c in range(IN_COLS // width):
        c0 = c * width
        acc = jnp.dot(hn, w_bf[:, c0:c0 + width], preferred_element_type=F32)
        if c in (2, 3):
            gain = qkg_ref[c - 2:c - 1, :]
            for b in range(width // LANES):
                blk = acc[:, b * LANES:(b + 1) * LANES]
                sq = blk * blk
                s_lo = jnp.sum(jnp.where(lo_half, sq, 0.0), axis=-1, keepdims=True)
                s_hi = jnp.sum(jnp.where(lo_half, 0.0, sq), axis=-1, keepdims=True)
                inv = jnp.where(lo_half,
                                lax.rsqrt(s_lo * (1.0 / HEAD_DIM) + EPS),
                                lax.rsqrt(s_hi * (1.0 / HEAD_DIM) + EPS))
                o_ref[:, c0 + b * LANES:c0 + (b + 1) * LANES] = ((blk * inv) * gain).astype(BF16)
        else:
            o_ref[:, c0:c0 + width] = acc.astype(BF16)


def _in_proj(x2, g1, w_in, qkg):
    t = x2.shape[0]
    tm = min(TM_PROJ, t)
    return pl.pallas_call(
        _inproj_body,
        grid=(t // tm,),
        in_specs=[
            pl.BlockSpec((tm, D_MODEL), lambda i: (i, 0)),
            pl.BlockSpec((1, D_MODEL), lambda i: (0, 0)),
            pl.BlockSpec((D_MODEL, IN_COLS), lambda i: (0, 0)),
            pl.BlockSpec((2, LANES), lambda i: (0, 0)),
        ],
        out_specs=pl.BlockSpec((tm, IN_COLS), lambda i: (i, 0)),
        out_shape=jax.ShapeDtypeStruct((t, IN_COLS), BF16),
        scratch_shapes=[pltpu.VMEM((D_MODEL, IN_COLS), BF16)],
        compiler_params=_cparams(1),
        name="in_proj",
    )(x2, g1, w_in, qkg)


CONV_BLOCK = 128


def _lru_tile(first, p_ref, shift_ref, cw_ref, cb_ref, wg_ref, bg_ref, lam_ref, og_ref, o_ref,
              xbuf, hbuf, hc):
    ts = p_ref.shape[0]
    half = LRU_WIDTH // 2
    ys, sumsq = [], None
    for c0 in (0, half):
        cols = slice(c0, c0 + half)
        gcols = slice(LRU_WIDTH + c0, LRU_WIDTH + c0 + half)
        x_bf = p_ref[:, cols]
        gl = p_ref[:, gcols].astype(F32)
        xbuf[0:CONV_BLOCK, cols] = jnp.where(first, jnp.zeros((), BF16), xbuf[ts:ts + CONV_BLOCK, cols])
        xbuf[CONV_BLOCK:CONV_BLOCK + ts, cols] = x_bf
        taps = []
        for j in range(CONV_W - 1):
            shifted = jnp.concatenate(
                [jnp.dot(shift_ref[j], xbuf[r0:r0 + 2 * CONV_BLOCK, cols], preferred_element_type=F32)
                 for r0 in range(0, ts, CONV_BLOCK)], axis=0)
            taps.append(shifted * cw_ref[j:j + 1, cols])
        xc = (((cb_ref[:, cols] + taps[0]) + taps[1]) + taps[2]
              + x_bf.astype(F32) * cw_ref[CONV_W - 1:CONV_W, cols])

        xc_bf = xc.astype(BF16)
        z_a = jnp.dot(xc_bf, wg_ref[cols, cols], preferred_element_type=F32) + bg_ref[:, cols]
        z_x = jnp.dot(xc_bf, wg_ref[cols, gcols], preferred_element_type=F32) + bg_ref[:, gcols]
        r = 0.5 * jnp.tanh(0.5 * z_a) + 0.5
        gi = 0.5 * jnp.tanh(0.5 * z_x) + 0.5
        nl = -lam_ref[:, cols]
        softplus = jnp.maximum(nl, 0.0) + jnp.log1p(jnp.exp(-jnp.abs(nl)))
        log_a = (-LRU_C) * r * softplus
        a = jnp.exp(log_a)
        v = -jnp.tanh(log_a) * (a * a + 1.0)
        u = jnp.where(v > 0.0, v * lax.rsqrt(v), 0.0) * (gi * xc)

        row = lax.broadcasted_iota(I32, (ts, half), 0) & (SUBLANES - 1)
        ca, cb = a, u
        for d in (1, 2, 4):
            a_sh = pltpu.roll(ca, d, axis=0)
            b_sh = pltpu.roll(cb, d, axis=0)
            take = row >= d
            cb = jnp.where(take, ca * b_sh + cb, cb)
            ca = jnp.where(take, ca * a_sh, ca)
        h = jnp.where(first, 0.0, hc[:, cols])
        for blk in range(ts // SUBLANES):
            r0 = blk * SUBLANES
            hb = ca[r0:r0 + SUBLANES, :] * h + cb[r0:r0 + SUBLANES, :]
            hbuf[r0:r0 + SUBLANES, cols] = hb
            h = hb[SUBLANES - 1:SUBLANES, :]
        hc[:, cols] = h

        y = hbuf[:, cols] * jax.nn.gelu(gl)
        ys.append(y)
        part = jnp.sum(y * y, axis=-1, keepdims=True)
        sumsq = part if sumsq is None else sumsq + part

    inv = lax.rsqrt(sumsq * (1.0 / LRU_WIDTH) + EPS)
    for k, c0 in enumerate((0, half)):
        cols = slice(c0, c0 + half)
        o_ref[:, cols] = ((ys[k] * inv) * og_ref[:, cols]).astype(o_ref.dtype)


def _conv_shift_matrices():
    t_idx = np.arange(CONV_BLOCK)[:, None]
    c_idx = np.arange(2 * CONV_BLOCK)[None, :]
    return jnp.asarray(np.stack([c_idx == CONV_BLOCK + t_idx - (CONV_W - 1) + j
                                 for j in range(CONV_W - 1)]), BF16)


SOFTMAX_ROWS = 32


def _attn_body(slope_ref, q_ref, k_ref, v_ref, bias_ref, lamp_ref, sg_ref, o_ref,
               qs_buf, s0, s1, p0, p1, a0, a1, m_buf, acc0, acc1, *, tq, nq):
    h = pl.program_id(1)
    slope = slope_ref[h]
    s_bufs, p_bufs, a_bufs, accs = (s0, s1), (p0, p1), (a0, a1), (acc0, acc1)
    lo_half = lax.broadcasted_iota(I32, (tq, LANES), 1) < HEAD_DIM
    for i in range(nq):
        q = q_ref[i * tq:(i + 1) * tq, :]
        zero = jnp.zeros_like(q)
        qs_buf[i, 0:tq, :] = jnp.where(lo_half, q, zero)
        qs_buf[i, tq:2 * tq, :] = jnp.where(lo_half, zero, q)
    ones = jnp.ones((tq, V_DIM), BF16)
    lp = lamp_ref[...]
    lam = (jnp.exp(jnp.sum(lp[0:1, :] * lp[1:2, :], axis=-1, keepdims=True))
           - jnp.exp(jnp.sum(lp[2:3, :] * lp[3:4, :], axis=-1, keepdims=True))
           + LAMBDA_INIT)
    pairs = [(i, j) for i in range(nq) for j in range(i + 1)]
    hq = tq // 2

    def row_sets(i, j):
        if j < i or hq % CHUNK:
            return [((0, 2 * tq), tq)]
        return [((0, hq), hq), ((hq, tq), tq), ((tq, tq + hq), hq), ((tq + hq, 2 * tq), tq)]

    def scores(t):
        i, j = pairs[t]
        s_buf = s_bufs[t % 2]
        nt_dims = (((1,), (1,)), ((), ()))
        if j < i or hq % CHUNK:
            s_buf[...] = lax.dot_general(qs_buf[i], k_ref[j * tq:(j + 1) * tq, :], nt_dims,
                                         preferred_element_type=F32)
            return
        s_buf[:, 0:hq] = lax.dot_general(qs_buf[i], k_ref[j * tq:j * tq + hq, :], nt_dims,
                                         preferred_element_type=F32)
        for (r0, r1), width in row_sets(i, j):
            if width == tq:
                s_buf[r0:r1, hq:tq] = lax.dot_general(
                    qs_buf[i, r0:r1, :], k_ref[j * tq + hq:(j + 1) * tq, :], nt_dims,
                    preferred_element_type=F32)

    def softmax(t):
        i, j = pairs[t]
        which = 1 if j == i else 0
        shift = slope * float(-(i - j) * tq)
        s_buf, p_buf, a_buf = s_bufs[t % 2], p_bufs[t % 2], a_bufs[t % 2]
        for (r0, r1), width in row_sets(i, j):
            for rb in range(r0, r1, SOFTMAX_ROWS):
                rows = slice(rb, rb + SOFTMAX_ROWS)
                sb = s_buf[rows, 0:width] + bias_ref[0, which, rows, 0:width]
                m_new = jnp.broadcast_to(jnp.max(sb, axis=-1, keepdims=True), (SOFTMAX_ROWS, LANES)) + shift
                if j > 0:
                    m_old = m_buf[rows, :]
                    m_new = jnp.maximum(m_old, m_new)
                    a_buf[rows, :] = jnp.exp2(m_old - m_new)
                m_sub = m_new - shift
                p_buf[rows, 0:width] = jnp.exp2(
                    sb - jnp.concatenate([m_sub] * (width // LANES), axis=1)).astype(BF16)
                m_buf[rows, :] = m_new

    def accumulate(t):
        i, j = pairs[t]
        acc = accs[i % 2]
        v_aug = jnp.concatenate([v_ref[j * tq:(j + 1) * tq, :], ones], axis=1)
        for (r0, r1), width in row_sets(i, j):
            pv = jnp.dot(p_bufs[t % 2][r0:r1, 0:width], v_aug[0:width, :], preferred_element_type=F32)
            if j == 0:
                acc[r0:r1, :] = pv
            else:
                alpha = a_bufs[t % 2][r0:r1, :]
                acc[r0:r1, :] = jnp.concatenate([alpha] * (2 * V_DIM // LANES), axis=1) * acc[r0:r1, :] + pv
        if j == i:
            o = (acc[0:tq, 0:V_DIM] / acc[0:tq, V_DIM:V_DIM + 1]
                 - lam * (acc[tq:2 * tq, 0:V_DIM] / acc[tq:2 * tq, V_DIM:V_DIM + 1]))
            ms = jnp.mean(o * o, axis=-1, keepdims=True)
            o = ((o * lax.rsqrt(ms + EPS)) * sg_ref[...]) * (1.0 - LAMBDA_INIT)
            o_ref[i * tq:(i + 1) * tq, :] = o.astype(o_ref.dtype)

    scores(0)
    for t in range(len(pairs)):
        if t + 1 < len(pairs):
            scores(t + 1)
        softmax(t)
        if t >= 1:
            accumulate(t - 1)
    accumulate(len(pairs) - 1)


def _alibi_tables(tq):
    slopes = np.exp2(-8.0 * np.arange(1, N_HEADS + 1, dtype=np.float64) / N_HEADS)
    qi = np.arange(tq)[:, None]
    kj = np.arange(tq)[None, :]
    off = -(slopes[:, None, None] * (qi - kj)[None])
    allowed = (kj // CHUNK) <= (qi // CHUNK)
    diag = np.where(allowed[None], -(slopes[:, None, None] * np.abs(qi - kj)[None]), NEG_BIG)
    tab = np.stack([off, diag], axis=1)
    tab = np.concatenate([tab, tab], axis=2)
    return jnp.asarray(tab * LOG2E, F32), jnp.asarray(slopes * LOG2E, F32)


def _attention(proj, b, s, lam_params, sub_g):
    tq = min(TQ, s)
    nq = s // tq
    bias, slopes = _alibi_tables(tq)
    qcol = 2 * LRU_WIDTH // LANES
    kcol = qcol + ATT_WIDTH // LANES
    vcol = kcol + ATT_WIDTH // LANES
    score_buf = pltpu.VMEM((2 * tq, tq), F32)
    prob_buf = pltpu.VMEM((2 * tq, tq), BF16)
    col_buf = pltpu.VMEM((2 * tq, LANES), F32)
    acc_buf = pltpu.VMEM((2 * tq, 2 * V_DIM), F32)
    grid_spec = pltpu.PrefetchScalarGridSpec(
        num_scalar_prefetch=1,
        grid=(b, N_HEADS),
        in_specs=[
            pl.BlockSpec((s, LANES), lambda bi, h, sl: (bi, qcol + h)),
            pl.BlockSpec((s, LANES), lambda bi, h, sl: (bi, kcol + h)),
            pl.BlockSpec((s, LANES), lambda bi, h, sl: (bi, vcol + h)),
            pl.BlockSpec((1, 2, 2 * tq, tq), lambda bi, h, sl: (h, 0, 0, 0)),
            pl.BlockSpec((4, HEAD_DIM), lambda bi, h, sl: (0, 0)),
            pl.BlockSpec((1, V_DIM), lambda bi, h, sl: (0, 0)),
        ],
        out_specs=pl.BlockSpec((s, V_DIM), lambda bi, h, sl: (bi, h)),
        scratch_shapes=[
            pltpu.VMEM((nq, 2 * tq, LANES), BF16),
            score_buf, score_buf, prob_buf, prob_buf, col_buf, col_buf, col_buf, acc_buf, acc_buf,
        ],
    )
    return pl.pallas_call(
        functools.partial(_attn_body, tq=tq, nq=nq),
        grid_spec=grid_spec,
        out_shape=jax.ShapeDtypeStruct((b * s, ATT_WIDTH), BF16),
        compiler_params=_cparams(2),
        name="diff_attn",
    )(slopes, proj, proj, proj, bias, lam_params, sub_g)


def _first_max4(v0, v1, v2, v3):
    m = jnp.maximum(jnp.maximum(v0, v1), jnp.maximum(v2, v3))
    idx = jnp.where(v0 == m, 0, jnp.where(v1 == m, 1, jnp.where(v2 == m, 2, 3))).astype(I32)
    return m, idx


def _mix_and_logits(yl_ref, ya_ref, x_ref, wo_bf, g2_ref, wr2_ref, br_ref, h1_ref, hn_cur, lg_cur):
    mix = (jnp.dot(yl_ref[...], wo_bf[0:LRU_WIDTH, :], preferred_element_type=F32)
           + jnp.dot(ya_ref[...], wo_bf[LRU_WIDTH:, :], preferred_element_type=F32))
    h1 = x_ref[...] + mix
    h1_ref[...] = h1
    ms = jnp.mean(h1 * h1, axis=-1, keepdims=True)
    hn = (h1 * lax.rsqrt(ms + EPS)) * g2_ref[...]
    hn_hi = hn.astype(BF16)
    hn_lo = (hn - hn_hi.astype(F32)).astype(BF16)
    hn_cur[...] = hn_hi
    hh_hl = jnp.dot(hn_hi, wr2_ref[...], preferred_element_type=F32)
    lg_cur[...] = (hh_hl[:, 0:LANES] + hh_hl[:, LANES:2 * LANES]
                   + jnp.dot(hn_lo, wr2_ref[:, 0:LANES], preferred_element_type=F32)) + br_ref[...]


def _route_and_sort(hn_ref, lg_ref, xl_ref, aux_ref, gt_ref):
    tm = hn_ref.shape[0]
    hn_hi = hn_ref[...]
    lt = lg_ref[...].T
    row = lambda n: lt[n:n + 1, :]
    gmax, gidx = _first_max4(row(0), row(1), row(2), row(3))
    zg = (jnp.exp(row(0) - gmax) + jnp.exp(row(1) - gmax)
          + jnp.exp(row(2) - gmax) + jnp.exp(row(3) - gmax))
    g_gate = 1.0 / zg
    base = N_GROUPS
    sel = [jnp.where(gidx == 0, row(base + j),
                     jnp.where(gidx == 1, row(base + 4 + j),
                               jnp.where(gidx == 2, row(base + 8 + j), row(base + 12 + j))))
           for j in range(EXPERTS_PER_GROUP)]
    m1, i1 = _first_max4(*sel)
    ze = sum(jnp.exp(sj - m1) for sj in sel)
    rest = [jnp.where(i1 == j, -jnp.inf, sel[j]) for j in range(EXPERTS_PER_GROUP)]
    m2, i2 = _first_max4(*rest)
    p1 = 1.0 / ze
    p2 = jnp.exp(m2 - m1) / ze
    gate1 = g_gate * (p1 / (p1 + p2))
    gate2 = g_gate * (p2 / (p1 + p2))
    e1 = gidx * EXPERTS_PER_GROUP + i1
    e2 = gidx * EXPERTS_PER_GROUP + i2

    eio = lax.broadcasted_iota(I32, (N_EXPERTS, tm), 0)
    oh1 = eio == e1
    oh2 = eio == e2
    both = (oh1 | oh2).astype(F32)
    cnt = jnp.sum(both, axis=1, keepdims=True)
    grp = jnp.floor((cnt + (ROW_CHUNK - 1)) * (1.0 / ROW_CHUNK)) * ROW_CHUNK
    ti = lax.broadcasted_iota(I32, (tm, tm), 0)
    tj = lax.broadcasted_iota(I32, (tm, tm), 1)
    before = (ti < tj).astype(BF16)
    rank = jnp.dot(both.astype(BF16), before, preferred_element_type=F32)
    start1 = jnp.sum(jnp.where(eio < e1, grp, 0.0), axis=0, keepdims=True)
    start2 = jnp.sum(jnp.where(eio < e2, grp, 0.0), axis=0, keepdims=True)
    slot1 = start1 + jnp.sum(jnp.where(oh1, rank, 0.0), axis=0, keepdims=True)
    slot2 = start2 + jnp.sum(jnp.where(oh2, rank, 0.0), axis=0, keepdims=True)

    loc = xl_ref.shape[0]
    sio = lax.broadcasted_iota(I32, (loc, tm), 0)
    perm = ((sio == slot1.astype(I32)) | (sio == slot2.astype(I32))).astype(BF16)
    xs = jnp.dot(perm, hn_hi, preferred_element_type=F32)
    xl_ref[...] = xs.astype(BF16)

    rio = lax.broadcasted_iota(I32, (LANES, tm), 0)
    aux_t = jnp.where(rio == 0, slot1, jnp.where(rio == 1, slot2,
                      jnp.where(rio == 2, gate1, jnp.where(rio == 3, gate2, 0.0))))
    aux_ref[...] = aux_t.T
    gt_ref[0] = jnp.broadcast_to(grp, (N_EXPERTS, LANES))


def _route_body(p_ref, ya_ref, x_ref, wo_ref, g2_ref, wr2_ref, br_ref,
                shift_ref, cw_ref, cb_ref, wg_ref, bg_ref, lam_ref, og_ref,
                h1_ref, xl_ref, aux_ref, gt_ref,
                wo_bf, yl_new, yl_hand, hn_cur, hn_prev, lg_cur, lg_prev, xbuf, hbuf, hc, *,
                tiles_per_seq, n_tiles):
    s = pl.program_id(0)

    @pl.when(s == 0)
    def _():
        wo_bf[...] = wo_ref[...].astype(BF16)
        yl_hand[...] = jnp.zeros(yl_hand.shape, BF16)
        hn_prev[...] = jnp.zeros(hn_prev.shape, BF16)
        lg_prev[...] = jnp.zeros(lg_prev.shape, F32)
        xbuf[...] = jnp.zeros(xbuf.shape, BF16)
        hc[...] = jnp.zeros(hc.shape, F32)

    _route_and_sort(hn_prev, lg_prev, xl_ref, aux_ref, gt_ref)
    _mix_and_logits(yl_hand, ya_ref, x_ref, wo_bf, g2_ref, wr2_ref, br_ref, h1_ref, hn_cur, lg_cur)
    first = lax.rem(s, tiles_per_seq) == 0
    _lru_tile(first, p_ref, shift_ref, cw_ref, cb_ref, wg_ref, bg_ref, lam_ref, og_ref, yl_new,
              xbuf, hbuf, hc)
    hn_prev[...] = hn_cur[...]
    lg_prev[...] = lg_cur[...]
    yl_hand[...] = jnp.where(s < n_tiles, yl_new[...], yl_hand[...])


def _out_route(proj, y_att, x2, seq, w_out, g2, wr2, br, conv_w, conv_b, wgate, bgate, lam, out_g):
    t = x2.shape[0]
    tm = min(TM_TOK, seq)
    nt = t // tm
    loc = 2 * tm + N_EXPERTS * ROW_CHUNK
    const = lambda shape: pl.BlockSpec(shape, lambda i: (0,) * len(shape))
    lag = lambda k: (lambda i: (jnp.clip(i - k, 0, nt - 1), 0))
    return pl.pallas_call(
        functools.partial(_route_body, tiles_per_seq=seq // tm, n_tiles=nt),
        grid=(nt + 2,),
        in_specs=[
            pl.BlockSpec((tm, 2 * LRU_WIDTH), lag(0)),
            pl.BlockSpec((tm, ATT_WIDTH), lag(1)),
            pl.BlockSpec((tm, D_MODEL), lag(1)),
            const((D_MODEL, D_MODEL)),
            const((1, D_MODEL)),
            const((D_MODEL, 2 * LANES)),
            const((1, LANES)),
            const((CONV_W - 1, CONV_BLOCK, 2 * CONV_BLOCK)),
            const((CONV_W, LRU_WIDTH)),
            const((1, LRU_WIDTH)),
            const((LRU_WIDTH, 2 * LRU_WIDTH)),
            const((1, 2 * LRU_WIDTH)),
            const((1, LRU_WIDTH)),
            const((1, LRU_WIDTH)),
        ],
        out_specs=[
            pl.BlockSpec((tm, D_MODEL), lag(1)),
            pl.BlockSpec((loc, D_MODEL), lag(2)),
            pl.BlockSpec((tm, LANES), lag(2)),
            pl.BlockSpec((1, N_EXPERTS, LANES), lambda i: (jnp.clip(i - 2, 0, nt - 1), 0, 0)),
        ],
        out_shape=[
            jax.ShapeDtypeStruct((t, D_MODEL), F32),
            jax.ShapeDtypeStruct((nt * loc, D_MODEL), BF16),
            jax.ShapeDtypeStruct((t, LANES), F32),
            jax.ShapeDtypeStruct((nt, N_EXPERTS, LANES), F32),
        ],
        scratch_shapes=[
            pltpu.VMEM((D_MODEL, D_MODEL), BF16),
            pltpu.VMEM((tm, LRU_WIDTH), BF16),
            pltpu.VMEM((tm, LRU_WIDTH), BF16),
            pltpu.VMEM((tm, D_MODEL), BF16),
            pltpu.VMEM((tm, D_MODEL), BF16),
            pltpu.VMEM((tm, LANES), F32),
            pltpu.VMEM((tm, LANES), F32),
            pltpu.VMEM((tm + CONV_BLOCK, LRU_WIDTH), BF16),
            pltpu.VMEM((tm, LRU_WIDTH), F32),
            pltpu.VMEM((1, LRU_WIDTH), F32),
        ],
        compiler_params=_cparams(1),
        name="out_route_lru",
    )(proj, y_att, x2, w_out, g2, wr2, br, _conv_shift_matrices(), conv_w, conv_b, wgate, bgate, lam,
      out_g)


CHUNKS_PER_TILE = TM_EXP // ROW_CHUNK


DUMP_CHUNKS = 2 * CHUNKS_PER_TILE


def _expert_body(tile_e_ref, nused_ref, src_ref, dst_ref, tail_row_ref, tail_n_ref,
                 xl_ref, wg_ref, wu_ref, wd_ref, yl_ref,
                 xbuf, ybuf, zbuf, wg_bf, wu_bf, wd_bf, gsem, ssem, zsem):
    j = pl.program_id(0)
    nused = nused_ref[0]
    nt = tail_n_ref.shape[0]
    slot = lax.rem(j, 2)
    dump_row = yl_ref.shape[0] - DUMP_CHUNKS * ROW_CHUNK

    def rows(c):
        if isinstance(c, int):
            return pl.ds(c * ROW_CHUNK, ROW_CHUNK)
        return pl.ds(pl.multiple_of(c * ROW_CHUNK, ROW_CHUNK), ROW_CHUNK)

    def start_gather(t, sl):
        for k in range(CHUNKS_PER_TILE):
            pltpu.make_async_copy(xl_ref.at[rows(src_ref[t * CHUNKS_PER_TILE + k])],
                                  xbuf.at[sl, rows(k)], gsem.at[sl]).start()

    def start_scatter(t, sl):
        for k in range(CHUNKS_PER_TILE):
            pltpu.make_async_copy(ybuf.at[sl, rows(k)],
                                  yl_ref.at[rows(dst_ref[t * CHUNKS_PER_TILE + k])], ssem.at[sl]).start()

    def wait_gather(sl):
        pltpu.make_async_copy(xl_ref.at[pl.ds(0, TM_EXP)], xbuf.at[sl], gsem.at[sl]).wait()

    def wait_scatter(sl):
        pltpu.make_async_copy(ybuf.at[sl], yl_ref.at[pl.ds(0, TM_EXP)], ssem.at[sl]).wait()

    def for_count(n, fn):
        def body(k, c):
            fn(k)
            return c
        lax.fori_loop(0, n, body, 0)

    def zero_fill(i, k):
        return pltpu.make_async_copy(
            zbuf, yl_ref.at[pl.ds(pl.multiple_of(tail_row_ref[i] + k * ROW_CHUNK, ROW_CHUNK), ROW_CHUNK)],
            zsem)

    @pl.when(j == 0)
    def _():
        zbuf[...] = jnp.zeros(zbuf.shape, BF16)
        ybuf[...] = jnp.zeros(ybuf.shape, BF16)
        for_count(nt, lambda i: for_count(tail_n_ref[i], lambda k: zero_fill(i, k).start()))
        start_gather(0, 0)
        for sl in range(2):
            for k in range(CHUNKS_PER_TILE):
                pltpu.make_async_copy(
                    ybuf.at[sl, rows(k)],
                    yl_ref.at[pl.ds(dump_row + (sl * CHUNKS_PER_TILE + k) * ROW_CHUNK, ROW_CHUNK)],
                    ssem.at[sl]).start()

    new_expert = (j == 0) | (tile_e_ref[j] != tile_e_ref[jnp.maximum(j - 1, 0)])

    @pl.when(new_expert & (j < nused))
    def _():
        wg_bf[...] = wg_ref[0].astype(BF16)
        wu_bf[...] = wu_ref[0].astype(BF16)
        wd_bf[...] = wd_ref[0].astype(BF16)

    @pl.when(j < nused)
    def _():
        wait_gather(slot)
        wait_scatter(slot)
        start_gather(jnp.minimum(j + 1, nused - 1), 1 - slot)
        xb = xbuf[slot]
        y = None
        for c in range(2):
            cols = slice(c * (D_EXPERT // 2), (c + 1) * (D_EXPERT // 2))
            gate = jnp.dot(xb, wg_bf[:, cols], preferred_element_type=F32)
            up = jnp.dot(xb, wu_bf[:, cols], preferred_element_type=F32)
            hid = (jax.nn.silu(gate) * up).astype(BF16)
            part = jnp.dot(hid, wd_bf[cols, :], preferred_element_type=F32)
            y = part if y is None else y + part
        ybuf[slot] = y.astype(BF16)
        start_scatter(j, slot)

    @pl.when(j == nused - 1)
    def _():
        wait_gather(1 - slot)
        wait_scatter(1 - slot)
        wait_scatter(slot)
        for_count(nt, lambda i: for_count(tail_n_ref[i], lambda k: zero_fill(i, 0).wait()))


def _experts(xl, tables, w_gate, w_up, w_down, n_tiles):
    tile_e, nused, src_map, dst_map, tail_row, tail_n = tables
    wmap = lambda j, te, *_: (te[j], 0, 0)
    grid_spec = pltpu.PrefetchScalarGridSpec(
        num_scalar_prefetch=6,
        grid=(n_tiles,),
        in_specs=[
            pl.BlockSpec(memory_space=pl.ANY),
            pl.BlockSpec((1, D_MODEL, D_EXPERT), wmap),
            pl.BlockSpec((1, D_MODEL, D_EXPERT), wmap),
            pl.BlockSpec((1, D_EXPERT, D_MODEL), wmap),
        ],
        out_specs=pl.BlockSpec(memory_space=pl.ANY),
        scratch_shapes=[
            pltpu.VMEM((2, TM_EXP, D_MODEL), BF16),
            pltpu.VMEM((2, TM_EXP, D_MODEL), BF16),
            pltpu.VMEM((ROW_CHUNK, D_MODEL), BF16),
            pltpu.VMEM((D_MODEL, D_EXPERT), BF16),
            pltpu.VMEM((D_MODEL, D_EXPERT), BF16),
            pltpu.VMEM((D_EXPERT, D_MODEL), BF16),
            pltpu.SemaphoreType.DMA((2,)),
            pltpu.SemaphoreType.DMA((2,)),
            pltpu.SemaphoreType.DMA,
        ],
    )
    return pl.pallas_call(
        _expert_body,
        grid_spec=grid_spec,
        out_shape=jax.ShapeDtypeStruct((xl.shape[0] + DUMP_CHUNKS * ROW_CHUNK, D_MODEL), BF16),
        compiler_params=_cparams(1),
        name="experts",
    )(tile_e, nused, src_map, dst_map, tail_row, tail_n, xl, w_gate, w_up, w_down)


def _combine_body(h1_ref, aux_ref, yl_ref, o_ref):
    tm = h1_ref.shape[0]
    loc = yl_ref.shape[0]
    aux = aux_ref[...]
    slot1 = aux[:, 0:1].astype(I32)
    slot2 = aux[:, 1:2].astype(I32)
    sio = lax.broadcasted_iota(I32, (tm, loc), 1)
    gm = jnp.where(sio == slot1, aux[:, 2:3], 0.0) + jnp.where(sio == slot2, aux[:, 3:4], 0.0)
    o_ref[...] = h1_ref[...] + jnp.dot(gm.astype(BF16), yl_ref[...], preferred_element_type=F32)


def _combine(h1, aux, yl, nt, tm, loc):
    return pl.pallas_call(
        _combine_body,
        grid=(nt,),
        in_specs=[
            pl.BlockSpec((tm, D_MODEL), lambda i: (i, 0)),
            pl.BlockSpec((tm, LANES), lambda i: (i, 0)),
            pl.BlockSpec((loc, D_MODEL), lambda i: (i, 0)),
        ],
        out_specs=pl.BlockSpec((tm, D_MODEL), lambda i: (i, 0)),
        out_shape=jax.ShapeDtypeStruct(h1.shape, F32),
        compiler_params=_cparams(1),
        name="combine",
    )(h1, aux, yl)


def _excl_cumsum(a, axis):
    return jnp.cumsum(a, axis=axis) - a


def _expert_tables(grp, n_tiles, loc):
    nt = grp.shape[0]
    gch = grp // ROW_CHUNK
    loc_start = _excl_cumsum(gch, 1)
    used = jnp.sum(gch, axis=1)
    col = jnp.sum(gch, axis=0)
    seg = ((col + CHUNKS_PER_TILE - 1) // CHUNKS_PER_TILE) * CHUNKS_PER_TILE
    seg_end = jnp.cumsum(seg)
    off = seg_end - seg
    tile_first = jnp.arange(n_tiles, dtype=I32) * CHUNKS_PER_TILE
    tile_e = jnp.minimum(jnp.sum(seg_end[None, :] <= tile_first[:, None], axis=1), N_EXPERTS - 1)
    nused = seg_end[-1:] // CHUNKS_PER_TILE
    onehot = tile_e[:, None] == jnp.arange(N_EXPERTS, dtype=I32)[None, :]
    pick = lambda tab: jnp.sum(jnp.where(onehot[:, :, None], tab.T[None], 0), axis=1)
    pick1 = lambda vec: jnp.sum(jnp.where(onehot, vec[None, :], 0), axis=1)
    first = tile_first - pick1(off)
    nvalid = jnp.clip(pick1(col) - first, 0, CHUNKS_PER_TILE)
    cum = jnp.cumsum(gch, axis=0)
    delta = jnp.arange(nt, dtype=I32)[:, None] * (loc // ROW_CHUNK) + loc_start - (cum - gch)
    step = delta - jnp.concatenate([jnp.zeros((1, N_EXPERTS), I32), delta[:-1]], axis=0)
    cum_prev = jnp.concatenate([jnp.full((1, N_EXPERTS), -1, I32), cum[:-1]], axis=0)
    cc = first[:, None] + jnp.arange(CHUNKS_PER_TILE, dtype=I32)[None, :]
    passed = pick(cum_prev)[:, None, :] <= cc[:, :, None]
    chunk_map = cc + jnp.sum(jnp.where(passed, pick(step)[:, None, :], 0), axis=2)
    k_in_tile = jnp.arange(CHUNKS_PER_TILE, dtype=I32)[None, :]
    valid = k_in_tile < nvalid[:, None]
    zero_chunk = loc // ROW_CHUNK - 1
    dump = nt * (loc // ROW_CHUNK) + (jnp.arange(n_tiles, dtype=I32)[:, None] % 2) * CHUNKS_PER_TILE + k_in_tile
    flat = lambda a: a.reshape(n_tiles * CHUNKS_PER_TILE)
    src_map = flat(jnp.where(valid, chunk_map, zero_chunk))
    dst_map = flat(jnp.where(valid, chunk_map, dump))

    tail_row = jnp.arange(nt, dtype=I32) * loc + used * ROW_CHUNK
    tail_n = loc // ROW_CHUNK - used
    as_i32 = lambda a: a.astype(I32)
    return tuple(map(as_i32, (tile_e, nused, src_map, dst_map, tail_row, tail_n)))


def kernel(x, norm1_g, w_in, conv_w, conv_b, w_gate_a, b_gate_a, w_gate_x, b_gate_x, lru_lambda,
           lru_out_g, q_norm_g, k_norm_g, lambda_q1, lambda_k1, lambda_q2, lambda_k2, sub_norm_g,
           w_out, norm2_g, w_router_group, b_router_group, w_router_expert, b_router_expert,
           w_expert_gate, w_expert_up, w_expert_down):
    b, s, d = x.shape
    assert d == D_MODEL and norm1_g.shape[0] == 1
    t = b * s
    l = 0
    x2 = x.reshape(t, d)

    scale = HEAD_DIM ** -0.5 * LOG2E
    qkg = jnp.stack([jnp.tile(q_norm_g[l], 2) * scale, jnp.tile(k_norm_g[l], 2)]).astype(F32)
    eye = jnp.eye(LRU_BLOCKS, dtype=F32)
    blockdiag = lambda w: jnp.einsum("ncd,nm->ncmd", w, eye).reshape(LRU_WIDTH, LRU_WIDTH)
    wgate = jnp.concatenate([blockdiag(w_gate_a[l]), blockdiag(w_gate_x[l])], axis=1).astype(BF16)
    bgate = jnp.concatenate([b_gate_a[l], b_gate_x[l]])[None, :]
    lam_params = jnp.stack([lambda_q1[l], lambda_k1[l], lambda_q2[l], lambda_k2[l]])
    wr = jnp.concatenate(
        [w_router_group[l], jnp.transpose(w_router_expert[l], (1, 0, 2)).reshape(d, N_EXPERTS)], axis=1)
    wr = jnp.pad(wr, ((0, 0), (0, LANES - wr.shape[1])))
    wr_hi = wr.astype(BF16)
    wr2 = jnp.concatenate([wr_hi, (wr - wr_hi.astype(F32)).astype(BF16)], axis=1)
    br = jnp.pad(jnp.concatenate([b_router_group[l], b_router_expert[l].reshape(-1)]),
                 (0, LANES - N_GROUPS - N_EXPERTS))[None, :]

    proj = _in_proj(x2, norm1_g[l][None, :], w_in[l], qkg)
    y_att = _attention(proj, b, s, lam_params, sub_norm_g[l][None, :])
    h1, xl, aux, gt = _out_route(proj, y_att, x2, s, w_out[l], norm2_g[l][None, :], wr2, br,
                                 conv_w[l], conv_b[l][None, :], wgate, bgate,
                                 lru_lambda[l][None, :], lru_out_g[l][None, :])

    tm = min(TM_TOK, s)
    nt = t // tm
    loc = 2 * tm + N_EXPERTS * ROW_CHUNK
    max_rows = 2 * t + nt * N_EXPERTS * (ROW_CHUNK - 1) + N_EXPERTS * (TM_EXP - ROW_CHUNK)
    n_tiles = -(-max_rows // TM_EXP)
    grp = gt[:, :, 0].astype(I32)
    tables = _expert_tables(grp, n_tiles, loc)
    yl = _experts(xl, tables, w_expert_gate[l], w_expert_up[l], w_expert_down[l], n_tiles)
    out = _combine(h1, aux, yl, nt, tm, loc)
    return out.reshape(b, s, d)
```

```python
import functools
import math

import numpy as np
import jax
import jax.numpy as jnp
from jax import lax
from jax.experimental import pallas as pl
from jax.experimental.pallas import tpu as pltpu

F32 = jnp.float32
BF16 = jnp.bfloat16
I32 = jnp.int32

D_MODEL = 1024
LRU_WIDTH = 512
LRU_BLOCKS = 8
LRU_BLOCK_W = LRU_WIDTH // LRU_BLOCKS
CONV_W = 4
LRU_C = 8.0
ATT_WIDTH = 512
N_HEADS = 4
HEAD_DIM = 64
V_DIM = 128
IN_COLS = 2 * LRU_WIDTH + 3 * ATT_WIDTH
N_GROUPS = 4
EXPERTS_PER_GROUP = 4
N_EXPERTS = N_GROUPS * EXPERTS_PER_GROUP
D_EXPERT = D_MODEL // 2
CHUNK = 64
EPS = 1e-6
NEG_BIG = -1e30
LAMBDA_INIT = 0.8 - 0.6 * math.exp(-0.3 * 0)
LOG2E = math.log2(math.e)

LANES = 128
SUBLANES = 8

TM_PROJ = 512
TQ = 512
TM_TOK = 512
TM_EXP = 512
ROW_CHUNK = 2 * SUBLANES
VMEM_LIMIT = 56 * 1024 * 1024


def _cparams(n_axes):
    return pltpu.CompilerParams(
        dimension_semantics=("arbitrary",) * n_axes, vmem_limit_bytes=VMEM_LIMIT)


def _inproj_body(x_ref, g1_ref, w_ref, qkg_ref, o_ref, w_bf):
    @pl.when(pl.program_id(0) == 0)
    def _():
        w_bf[...] = w_ref[...].astype(BF16)

    x = x_ref[...]
    ms = jnp.mean(x * x, axis=-1, keepdims=True)
    hn = ((x * lax.rsqrt(ms + EPS)) * g1_ref[...]).astype(BF16)
    tm = x.shape[0]
    lo_half = lax.broadcasted_iota(I32, (tm, LANES), 1) < HEAD_DIM
    width = 512
    for c in range(IN_COLS // width):
        c0 = c * width
        acc = jnp.dot(hn, w_bf[:, c0:c0 + width], preferred_element_type=F32)
        if c in (2, 3):
            gain = qkg_ref[c - 2:c - 1, :]
            for b in range(width // LANES):
                blk = acc[:, b * LANES:(b + 1) * LANES]
                sq = blk * blk
                s_lo = jnp.sum(jnp.where(lo_half, sq, 0.0), axis=-1, keepdims=True)
                s_hi = jnp.sum(jnp.where(lo_half, 0.0, sq), axis=-1, keepdims=True)
                inv = jnp.where(lo_half,
                                lax.rsqrt(s_lo * (1.0 / HEAD_DIM) + EPS),
                                lax.rsqrt(s_hi * (1.0 / HEAD_DIM) + EPS))
                o_ref[:, c0 + b * LANES:c0 + (b + 1) * LANES] = ((blk * inv) * gain).astype(BF16)
        else:
            o_ref[:, c0:c0 + width] = acc.astype(BF16)


def _in_proj(x2, g1, w_in, qkg):
    t = x2.shape[0]
    tm = min(TM_PROJ, t)
    return pl.pallas_call(
        _inproj_body,
        grid=(t // tm,),
        in_specs=[
            pl.BlockSpec((tm, D_MODEL), lambda i: (i, 0)),
            pl.BlockSpec((1, D_MODEL), lambda i: (0, 0)),
            pl.BlockSpec((D_MODEL, IN_COLS), lambda i: (0, 0)),
            pl.BlockSpec((2, LANES), lambda i: (0, 0)),
        ],
        out_specs=pl.BlockSpec((tm, IN_COLS), lambda i: (i, 0)),
        out_shape=jax.ShapeDtypeStruct((t, IN_COLS), BF16),
        scratch_shapes=[pltpu.VMEM((D_MODEL, IN_COLS), BF16)],
        compiler_params=_cparams(1),
        name="in_proj",
    )(x2, g1, w_in, qkg)


CONV_BLOCK = 128


def _lru_tile(first, p_ref, shift_ref, cw_ref, cb_ref, wg_ref, bg_ref, lam_ref, og_ref, o_ref,
              xbuf, hbuf, hc):
    ts = p_ref.shape[0]
    half = LRU_WIDTH // 2
    assert half % LRU_BLOCK_W == 0
    ys, sumsq = [], None
    for c0 in (0, half):
        cols = slice(c0, c0 + half)
        gcols = slice(LRU_WIDTH + c0, LRU_WIDTH + c0 + half)
        x_bf = p_ref[:, cols]
        gl = p_ref[:, gcols].astype(F32)
        xbuf[0:CONV_BLOCK, cols] = jnp.where(first, jnp.zeros((), BF16), xbuf[ts:ts + CONV_BLOCK, cols])
        xbuf[CONV_BLOCK:CONV_BLOCK + ts, cols] = x_bf
        taps = []
        for j in range(CONV_W - 1):
            shifted = jnp.concatenate(
                [jnp.dot(shift_ref[j], xbuf[r0:r0 + 2 * CONV_BLOCK, cols], preferred_element_type=F32)
                 for r0 in range(0, ts, CONV_BLOCK)], axis=0)
            taps.append(shifted * cw_ref[j:j + 1, cols])
        xc = (((cb_ref[:, cols] + taps[0]) + taps[1]) + taps[2]
              + x_bf.astype(F32) * cw_ref[CONV_W - 1:CONV_W, cols])

        xc_bf = xc.astype(BF16)
        z_a = jnp.dot(xc_bf, wg_ref[cols, cols], preferred_element_type=F32) + bg_ref[:, cols]
        z_x = jnp.dot(xc_bf, wg_ref[cols, gcols], preferred_element_type=F32) + bg_ref[:, gcols]
        r = 0.5 * jnp.tanh(0.5 * z_a) + 0.5
        gi = 0.5 * jnp.tanh(0.5 * z_x) + 0.5
        nl = -lam_ref[:, cols]
        softplus = jnp.maximum(nl, 0.0) + jnp.log1p(jnp.exp(-jnp.abs(nl)))
        log_a = (-LRU_C) * r * softplus
        a = jnp.exp(log_a)
        v = -jnp.tanh(log_a) * (a * a + 1.0)
        u = jnp.where(v > 0.0, v * lax.rsqrt(v), 0.0) * (gi * xc)

        row = lax.broadcasted_iota(I32, (ts, half), 0) & (SUBLANES - 1)
        ca, cb = a, u
        for d in (1, 2, 4):
            a_sh = pltpu.roll(ca, d, axis=0)
            b_sh = pltpu.roll(cb, d, axis=0)
            take = row >= d
            cb = jnp.where(take, ca * b_sh + cb, cb)
            ca = jnp.where(take, ca * a_sh, ca)
        h = jnp.where(first, 0.0, hc[:, cols])
        for blk in range(ts // SUBLANES):
            r0 = blk * SUBLANES
            hb = ca[r0:r0 + SUBLANES, :] * h + cb[r0:r0 + SUBLANES, :]
            hbuf[r0:r0 + SUBLANES, cols] = hb
            h = hb[SUBLANES - 1:SUBLANES, :]
        hc[:, cols] = h

        y = hbuf[:, cols] * jax.nn.gelu(gl)
        ys.append(y)
        part = jnp.sum(y * y, axis=-1, keepdims=True)
        sumsq = part if sumsq is None else sumsq + part

    inv = lax.rsqrt(sumsq * (1.0 / LRU_WIDTH) + EPS)
    for k, c0 in enumerate((0, half)):
        cols = slice(c0, c0 + half)
        o_ref[:, cols] = ((ys[k] * inv) * og_ref[:, cols]).astype(o_ref.dtype)


def _conv_shift_matrices():
    t_idx = np.arange(CONV_BLOCK)[:, None]
    c_idx = np.arange(2 * CONV_BLOCK)[None, :]
    return jnp.asarray(np.stack([c_idx == CONV_BLOCK + t_idx - (CONV_W - 1) + j
                                 for j in range(CONV_W - 1)]), BF16)


SOFTMAX_ROWS = 32


def _attn_body(slope_ref, q_ref, k_ref, v_ref, bias_ref, lamp_ref, sg_ref, o_ref,
               qs_buf, s0, s1, p0, p1, a0, a1, m_buf, acc0, acc1, *, tq, nq):
    h = pl.program_id(1)
    slope = slope_ref[h]
    s_bufs, p_bufs, a_bufs, accs = (s0, s1), (p0, p1), (a0, a1), (acc0, acc1)
    lo_half = lax.broadcasted_iota(I32, (tq, LANES), 1) < HEAD_DIM
    for i in range(nq):
        q = q_ref[i * tq:(i + 1) * tq, :]
        zero = jnp.zeros_like(q)
        qs_buf[i, 0:tq, :] = jnp.where(lo_half, q, zero)
        qs_buf[i, tq:2 * tq, :] = jnp.where(lo_half, zero, q)
    ones = jnp.ones((tq, V_DIM), BF16)
    lp = lamp_ref[...]
    lam = (jnp.exp(jnp.sum(lp[0:1, :] * lp[1:2, :], axis=-1, keepdims=True))
           - jnp.exp(jnp.sum(lp[2:3, :] * lp[3:4, :], axis=-1, keepdims=True))
           + LAMBDA_INIT)
    pairs = [(i, j) for i in range(nq) for j in range(i + 1)]
    hq = tq // 2

    def row_sets(i, j):
        if j < i or hq % CHUNK:
            return [((0, 2 * tq), tq)]
        return [((0, hq), hq), ((hq, tq), tq), ((tq, tq + hq), hq), ((tq + hq, 2 * tq), tq)]

    def scores(t):
        i, j = pairs[t]
        s_buf = s_bufs[t % 2]
        nt_dims = (((1,), (1,)), ((), ()))
        if j < i or hq % CHUNK:
            s_buf[...] = lax.dot_general(qs_buf[i], k_ref[j * tq:(j + 1) * tq, :], nt_dims,
                                         preferred_element_type=F32)
            return
        s_buf[:, 0:hq] = lax.dot_general(qs_buf[i], k_ref[j * tq:j * tq + hq, :], nt_dims,
                                         preferred_element_type=F32)
        for (r0, r1), width in row_sets(i, j):
            if width == tq:
                s_buf[r0:r1, hq:tq] = lax.dot_general(
                    qs_buf[i, r0:r1, :], k_ref[j * tq + hq:(j + 1) * tq, :], nt_dims,
                    preferred_element_type=F32)

    def softmax(t):
        i, j = pairs[t]
        which = 1 if j == i else 0
        shift = slope * float(-(i - j) * tq)
        s_buf, p_buf, a_buf = s_bufs[t % 2], p_bufs[t % 2], a_bufs[t % 2]
        for (r0, r1), width in row_sets(i, j):
            for rb in range(r0, r1, SOFTMAX_ROWS):
                rows = slice(rb, rb + SOFTMAX_ROWS)
                sb = s_buf[rows, 0:width] + bias_ref[0, which, rows, 0:width]
                m_new = jnp.broadcast_to(jnp.max(sb, axis=-1, keepdims=True), (SOFTMAX_ROWS, LANES)) + shift
                if j > 0:
                    m_old = m_buf[rows, :]
                    m_new = jnp.maximum(m_old, m_new)
                    a_buf[rows, :] = jnp.exp2(m_old - m_new)
                m_sub = m_new - shift
                p_buf[rows, 0:width] = jnp.exp2(
                    sb - jnp.concatenate([m_sub] * (width // LANES), axis=1)).astype(BF16)
                m_buf[rows, :] = m_new

    def accumulate(t):
        i, j = pairs[t]
        acc = accs[i % 2]
        v_aug = jnp.concatenate([v_ref[j * tq:(j + 1) * tq, :], ones], axis=1)
        for (r0, r1), width in row_sets(i, j):
            pv = jnp.dot(p_bufs[t % 2][r0:r1, 0:width], v_aug[0:width, :], preferred_element_type=F32)
            if j == 0:
                acc[r0:r1, :] = pv
            else:
                alpha = a_bufs[t % 2][r0:r1, :]
                acc[r0:r1, :] = jnp.concatenate([alpha] * (2 * V_DIM // LANES), axis=1) * acc[r0:r1, :] + pv
        if j == i:
            o = (acc[0:tq, 0:V_DIM] / acc[0:tq, V_DIM:V_DIM + 1]
                 - lam * (acc[tq:2 * tq, 0:V_DIM] / acc[tq:2 * tq, V_DIM:V_DIM + 1]))
            ms = jnp.mean(o * o, axis=-1, keepdims=True)
            o = ((o * lax.rsqrt(ms + EPS)) * sg_ref[...]) * (1.0 - LAMBDA_INIT)
            o_ref[i * tq:(i + 1) * tq, :] = o.astype(o_ref.dtype)

    scores(0)
    for t in range(len(pairs)):
        if t + 1 < len(pairs):
            scores(t + 1)
        softmax(t)
        if t >= 1:
            accumulate(t - 1)
    accumulate(len(pairs) - 1)


def _alibi_tables(tq):
    slopes = np.exp2(-8.0 * np.arange(1, N_HEADS + 1, dtype=np.float64) / N_HEADS)
    qi = np.arange(tq)[:, None]
    kj = np.arange(tq)[None, :]
    off = -(slopes[:, None, None] * (qi - kj)[None])
    allowed = (kj // CHUNK) <= (qi // CHUNK)
    diag = np.where(allowed[None], -(slopes[:, None, None] * np.abs(qi - kj)[None]), NEG_BIG)
    tab = np.stack([off, diag], axis=1)
    tab = np.concatenate([tab, tab], axis=2)
    return jnp.asarray(tab * LOG2E, F32), jnp.asarray(slopes * LOG2E, F32)


def _attention(proj, b, s, lam_params, sub_g):
    tq = min(TQ, s)
    nq = s // tq
    bias, slopes = _alibi_tables(tq)
    qcol = 2 * LRU_WIDTH // LANES
    kcol = qcol + ATT_WIDTH // LANES
    vcol = kcol + ATT_WIDTH // LANES
    score_buf = pltpu.VMEM((2 * tq, tq), F32)
    prob_buf = pltpu.VMEM((2 * tq, tq), BF16)
    col_buf = pltpu.VMEM((2 * tq, LANES), F32)
    acc_buf = pltpu.VMEM((2 * tq, 2 * V_DIM), F32)
    grid_spec = pltpu.PrefetchScalarGridSpec(
        num_scalar_prefetch=1,
        grid=(b, N_HEADS),
        in_specs=[
            pl.BlockSpec((s, LANES), lambda bi, h, sl: (bi, qcol + h)),
            pl.BlockSpec((s, LANES), lambda bi, h, sl: (bi, kcol + h)),
            pl.BlockSpec((s, LANES), lambda bi, h, sl: (bi, vcol + h)),
            pl.BlockSpec((1, 2, 2 * tq, tq), lambda bi, h, sl: (h, 0, 0, 0)),
            pl.BlockSpec((4, HEAD_DIM), lambda bi, h, sl: (0, 0)),
            pl.BlockSpec((1, V_DIM), lambda bi, h, sl: (0, 0)),
        ],
        out_specs=pl.BlockSpec((s, V_DIM), lambda bi, h, sl: (bi, h)),
        scratch_shapes=[
            pltpu.VMEM((nq, 2 * tq, LANES), BF16),
            score_buf, score_buf, prob_buf, prob_buf, col_buf, col_buf, col_buf, acc_buf, acc_buf,
        ],
    )
    return pl.pallas_call(
        functools.partial(_attn_body, tq=tq, nq=nq),
        grid_spec=grid_spec,
        out_shape=jax.ShapeDtypeStruct((b * s, ATT_WIDTH), BF16),
        compiler_params=_cparams(2),
        name="diff_attn",
    )(slopes, proj, proj, proj, bias, lam_params, sub_g)


def _first_max4(v0, v1, v2, v3):
    m = jnp.maximum(jnp.maximum(v0, v1), jnp.maximum(v2, v3))
    idx = jnp.where(v0 == m, 0, jnp.where(v1 == m, 1, jnp.where(v2 == m, 2, 3))).astype(I32)
    return m, idx


def _mix_and_logits(yl_ref, ya_ref, x_ref, wo_bf, g2_ref, wr2_ref, br_ref, h1_ref, hn_cur, lg_cur):
    mix = (jnp.dot(yl_ref[...], wo_bf[0:LRU_WIDTH, :], preferred_element_type=F32)
           + jnp.dot(ya_ref[...], wo_bf[LRU_WIDTH:, :], preferred_element_type=F32))
    h1 = x_ref[...] + mix
    h1_ref[...] = h1
    ms = jnp.mean(h1 * h1, axis=-1, keepdims=True)
    hn = (h1 * lax.rsqrt(ms + EPS)) * g2_ref[...]
    hn_hi = hn.astype(BF16)
    hn_lo = (hn - hn_hi.astype(F32)).astype(BF16)
    hn_cur[...] = hn_hi
    hh_hl = jnp.dot(hn_hi, wr2_ref[...], preferred_element_type=F32)
    lg_cur[...] = (hh_hl[:, 0:LANES] + hh_hl[:, LANES:2 * LANES]
                   + jnp.dot(hn_lo, wr2_ref[:, 0:LANES], preferred_element_type=F32)) + br_ref[...]


def _route_and_sort(hn_ref, lg_ref, xl_ref, aux_ref, gt_ref):
    tm = hn_ref.shape[0]
    hn_hi = hn_ref[...]
    lt = lg_ref[...].T
    row = lambda n: lt[n:n + 1, :]
    gmax, gidx = _first_max4(row(0), row(1), row(2), row(3))
    zg = (jnp.exp(row(0) - gmax) + jnp.exp(row(1) - gmax)
          + jnp.exp(row(2) - gmax) + jnp.exp(row(3) - gmax))
    g_gate = 1.0 / zg
    base = N_GROUPS
    sel = [jnp.where(gidx == 0, row(base + j),
                     jnp.where(gidx == 1, row(base + 4 + j),
                               jnp.where(gidx == 2, row(base + 8 + j), row(base + 12 + j))))
           for j in range(EXPERTS_PER_GROUP)]
    m1, i1 = _first_max4(*sel)
    ze = sum(jnp.exp(sj - m1) for sj in sel)
    rest = [jnp.where(i1 == j, -jnp.inf, sel[j]) for j in range(EXPERTS_PER_GROUP)]
    m2, i2 = _first_max4(*rest)
    p1 = 1.0 / ze
    p2 = jnp.exp(m2 - m1) / ze
    gate1 = g_gate * (p1 / (p1 + p2))
    gate2 = g_gate * (p2 / (p1 + p2))
    e1 = gidx * EXPERTS_PER_GROUP + i1
    e2 = gidx * EXPERTS_PER_GROUP + i2

    eio = lax.broadcasted_iota(I32, (N_EXPERTS, tm), 0)
    oh1 = eio == e1
    oh2 = eio == e2
    both = (oh1 | oh2).astype(F32)
    cnt = jnp.sum(both, axis=1, keepdims=True)
    grp = jnp.floor((cnt + (ROW_CHUNK - 1)) * (1.0 / ROW_CHUNK)) * ROW_CHUNK
    ti = lax.broadcasted_iota(I32, (tm, tm), 0)
    tj = lax.broadcasted_iota(I32, (tm, tm), 1)
    before = (ti < tj).astype(BF16)
    rank = jnp.dot(both.astype(BF16), before, preferred_element_type=F32)
    start1 = jnp.sum(jnp.where(eio < e1, grp, 0.0), axis=0, keepdims=True)
    start2 = jnp.sum(jnp.where(eio < e2, grp, 0.0), axis=0, keepdims=True)
    slot1 = start1 + jnp.sum(jnp.where(oh1, rank, 0.0), axis=0, keepdims=True)
    slot2 = start2 + jnp.sum(jnp.where(oh2, rank, 0.0), axis=0, keepdims=True)

    loc = xl_ref.shape[0]
    sio = lax.broadcasted_iota(I32, (loc, tm), 0)
    perm = ((sio == slot1.astype(I32)) | (sio == slot2.astype(I32))).astype(BF16)
    xs = jnp.dot(perm, hn_hi, preferred_element_type=F32)
    xl_ref[...] = xs.astype(BF16)

    rio = lax.broadcasted_iota(I32, (LANES, tm), 0)
    aux_t = jnp.where(rio == 0, slot1, jnp.where(rio == 1, slot2,
                      jnp.where(rio == 2, gate1, jnp.where(rio == 3, gate2, 0.0))))
    aux_ref[...] = aux_t.T
    gt_ref[0] = jnp.broadcast_to(grp, (N_EXPERTS, LANES))


def _route_body(p_ref, ya_ref, x_ref, wo_ref, g2_ref, wr2_ref, br_ref,
                shift_ref, cw_ref, cb_ref, wg_ref, bg_ref, lam_ref, og_ref,
                h1_ref, xl_ref, aux_ref, gt_ref,
                wo_bf, yl_new, yl_hand, hn_cur, hn_prev, lg_cur, lg_prev, xbuf, hbuf, hc, *,
                tiles_per_seq, n_tiles):
    s = pl.program_id(0)

    @pl.when(s == 0)
    def _():
        wo_bf[...] = wo_ref[...].astype(BF16)
        yl_hand[...] = jnp.zeros(yl_hand.shape, BF16)
        hn_prev[...] = jnp.zeros(hn_prev.shape, BF16)
        lg_prev[...] = jnp.zeros(lg_prev.shape, F32)
        xbuf[...] = jnp.zeros(xbuf.shape, BF16)
        hc[...] = jnp.zeros(hc.shape, F32)

    _route_and_sort(hn_prev, lg_prev, xl_ref, aux_ref, gt_ref)
    _mix_and_logits(yl_hand, ya_ref, x_ref, wo_bf, g2_ref, wr2_ref, br_ref, h1_ref, hn_cur, lg_cur)
    first = lax.rem(s, tiles_per_seq) == 0
    _lru_tile(first, p_ref, shift_ref, cw_ref, cb_ref, wg_ref, bg_ref, lam_ref, og_ref, yl_new,
              xbuf, hbuf, hc)
    hn_prev[...] = hn_cur[...]
    lg_prev[...] = lg_cur[...]
    yl_hand[...] = jnp.where(s < n_tiles, yl_new[...], yl_hand[...])


def _out_route(proj, y_att, x2, seq, w_out, g2, wr2, br, conv_w, conv_b, wgate, bgate, lam, out_g):
    t = x2.shape[0]
    tm = min(TM_TOK, seq)
    nt = t // tm
    loc = 2 * tm + N_EXPERTS * ROW_CHUNK
    const = lambda shape: pl.BlockSpec(shape, lambda i: (0,) * len(shape))
    lag = lambda k: (lambda i: (jnp.clip(i - k, 0, nt - 1), 0))
    return pl.pallas_call(
        functools.partial(_route_body, tiles_per_seq=seq // tm, n_tiles=nt),
        grid=(nt + 2,),
        in_specs=[
            pl.BlockSpec((tm, 2 * LRU_WIDTH), lag(0)),
            pl.BlockSpec((tm, ATT_WIDTH), lag(1)),
            pl.BlockSpec((tm, D_MODEL), lag(1)),
            const((D_MODEL, D_MODEL)),
            const((1, D_MODEL)),
            const((D_MODEL, 2 * LANES)),
            const((1, LANES)),
            const((CONV_W - 1, CONV_BLOCK, 2 * CONV_BLOCK)),
            const((CONV_W, LRU_WIDTH)),
            const((1, LRU_WIDTH)),
            const((LRU_WIDTH, 2 * LRU_WIDTH)),
            const((1, 2 * LRU_WIDTH)),
            const((1, LRU_WIDTH)),
            const((1, LRU_WIDTH)),
        ],
        out_specs=[
            pl.BlockSpec((tm, D_MODEL), lag(1)),
            pl.BlockSpec((loc, D_MODEL), lag(2)),
            pl.BlockSpec((tm, LANES), lag(2)),
            pl.BlockSpec((1, N_EXPERTS, LANES), lambda i: (jnp.clip(i - 2, 0, nt - 1), 0, 0)),
        ],
        out_shape=[
            jax.ShapeDtypeStruct((t, D_MODEL), F32),
            jax.ShapeDtypeStruct((nt * loc, D_MODEL), BF16),
            jax.ShapeDtypeStruct((t, LANES), F32),
            jax.ShapeDtypeStruct((nt, N_EXPERTS, LANES), F32),
        ],
        scratch_shapes=[
            pltpu.VMEM((D_MODEL, D_MODEL), BF16),
            pltpu.VMEM((tm, LRU_WIDTH), BF16),
            pltpu.VMEM((tm, LRU_WIDTH), BF16),
            pltpu.VMEM((tm, D_MODEL), BF16),
            pltpu.VMEM((tm, D_MODEL), BF16),
            pltpu.VMEM((tm, LANES), F32),
            pltpu.VMEM((tm, LANES), F32),
            pltpu.VMEM((tm + CONV_BLOCK, LRU_WIDTH), BF16),
            pltpu.VMEM((tm, LRU_WIDTH), F32),
            pltpu.VMEM((1, LRU_WIDTH), F32),
        ],
        compiler_params=_cparams(1),
        name="out_route_lru",
    )(proj, y_att, x2, w_out, g2, wr2, br, _conv_shift_matrices(), conv_w, conv_b, wgate, bgate, lam,
      out_g)


CHUNKS_PER_TILE = TM_EXP // ROW_CHUNK


DUMP_CHUNKS = 2 * CHUNKS_PER_TILE


def _expert_body(tile_e_ref, nused_ref, src_ref, dst_ref, tail_row_ref, tail_n_ref,
                 xl_ref, wg_ref, wu_ref, wd_ref, yl_ref,
                 xbuf, ybuf, zbuf, wg_bf, wu_bf, wd_bf, gsem, ssem, zsem):
    j = pl.program_id(0)
    nused = nused_ref[0]
    nt = tail_n_ref.shape[0]
    slot = lax.rem(j, 2)
    dump_row = yl_ref.shape[0] - DUMP_CHUNKS * ROW_CHUNK

    def rows(c):
        if isinstance(c, int):
            return pl.ds(c * ROW_CHUNK, ROW_CHUNK)
        return pl.ds(pl.multiple_of(c * ROW_CHUNK, ROW_CHUNK), ROW_CHUNK)

    def start_gather(t, sl):
        for k in range(CHUNKS_PER_TILE):
            pltpu.make_async_copy(xl_ref.at[rows(src_ref[t * CHUNKS_PER_TILE + k])],
                                  xbuf.at[sl, rows(k)], gsem.at[sl]).start()

    def start_scatter(t, sl):
        for k in range(CHUNKS_PER_TILE):
            pltpu.make_async_copy(ybuf.at[sl, rows(k)],
                                  yl_ref.at[rows(dst_ref[t * CHUNKS_PER_TILE + k])], ssem.at[sl]).start()

    def wait_gather(sl):
        pltpu.make_async_copy(xl_ref.at[pl.ds(0, TM_EXP)], xbuf.at[sl], gsem.at[sl]).wait()

    def wait_scatter(sl):
        pltpu.make_async_copy(ybuf.at[sl], yl_ref.at[pl.ds(0, TM_EXP)], ssem.at[sl]).wait()

    def for_count(n, fn):
        def body(k, c):
            fn(k)
            return c
        lax.fori_loop(0, n, body, 0)

    def zero_fill(i, k):
        return pltpu.make_async_copy(
            zbuf, yl_ref.at[pl.ds(pl.multiple_of(tail_row_ref[i] + k * ROW_CHUNK, ROW_CHUNK), ROW_CHUNK)],
            zsem)

    @pl.when(j == 0)
    def _():
        zbuf[...] = jnp.zeros(zbuf.shape, BF16)
        ybuf[...] = jnp.zeros(ybuf.shape, BF16)
        for_count(nt, lambda i: for_count(tail_n_ref[i], lambda k: zero_fill(i, k).start()))
        start_gather(0, 0)
        for sl in range(2):
            for k in range(CHUNKS_PER_TILE):
                pltpu.make_async_copy(
                    ybuf.at[sl, rows(k)],
                    yl_ref.at[pl.ds(dump_row + (sl * CHUNKS_PER_TILE + k) * ROW_CHUNK, ROW_CHUNK)],
                    ssem.at[sl]).start()

    new_expert = (j == 0) | (tile_e_ref[j] != tile_e_ref[jnp.maximum(j - 1, 0)])

    @pl.when(new_expert & (j < nused))
    def _():
        wg_bf[...] = wg_ref[0].astype(BF16)
        wu_bf[...] = wu_ref[0].astype(BF16)
        wd_bf[...] = wd_ref[0].astype(BF16)

    @pl.when(j < nused)
    def _():
        wait_gather(slot)
        wait_scatter(slot)
        start_gather(jnp.minimum(j + 1, nused - 1), 1 - slot)
        xb = xbuf[slot]
        y = None
        for c in range(2):
            cols = slice(c * (D_EXPERT // 2), (c + 1) * (D_EXPERT // 2))
            gate = jnp.dot(xb, wg_bf[:, cols], preferred_element_type=F32)
            up = jnp.dot(xb, wu_bf[:, cols], preferred_element_type=F32)
            hid = (jax.nn.silu(gate) * up).astype(BF16)
            part = jnp.dot(hid, wd_bf[cols, :], preferred_element_type=F32)
            y = part if y is None else y + part
        ybuf[slot] = y.astype(BF16)
        start_scatter(j, slot)

    @pl.when(j == nused - 1)
    def _():
        wait_gather(1 - slot)
        wait_scatter(1 - slot)
        wait_scatter(slot)
        for_count(nt, lambda i: for_count(tail_n_ref[i], lambda k: zero_fill(i, 0).wait()))


def _experts(xl, tables, w_gate, w_up, w_down, n_tiles):
    tile_e, nused, src_map, dst_map, tail_row, tail_n = tables
    wmap = lambda j, te, *_: (te[j], 0, 0)
    grid_spec = pltpu.PrefetchScalarGridSpec(
        num_scalar_prefetch=6,
        grid=(n_tiles,),
        in_specs=[
            pl.BlockSpec(memory_space=pl.ANY),
            pl.BlockSpec((1, D_MODEL, D_EXPERT), wmap),
            pl.BlockSpec((1, D_MODEL, D_EXPERT), wmap),
            pl.BlockSpec((1, D_EXPERT, D_MODEL), wmap),
        ],
        out_specs=pl.BlockSpec(memory_space=pl.ANY),
        scratch_shapes=[
            pltpu.VMEM((2, TM_EXP, D_MODEL), BF16),
            pltpu.VMEM((2, TM_EXP, D_MODEL), BF16),
            pltpu.VMEM((ROW_CHUNK, D_MODEL), BF16),
            pltpu.VMEM((D_MODEL, D_EXPERT), BF16),
            pltpu.VMEM((D_MODEL, D_EXPERT), BF16),
            pltpu.VMEM((D_EXPERT, D_MODEL), BF16),
            pltpu.SemaphoreType.DMA((2,)),
            pltpu.SemaphoreType.DMA((2,)),
            pltpu.SemaphoreType.DMA,
        ],
    )
    return pl.pallas_call(
        _expert_body,
        grid_spec=grid_spec,
        out_shape=jax.ShapeDtypeStruct((xl.shape[0] + DUMP_CHUNKS * ROW_CHUNK, D_MODEL), BF16),
        compiler_params=_cparams(1),
        name="experts",
    )(tile_e, nused, src_map, dst_map, tail_row, tail_n, xl, w_gate, w_up, w_down)


def _combine_body(h1_ref, aux_ref, yl_ref, o_ref):
    tm = h1_ref.shape[0]
    loc = yl_ref.shape[0]
    aux = aux_ref[...]
    slot1 = aux[:, 0:1].astype(I32)
    slot2 = aux[:, 1:2].astype(I32)
    sio = lax.broadcasted_iota(I32, (tm, loc), 1)
    gm = jnp.where(sio == slot1, aux[:, 2:3], 0.0) + jnp.where(sio == slot2, aux[:, 3:4], 0.0)
    o_ref[...] = h1_ref[...] + jnp.dot(gm.astype(BF16), yl_ref[...], preferred_element_type=F32)


def _combine(h1, aux, yl, nt, tm, loc):
    return pl.pallas_call(
        _combine_body,
        grid=(nt,),
        in_specs=[
            pl.BlockSpec((tm, D_MODEL), lambda i: (i, 0)),
            pl.BlockSpec((tm, LANES), lambda i: (i, 0)),
            pl.BlockSpec((loc, D_MODEL), lambda i: (i, 0)),
        ],
        out_specs=pl.BlockSpec((tm, D_MODEL), lambda i: (i, 0)),
        out_shape=jax.ShapeDtypeStruct(h1.shape, F32),
        compiler_params=_cparams(1),
        name="combine",
    )(h1, aux, yl)


def _excl_cumsum(a, axis):
    return jnp.cumsum(a, axis=axis) - a


def _expert_tables(grp, n_tiles, loc):
    nt = grp.shape[0]
    gch = grp // ROW_CHUNK
    loc_start = _excl_cumsum(gch, 1)
    used = jnp.sum(gch, axis=1)
    col = jnp.sum(gch, axis=0)
    seg = ((col + CHUNKS_PER_TILE - 1) // CHUNKS_PER_TILE) * CHUNKS_PER_TILE
    seg_end = jnp.cumsum(seg)
    off = seg_end - seg
    tile_first = jnp.arange(n_tiles, dtype=I32) * CHUNKS_PER_TILE
    tile_e = jnp.minimum(jnp.sum(seg_end[None, :] <= tile_first[:, None], axis=1), N_EXPERTS - 1)
    nused = seg_end[-1:] // CHUNKS_PER_TILE
    onehot = tile_e[:, None] == jnp.arange(N_EXPERTS, dtype=I32)[None, :]
    pick = lambda tab: jnp.sum(jnp.where(onehot[:, :, None], tab.T[None], 0), axis=1)
    pick1 = lambda vec: jnp.sum(jnp.where(onehot, vec[None, :], 0), axis=1)
    first = tile_first - pick1(off)
    nvalid = jnp.clip(pick1(col) - first, 0, CHUNKS_PER_TILE)
    cum = jnp.cumsum(gch, axis=0)
    delta = jnp.arange(nt, dtype=I32)[:, None] * (loc // ROW_CHUNK) + loc_start - (cum - gch)
    step = delta - jnp.concatenate([jnp.zeros((1, N_EXPERTS), I32), delta[:-1]], axis=0)
    cum_prev = jnp.concatenate([jnp.full((1, N_EXPERTS), -1, I32), cum[:-1]], axis=0)
    cc = first[:, None] + jnp.arange(CHUNKS_PER_TILE, dtype=I32)[None, :]
    passed = pick(cum_prev)[:, None, :] <= cc[:, :, None]
    chunk_map = cc + jnp.sum(jnp.where(passed, pick(step)[:, None, :], 0), axis=2)
    k_in_tile = jnp.arange(CHUNKS_PER_TILE, dtype=I32)[None, :]
    valid = k_in_tile < nvalid[:, None]
    zero_chunk = loc // ROW_CHUNK - 1
    dump = nt * (loc // ROW_CHUNK) + (jnp.arange(n_tiles, dtype=I32)[:, None] % 2) * CHUNKS_PER_TILE + k_in_tile
    flat = lambda a: a.reshape(n_tiles * CHUNKS_PER_TILE)
    src_map = flat(jnp.where(valid, chunk_map, zero_chunk))
    dst_map = flat(jnp.where(valid, chunk_map, dump))

    tail_row = jnp.arange(nt, dtype=I32) * loc + used * ROW_CHUNK
    tail_n = loc // ROW_CHUNK - used
    as_i32 = lambda a: a.astype(I32)
    return tuple(map(as_i32, (tile_e, nused, src_map, dst_map, tail_row, tail_n)))


def kernel(x, norm1_g, w_in, conv_w, conv_b, w_gate_a, b_gate_a, w_gate_x, b_gate_x, lru_lambda,
           lru_out_g, q_norm_g, k_norm_g, lambda_q1, lambda_k1, lambda_q2, lambda_k2, sub_norm_g,
           w_out, norm2_g, w_router_group, b_router_group, w_router_expert, b_router_expert,
           w_expert_gate, w_expert_up, w_expert_down):
    b, s, d = x.shape
    assert d == D_MODEL and norm1_g.shape[0] == 1
    assert s % min(TQ, s) == 0 and s % min(TM_TOK, s) == 0 and min(TM_TOK, s) % CONV_BLOCK == 0
    t = b * s
    assert t % min(TM_PROJ, t) == 0
    l = 0
    x2 = x.reshape(t, d)

    scale = HEAD_DIM ** -0.5 * LOG2E
    qkg = jnp.stack([jnp.tile(q_norm_g[l], 2) * scale, jnp.tile(k_norm_g[l], 2)]).astype(F32)
    eye = jnp.eye(LRU_BLOCKS, dtype=F32)
    blockdiag = lambda w: jnp.einsum("ncd,nm->ncmd", w, eye).reshape(LRU_WIDTH, LRU_WIDTH)
    wgate = jnp.concatenate([blockdiag(w_gate_a[l]), blockdiag(w_gate_x[l])], axis=1).astype(BF16)
    bgate = jnp.concatenate([b_gate_a[l], b_gate_x[l]])[None, :]
    lam_params = jnp.stack([lambda_q1[l], lambda_k1[l], lambda_q2[l], lambda_k2[l]])
    wr = jnp.concatenate(
        [w_router_group[l], jnp.transpose(w_router_expert[l], (1, 0, 2)).reshape(d, N_EXPERTS)], axis=1)
    wr = jnp.pad(wr, ((0, 0), (0, LANES - wr.shape[1])))
    wr_hi = wr.astype(BF16)
    wr2 = jnp.concatenate([wr_hi, (wr - wr_hi.astype(F32)).astype(BF16)], axis=1)
    br = jnp.pad(jnp.concatenate([b_router_group[l], b_router_expert[l].reshape(-1)]),
                 (0, LANES - N_GROUPS - N_EXPERTS))[None, :]

    proj = _in_proj(x2, norm1_g[l][None, :], w_in[l], qkg)
    y_att = _attention(proj, b, s, lam_params, sub_norm_g[l][None, :])
    h1, xl, aux, gt = _out_route(proj, y_att, x2, s, w_out[l], norm2_g[l][None, :], wr2, br,
                                 conv_w[l], conv_b[l][None, :], wgate, bgate,
                                 lru_lambda[l][None, :], lru_out_g[l][None, :])

    tm = min(TM_TOK, s)
    nt = t // tm
    loc = 2 * tm + N_EXPERTS * ROW_CHUNK
    max_rows = 2 * t + nt * N_EXPERTS * (ROW_CHUNK - 1) + N_EXPERTS * (TM_EXP - ROW_CHUNK)
    n_tiles = -(-max_rows // TM_EXP)
    grp = gt[:, :, 0].astype(I32)
    tables = _expert_tables(grp, n_tiles, loc)
    yl = _experts(xl, tables, w_expert_gate[l], w_expert_up[l], w_expert_down[l], n_tiles)
    out = _combine(h1, aux, yl, nt, tm, loc)
    return out.reshape(b, s, d)
```
